```python
import math
import jax, jax.numpy as jnp
from jax import lax
import numpy as np

D_MODEL = 1024
BATCH = 8
SEQ = 4096
DEPTH = 2
DEC_BATCH = 16
DEC_SEQ = 64
PAST_LEN = 1024

CHUNK = 64
Q_BLOCK = 128
N_MEM = 256
MEM_HEADS = 4
MEM_HD = 64
MEM_W = MEM_HEADS * MEM_HD
MIX_W = D_MODEL - MEM_W
RET_HEADS = 6
RET_HD = MIX_W // RET_HEADS
RET_THETA = 10000.0
DIFF_HEADS = 6
DIFF_HD = MIX_W // (2 * DIFF_HEADS)
ROPE_THETA = 500000.0
ROT_DIM = DIFF_HD // 4
D_FF = -(-8 * D_MODEL // (3 * 256)) * 256
ALPHA = (2 * DEPTH) ** 0.25
BETA = (8 * DEPTH) ** -0.25
N_RET = (DEPTH + 1) // 2
N_DIFF = DEPTH // 2
LN_EPS = 1e-5
NEG_INF = -1e30

kernel_name = "chunk_causal_retention_diffattn_memory_encoder_step"


def _lambda_init(layer_idx):
    return 0.8 - 0.6 * math.exp(-0.3 * layer_idx)


def _layer_norm(x, g, b):
    xf = x.astype(jnp.float32)
    mu = jnp.mean(xf, axis=-1, keepdims=True)
    var = jnp.mean(jnp.square(xf - mu), axis=-1, keepdims=True)
    return ((xf - mu) * lax.rsqrt(var + LN_EPS) * g.astype(jnp.float32) + b.astype(jnp.float32)).astype(x.dtype)


def _post_norm(x, y, g, b):
    return _layer_norm(ALPHA * x + y, g, b)


def _rope(x, pos, rot_dim, theta):
    half = rot_dim // 2
    inv_freq = jnp.exp(-math.log(theta) * jnp.arange(half, dtype=jnp.float32) * 2.0 / rot_dim)
    ang = pos.astype(jnp.float32)[:, None] * inv_freq[None, :]
    bshape = (pos.shape[0],) + (1,) * (x.ndim - 3) + (half,)
    cos = jnp.cos(ang).reshape(bshape)
    sin = jnp.sin(ang).reshape(bshape)
    xf = x.astype(jnp.float32)
    x1 = xf[..., :half]
    x2 = xf[..., half:rot_dim]
    out = jnp.concatenate([x1 * cos - x2 * sin, x1 * sin + x2 * cos, xf[..., rot_dim:]], axis=-1)
    return out.astype(x.dtype)


def _memory_attention(mq, mem_k, mem_v):
    B, S = mq.shape[:2]
    s = jnp.einsum('bshd,bmhd->bhsm', mq, mem_k).astype(jnp.float32) * MEM_HD ** -0.5
    p = jax.nn.softmax(s, axis=-1)
    o = jnp.einsum('bhsm,bmhd->bshd', p, mem_v.astype(jnp.float32))
    return o.reshape(B, S, MEM_W)


def _retention_scan(q, k, v, r0):
    B, S, H, DK = q.shape
    DV = v.shape[-1]
    cl = min(S, CHUNK)
    nc = S // cl
    log_g = jnp.log(1.0 - jnp.exp2(-5.0 - jnp.arange(H, dtype=jnp.float32)))
    idx = jnp.arange(cl, dtype=jnp.float32)
    decay_in = jnp.exp(jnp.abs(idx[:, None] - idx[None, :])[None] * log_g[:, None, None])
    xi = jnp.exp((idx + 1.0)[:, None] * log_g[None, :])
    zeta = jnp.exp((cl - 1.0 - idx)[:, None] * log_g[None, :])
    g_chunk = jnp.exp(cl * log_g)

    def to_chunks(t):
        return t.reshape(B, nc, cl, H, t.shape[-1]).transpose(1, 0, 2, 3, 4)

    def step(r, qkv):
        qc, kc, vc = qkv
        inner = jnp.einsum('bihd,bjhd->bhij', qc, kc) * decay_in
        o = (jnp.einsum('bhij,bjhe->bihe', inner, vc)
             + jnp.einsum('bihd,bhde->bihe', qc, r) * xi[None, :, :, None])
        r = g_chunk[None, :, None, None] * r + jnp.einsum('bjhd,bjhe,jh->bhde', kc, vc, zeta)
        return r, o

    r, o = lax.scan(step, r0, (to_chunks(q), to_chunks(k), to_chunks(v)))
    return o.transpose(1, 0, 2, 3, 4).reshape(B, S, H, DV), r


def _retention_mixer(x, pos, r0, mem_k, mem_v, w_in, gn_g):
    B, S, _ = x.shape
    proj = x @ w_in
    q, k, v, g, mq = jnp.split(proj, [MIX_W, 2 * MIX_W, 3 * MIX_W, 4 * MIX_W], axis=-1)
    q = _rope(q.reshape(B, S, RET_HEADS, RET_HD), pos, RET_HD, RET_THETA).astype(jnp.float32)
    k = _rope(k.reshape(B, S, RET_HEADS, RET_HD), pos, RET_HD, RET_THETA).astype(jnp.float32) * RET_HD ** -0.5
    v = v.reshape(B, S, RET_HEADS, RET_HD).astype(jnp.float32)
    o, r_new = _retention_scan(q, k, v, r0.astype(jnp.float32))
    mu = jnp.mean(o, axis=-1, keepdims=True)
    var = jnp.mean(jnp.square(o - mu), axis=-1, keepdims=True)
    o = ((o - mu) * lax.rsqrt(var + LN_EPS)).reshape(B, S, MIX_W) * gn_g.astype(jnp.float32)
    o = o * jax.nn.silu(g.astype(jnp.float32))
    m = _memory_attention(mq.reshape(B, S, MEM_HEADS, MEM_HD), mem_k, mem_v)
    return jnp.concatenate([o, m], axis=-1).astype(x.dtype), r_new


def _diff_project(x, pos, w_in):
    B, S, _ = x.shape
    proj = x @ w_in
    q, k, v, mq = jnp.split(proj, [MIX_W, 2 * MIX_W, 3 * MIX_W], axis=-1)
    q = _rope(q.reshape(B, S, DIFF_HEADS, 2, DIFF_HD), pos, ROT_DIM, ROPE_THETA)
    k = _rope(k.reshape(B, S, DIFF_HEADS, 2, DIFF_HD), pos, ROT_DIM, ROPE_THETA)
    v = v.reshape(B, S, DIFF_HEADS, 2 * DIFF_HD)
    mq = mq.reshape(B, S, MEM_HEADS, MEM_HD)
    return q, k, v, mq


def _diff_lambda(lq1, lk1, lq2, lk2, lam_init):
    f = jnp.float32
    return (jnp.exp(jnp.sum(lq1.astype(f) * lk1.astype(f)))
            - jnp.exp(jnp.sum(lq2.astype(f) * lk2.astype(f))) + lam_init)


def _diff_weighted_values(s, lam, vf):
    p = jax.nn.softmax(s, axis=-1)
    a = p[:, :, 0] - lam * p[:, :, 1]
    return jnp.einsum('bhqk,bkhe->bqhe', a, vf)


def _diff_attention_prompt(q, k, v, lam):
    B, S = q.shape[:2]
    nb = S // Q_BLOCK
    qb = q.reshape(B, nb, Q_BLOCK, DIFF_HEADS, 2, DIFF_HD).transpose(1, 0, 2, 3, 4, 5)
    key_chunk = jnp.arange(S) // CHUNK
    vf = v.astype(jnp.float32)

    def block(args):
        q_blk, b_idx = args
        s = jnp.einsum('bqhcd,bkhcd->bhcqk', q_blk, k).astype(jnp.float32) * DIFF_HD ** -0.5
        q_chunk = (b_idx * Q_BLOCK + jnp.arange(Q_BLOCK)) // CHUNK
        mask = key_chunk[None, :] <= q_chunk[:, None]
        s = jnp.where(mask, s, NEG_INF)
        return _diff_weighted_values(s, lam, vf)

    o = lax.map(block, (qb, jnp.arange(nb)))
    return o.transpose(1, 0, 2, 3, 4).reshape(B, S, DIFF_HEADS, 2 * DIFF_HD)


def _diff_attention_sample(q, k_all, v_all, lam):
    s = jnp.einsum('bqhcd,bkhcd->bhcqk', q, k_all).astype(jnp.float32) * DIFF_HD ** -0.5
    return _diff_weighted_values(s, lam, v_all.astype(jnp.float32))


def _diff_head_norm(o, subln_g, lam_init):
    B, S = o.shape[:2]
    ms = jnp.mean(jnp.square(o), axis=-1, keepdims=True)
    o = o * lax.rsqrt(ms + LN_EPS) * subln_g.astype(jnp.float32) * (1.0 - lam_init)
    return o.reshape(B, S, MIX_W)


def _swiglu(x, w_gate, w_up, w_down):
    return (jax.nn.silu(x @ w_gate) * (x @ w_up)) @ w_down


def setup_inputs(seed: int = 0) -> dict:
    key = jax.random.key(seed)
    ks = jax.random.split(key, 32)
    f = jnp.float32

    def nrm(k, shape, scale):
        return jax.random.normal(k, shape, f) * scale

    d_is = D_MODEL ** -0.5
    ret_cols = 4 * MIX_W + MEM_W
    ret_col_scale = jnp.ones((ret_cols,), f).at[2 * MIX_W:3 * MIX_W].set(BETA)
    diff_cols = 3 * MIX_W + MEM_W
    diff_col_scale = jnp.ones((diff_cols,), f).at[2 * MIX_W:3 * MIX_W].set(BETA)
    mem_col_scale = jnp.ones((2 * MEM_W,), f).at[MEM_W:].set(BETA)
    return {
        "x_prompt": nrm(ks[0], (BATCH, SEQ, D_MODEL), 1.0),
        "x_sample": nrm(ks[1], (DEC_BATCH, DEC_SEQ, D_MODEL), 1.0),
        "mem_prompt": nrm(ks[2], (BATCH, N_MEM, D_MODEL), 1.0),
        "cache_ret_state": nrm(ks[3], (N_RET, DEC_BATCH, RET_HEADS, RET_HD, RET_HD), 0.5),
        "cache_diff_k": nrm(ks[4], (N_DIFF, DEC_BATCH, PAST_LEN, DIFF_HEADS, 2 * DIFF_HD), 1.0),
        "cache_diff_v": nrm(ks[5], (N_DIFF, DEC_BATCH, PAST_LEN, DIFF_HEADS, 2 * DIFF_HD), BETA),
        "cache_mem_k": nrm(ks[6], (DEPTH, DEC_BATCH, N_MEM, MEM_HEADS, MEM_HD), 1.0),
        "cache_mem_v": nrm(ks[7], (DEPTH, DEC_BATCH, N_MEM, MEM_HEADS, MEM_HD), BETA),
        "ret_w_in": nrm(ks[8], (N_RET, D_MODEL, ret_cols), d_is) * ret_col_scale,
        "ret_gn_g": 1.0 + nrm(ks[9], (N_RET, MIX_W), 0.02),
        "diff_w_in": nrm(ks[10], (N_DIFF, D_MODEL, diff_cols), d_is) * diff_col_scale,
        "diff_lambda_q1": nrm(ks[11], (N_DIFF, DIFF_HD), 0.1),
        "diff_lambda_k1": nrm(ks[12], (N_DIFF, DIFF_HD), 0.1),
        "diff_lambda_q2": nrm(ks[13], (N_DIFF, DIFF_HD), 0.1),
        "diff_lambda_k2": nrm(ks[14], (N_DIFF, DIFF_HD), 0.1),
        "diff_subln_g": 1.0 + nrm(ks[15], (N_DIFF, 2 * DIFF_HD), 0.02),
        "w_mem_kv": nrm(ks[16], (DEPTH, D_MODEL, 2 * MEM_W), d_is) * mem_col_scale,
        "w_o": nrm(ks[17], (DEPTH, D_MODEL, D_MODEL), d_is * BETA),
        "ln1_g": 1.0 + nrm(ks[18], (DEPTH, D_MODEL), 0.02),
        "ln1_b": nrm(ks[19], (DEPTH, D_MODEL), 0.02),
        "w_gate": nrm(ks[20], (DEPTH, D_MODEL, D_FF), d_is),
        "w_up": nrm(ks[21], (DEPTH, D_MODEL, D_FF), d_is),
        "w_down": nrm(ks[22], (DEPTH, D_FF, D_MODEL), D_FF ** -0.5 * BETA),
        "ln2_g": 1.0 + nrm(ks[23], (DEPTH, D_MODEL), 0.02),
        "ln2_b": nrm(ks[24], (DEPTH, D_MODEL), 0.02),
    }


def reference(x_prompt, x_sample, mem_prompt, cache_ret_state, cache_diff_k, cache_diff_v,
              cache_mem_k, cache_mem_v, ret_w_in, ret_gn_g, diff_w_in, diff_lambda_q1,
              diff_lambda_k1, diff_lambda_q2, diff_lambda_k2, diff_subln_g, w_mem_kv, w_o,
              ln1_g, ln1_b, w_gate, w_up, w_down, ln2_g, ln2_b):
    Bp, Sp, _ = x_prompt.shape
    Bs, Ss, _ = x_sample.shape
    pos_p = jnp.arange(Sp)
    pos_s = PAST_LEN + jnp.arange(Ss)
    xp, xs = x_prompt, x_sample
    ret_p, ret_s, dkp, dvp, dks, dvs, mkp, mvp = [], [], [], [], [], [], [], []
    for i in range(DEPTH):
        j = i // 2
        mk_p, mv_p = jnp.split(mem_prompt @ w_mem_kv[i], 2, axis=-1)
        mk_p = mk_p.reshape(Bp, N_MEM, MEM_HEADS, MEM_HD)
        mv_p = mv_p.reshape(Bp, N_MEM, MEM_HEADS, MEM_HD)
        mkp.append(mk_p)
        mvp.append(mv_p)
        mk_s, mv_s = cache_mem_k[i], cache_mem_v[i]
        if i % 2 == 0:
            r0 = jnp.zeros((Bp, RET_HEADS, RET_HD, RET_HD), jnp.float32)
            hp, rp = _retention_mixer(xp, pos_p, r0, mk_p, mv_p, ret_w_in[j], ret_gn_g[j])
            hs, rs = _retention_mixer(xs, pos_s, cache_ret_state[j], mk_s, mv_s, ret_w_in[j], ret_gn_g[j])
            ret_p.append(rp.astype(xp.dtype))
            ret_s.append(rs.astype(xs.dtype))
        else:
            lam_init = _lambda_init(i)
            lam = _diff_lambda(diff_lambda_q1[j], diff_lambda_k1[j], diff_lambda_q2[j], diff_lambda_k2[j], lam_init)
            q, k, v, mq = _diff_project(xp, pos_p, diff_w_in[j])
            o = _diff_attention_prompt(q, k, v, lam)
            hp = jnp.concatenate([_diff_head_norm(o, diff_subln_g[j], lam_init),
                                  _memory_attention(mq, mk_p, mv_p)], axis=-1).astype(xp.dtype)
            dkp.append(k.reshape(Bp, Sp, DIFF_HEADS, 2 * DIFF_HD))
            dvp.append(v)
            q, k, v, mq = _diff_project(xs, pos_s, diff_w_in[j])
            k_all = jnp.concatenate(
                [cache_diff_k[j].reshape(Bs, PAST_LEN, DIFF_HEADS, 2, DIFF_HD).astype(k.dtype), k], axis=1)
            v_all = jnp.concatenate([cache_diff_v[j].astype(v.dtype), v], axis=1)
            o = _diff_attention_sample(q, k_all, v_all, lam)
            hs = jnp.concatenate([_diff_head_norm(o, diff_subln_g[j], lam_init),
                                  _memory_attention(mq, mk_s, mv_s)], axis=-1).astype(xs.dtype)
            dks.append(k.reshape(Bs, Ss, DIFF_HEADS, 2 * DIFF_HD))
            dvs.append(v)
        xp = _post_norm(xp, hp @ w_o[i], ln1_g[i], ln1_b[i])
        xp = _post_norm(xp, _swiglu(xp, w_gate[i], w_up[i], w_down[i]), ln2_g[i], ln2_b[i])
        xs = _post_norm(xs, hs @ w_o[i], ln1_g[i], ln1_b[i])
        xs = _post_norm(xs, _swiglu(xs, w_gate[i], w_up[i], w_down[i]), ln2_g[i], ln2_b[i])
    return (xp, xs, jnp.stack(ret_p), jnp.stack(ret_s), jnp.stack(dkp), jnp.stack(dvp),
            jnp.stack(dks), jnp.stack(dvs), jnp.stack(mkp), jnp.stack(mvp))
```

```python
import functools
import math

import jax
import jax.numpy as jnp
import numpy as np
from jax import lax
from jax.experimental import pallas as pl
from jax.experimental.pallas import tpu as pltpu

D_MODEL = 1024
DEPTH = 2
PAST_LEN = 1024
CHUNK = 64
N_MEM = 256
MEM_HEADS = 4
MEM_HD = 64
MEM_W = MEM_HEADS * MEM_HD
MIX_W = D_MODEL - MEM_W
RET_HEADS = 6
RET_HD = MIX_W // RET_HEADS
RET_THETA = 10000.0
DIFF_HEADS = 6
DIFF_HD = MIX_W // (2 * DIFF_HEADS)
ROPE_THETA = 500000.0
ROT_DIM = DIFF_HD // 4
D_FF = -(-8 * D_MODEL // (3 * 256)) * 256
ALPHA = (2 * DEPTH) ** 0.25
LN_EPS = 1e-5
NEG_INF = -1e30

HEAD_W = 128
VMEM_LIMIT = 56 * 1024 * 1024

F32 = jnp.float32
BF16 = jnp.bfloat16


def _lambda_init(layer_idx):
    return 0.8 - 0.6 * math.exp(-0.3 * layer_idx)


def _dot(a, b):
    return jnp.dot(a.astype(BF16), b.astype(BF16), preferred_element_type=F32)


def _dot_nt(a, b):
    return lax.dot_general(a.astype(BF16), b.astype(BF16), (((1,), (1,)), ((), ())),
                           preferred_element_type=F32)


def _dot_tn(a, b):
    return lax.dot_general(a.astype(BF16), b.astype(BF16), (((0,), (0,)), ((), ())),
                           preferred_element_type=F32)


def _const_spec(shape):
    return pl.BlockSpec(shape, lambda *_: (0,) * len(shape), pipeline_mode=pl.Buffered(1))


def _params(*semantics):
    return pltpu.CompilerParams(dimension_semantics=semantics, vmem_limit_bytes=VMEM_LIMIT)


def _layer_norm_rows(y, g, b):
    mu = jnp.mean(y, axis=-1, keepdims=True)
    d = y - mu
    var = jnp.mean(d * d, axis=-1, keepdims=True)
    return d * lax.rsqrt(var + LN_EPS) * g + b


def _softmax_rows(s):
    s = s - jnp.max(s, axis=-1, keepdims=True)
    p = jnp.exp(s)
    return p / jnp.sum(p, axis=-1, keepdims=True)


def _memory_attention(mq, mk, mv):
    lane = lax.broadcasted_iota(jnp.int32, mk.shape, 1)
    mqb = (mq * MEM_HD ** -0.5).astype(BF16)
    out = None
    for h in range(MEM_HEADS):
        sel = (lane >= h * MEM_HD) & (lane < (h + 1) * MEM_HD)
        p = _softmax_rows(_dot_nt(mqb, jnp.where(sel, mk, 0.0)))
        o = _dot(p, jnp.where(sel, mv, 0.0))
        out = o if out is None else out + o
    return out


def _mem_kv_kernel(x_ref, w_ref, k_ref, v_ref):
    kv = _dot(x_ref[...], w_ref[0])
    k_ref[0] = kv[:, :MEM_W]
    v_ref[0] = kv[:, MEM_W:]


def _mem_kv(mem_flat, w_bf):
    m = mem_flat.shape[0]
    tm = min(m, 512)
    out = jax.ShapeDtypeStruct((DEPTH, m, MEM_W), F32)
    return pl.pallas_call(
        _mem_kv_kernel,
        grid=(DEPTH, m // tm),
        in_specs=[pl.BlockSpec((tm, D_MODEL), lambda l, i: (i, 0)),
                  pl.BlockSpec((1, D_MODEL, 2 * MEM_W), lambda l, i: (l, 0, 0))],
        out_specs=[pl.BlockSpec((1, tm, MEM_W), lambda l, i: (l, i, 0))] * 2,
        out_shape=[out, out],
        compiler_params=_params("arbitrary", "arbitrary"),
    )(mem_flat, w_bf)


def _ret_tables(tile):
    h = np.arange(RET_HEADS, dtype=np.float64)
    log_g = np.log(1.0 - np.exp2(-5.0 - h))
    idx = np.arange(tile, dtype=np.float64)
    dist = np.abs(idx[:, None] - idx[None, :])
    visible = (idx[None, :] // CHUNK) <= (idx[:, None] // CHUNK)
    decay = np.where(visible[None], np.exp(dist[None] * log_g[:, None, None]), 0.0)
    xi = np.exp((idx + 1.0)[None, :] * log_g[:, None])[:, :, None]
    zeta = np.exp((tile - 1.0 - idx)[None, :] * log_g[:, None])[:, :, None]
    g_tile = np.exp(tile * log_g)
    f = lambda a: jnp.asarray(a, F32)
    return f(decay), f(xi), f(zeta), [float(g) for g in g_tile]


def _ret_kernel(x_ref, w_ref, cos_ref, sin_ref, r0_ref, mk_ref, mv_ref, gn_ref, decay_ref, xi_ref,
                zeta_ref, o_ref, m_ref, r_ref, state_ref, *, tile, g_tile):
    step = pl.program_id(1)

    @pl.when(step == 0)
    def _():
        state_ref[...] = r0_ref[0]

    proj = _dot(x_ref[0], w_ref[...])
    t = proj.shape[0]
    cos = cos_ref[...]
    sin = sin_ref[...]

    def rope(a):
        return a * cos + pltpu.roll(a, RET_HD // 2, 1) * sin

    for h in range(RET_HEADS):
        col = h * HEAD_W
        q = rope(proj[:, col:col + HEAD_W])
        k = rope(proj[:, MIX_W + col:MIX_W + col + HEAD_W]) * RET_HD ** -0.5
        v = proj[:, 2 * MIX_W + col:2 * MIX_W + col + HEAD_W]
        gate = proj[:, 3 * MIX_W + col:3 * MIX_W + col + HEAD_W]
        outs = []
        r = state_ref[h]
        for c in range(t // tile):
            rows = slice(c * tile, (c + 1) * tile)
            qc, kc, vc = q[rows], k[rows], v[rows]
            inner = _dot_nt(qc, kc) * decay_ref[h]
            outs.append(_dot(inner, vc) + _dot(qc, r) * xi_ref[h])
            r = g_tile[h] * r + _dot_tn(kc * zeta_ref[h], vc)
        state_ref[h] = r
        o = outs[0] if len(outs) == 1 else jnp.concatenate(outs, axis=0)
        mu = jnp.mean(o, axis=-1, keepdims=True)
        d = o - mu
        var = jnp.mean(d * d, axis=-1, keepdims=True)
        o = d * lax.rsqrt(var + LN_EPS) * gn_ref[:, col:col + HEAD_W]
        o = o * (gate / (1.0 + jnp.exp(-gate)))
        o_ref[0, :, col:col + HEAD_W] = o.astype(o_ref.dtype)

    m = _memory_attention(proj[:, 4 * MIX_W:], mk_ref[0], mv_ref[0])
    m_ref[0] = m.astype(m_ref.dtype)

    @pl.when(step == pl.num_programs(1) - 1)
    def _():
        r_ref[0] = state_ref[...]


def _ret_mixer(x, pos, r0, mk, mv, w_bf, gn_g, *, t_step, tile):
    b, s, _ = x.shape
    half = RET_HD // 2
    inv_freq = jnp.exp(-math.log(RET_THETA) * jnp.arange(half, dtype=F32) * 2.0 / RET_HD)
    ang = pos.astype(F32)[:, None] * inv_freq[None, :]
    cos = jnp.concatenate([jnp.cos(ang), jnp.cos(ang)], axis=-1)
    sin = jnp.concatenate([-jnp.sin(ang), jnp.sin(ang)], axis=-1)
    decay, xi, zeta, g_tile = _ret_tables(tile)
    cols = w_bf.shape[1]
    kern = functools.partial(_ret_kernel, tile=tile, g_tile=g_tile)
    return pl.pallas_call(
        kern,
        grid=(b, s // t_step),
        in_specs=[
            pl.BlockSpec((1, t_step, D_MODEL), lambda i, j: (i, j, 0)),
            _const_spec((D_MODEL, cols)),
            pl.BlockSpec((t_step, HEAD_W), lambda i, j: (j, 0)),
            pl.BlockSpec((t_step, HEAD_W), lambda i, j: (j, 0)),
            pl.BlockSpec((1, RET_HEADS, RET_HD, RET_HD), lambda i, j: (i, 0, 0, 0)),
            pl.BlockSpec((1, N_MEM, MEM_W), lambda i, j: (i, 0, 0)),
            pl.BlockSpec((1, N_MEM, MEM_W), lambda i, j: (i, 0, 0)),
            _const_spec((1, MIX_W)),
            _const_spec((RET_HEADS, tile, tile)),
            _const_spec((RET_HEADS, tile, 1)),
            _const_spec((RET_HEADS, tile, 1)),
        ],
        out_specs=[
            pl.BlockSpec((1, t_step, MIX_W), lambda i, j: (i, j, 0)),
            pl.BlockSpec((1, t_step, MEM_W), lambda i, j: (i, j, 0)),
            pl.BlockSpec((1, RET_HEADS, RET_HD, RET_HD), lambda i, j: (i, 0, 0, 0)),
        ],
        out_shape=[
            jax.ShapeDtypeStruct((b, s, MIX_W), BF16),
            jax.ShapeDtypeStruct((b, s, MEM_W), BF16),
            jax.ShapeDtypeStruct((b, RET_HEADS, RET_HD, RET_HD), F32),
        ],
        scratch_shapes=[pltpu.VMEM((RET_HEADS, RET_HD, RET_HD), F32)],
        compiler_params=_params("arbitrary", "arbitrary"),
    )(x, w_bf, cos, sin, r0, mk, mv, gn_g.reshape(1, MIX_W), decay, xi, zeta)


def _diff_proj_kernel(x_ref, w_ref, c_ref, sa_ref, sb_ref, mk_ref, mv_ref, q_ref, k_ref, v_ref, m_ref):
    proj = _dot(x_ref[0], w_ref[...])
    c = c_ref[...]
    sa = sa_ref[...]
    sb = sb_ref[...]

    def rope(a):
        return (a * c + pltpu.roll(a, ROT_DIM // 2, 1) * sa
                + pltpu.roll(a, HEAD_W - ROT_DIM // 2, 1) * sb)

    for h in range(DIFF_HEADS):
        col = h * HEAD_W
        q = rope(proj[:, col:col + HEAD_W]) * DIFF_HD ** -0.5
        q_ref[0, :, col:col + HEAD_W] = q.astype(q_ref.dtype)
        k_ref[0, :, col:col + HEAD_W] = rope(proj[:, MIX_W + col:MIX_W + col + HEAD_W])
    v_ref[0] = proj[:, 2 * MIX_W:3 * MIX_W]
    m = _memory_attention(proj[:, 3 * MIX_W:], mk_ref[0], mv_ref[0])
    m_ref[0] = m.astype(m_ref.dtype)


def _diff_project(x, pos, mk, mv, w_bf, *, t_step):
    b, s, _ = x.shape
    half = ROT_DIM // 2
    inv_freq = jnp.exp(-math.log(ROPE_THETA) * jnp.arange(half, dtype=F32) * 2.0 / ROT_DIM)
    ang = pos.astype(F32)[:, None] * inv_freq[None, :]
    cos, sin = jnp.cos(ang), jnp.sin(ang)
    ones = jnp.ones((s, DIFF_HD - ROT_DIM), F32)
    zeros_rest = jnp.zeros((s, DIFF_HD - ROT_DIM), F32)
    zeros_half = jnp.zeros((s, half), F32)
    c64 = jnp.concatenate([cos, cos, ones], axis=-1)
    sa64 = jnp.concatenate([zeros_half, sin, zeros_rest], axis=-1)
    sb64 = jnp.concatenate([-sin, zeros_half, zeros_rest], axis=-1)
    tables = [jnp.concatenate([a, a], axis=-1) for a in (c64, sa64, sb64)]
    cols = w_bf.shape[1]
    tab_spec = pl.BlockSpec((t_step, HEAD_W), lambda i, j: (j, 0))
    tok_spec = lambda w: pl.BlockSpec((1, t_step, w), lambda i, j: (i, j, 0))
    mem_spec = pl.BlockSpec((1, N_MEM, MEM_W), lambda i, j: (i, 0, 0))
    return pl.pallas_call(
        _diff_proj_kernel,
        grid=(b, s // t_step),
        in_specs=[tok_spec(D_MODEL), _const_spec((D_MODEL, cols)), tab_spec, tab_spec, tab_spec,
                  mem_spec, mem_spec],
        out_specs=[tok_spec(MIX_W), tok_spec(MIX_W), tok_spec(MIX_W), tok_spec(MEM_W)],
        out_shape=[
            jax.ShapeDtypeStruct((b, s, MIX_W), BF16),
            jax.ShapeDtypeStruct((b, s, MIX_W), F32),
            jax.ShapeDtypeStruct((b, s, MIX_W), F32),
            jax.ShapeDtypeStruct((b, s, MEM_W), BF16),
        ],
        compiler_params=_params("arbitrary", "arbitrary"),
    )(x, w_bf, *tables, mk, mv)


def _diff_lambda(lq1_ref, lk1_ref, lq2_ref, lk2_ref, lam_init):
    a = jnp.sum(lq1_ref[...] * lk1_ref[...], axis=-1, keepdims=True)
    b = jnp.sum(lq2_ref[...] * lk2_ref[...], axis=-1, keepdims=True)
    return jnp.exp(a) - jnp.exp(b) + lam_init


def _diff_finish(o, g_ref, lam_init):
    ms = jnp.mean(o * o, axis=-1, keepdims=True)
    return o * lax.rsqrt(ms + LN_EPS) * g_ref[...] * (1.0 - lam_init)


def _split_heads(k):
    lane = lax.broadcasted_iota(jnp.int32, k.shape, 1)
    first = lane < DIFF_HD
    return jnp.where(first, k, 0.0).astype(BF16), jnp.where(first, 0.0, k).astype(BF16)


def _diff_attn_prompt_kernel(q_ref, k_ref, v_ref, lq1_ref, lk1_ref, lq2_ref, lk2_ref, g_ref, o_ref,
                             k1_ref, k2_ref, vb_ref, *, tq, lam_init):
    qi = pl.program_id(2)

    @pl.when(qi == 0)
    def _():
        k1, k2 = _split_heads(k_ref[0])
        k1_ref[...] = k1
        k2_ref[...] = k2
        vb_ref[...] = v_ref[0].astype(BF16)

    q = q_ref[0]

    def update(carry, s, vt):
        m, l, acc = carry
        m_new = jnp.maximum(m, jnp.max(s, axis=-1, keepdims=True))
        scale = jnp.exp(m - m_new)
        p = jnp.exp(s - m_new)
        l = scale * l + jnp.sum(p, axis=-1, keepdims=True)
        acc = scale * acc + _dot(p, vt)
        return m_new, l, acc

    def tile(j, carry, masked):
        rows = pl.ds(pl.multiple_of(j * tq, tq), tq)
        s1 = _dot_nt(q, k1_ref[rows, :])
        s2 = _dot_nt(q, k2_ref[rows, :])
        if masked:
            r = lax.broadcasted_iota(jnp.int32, (tq, tq), 0) // CHUNK
            c = lax.broadcasted_iota(jnp.int32, (tq, tq), 1) // CHUNK
            s1 = jnp.where(c <= r, s1, NEG_INF)
            s2 = jnp.where(c <= r, s2, NEG_INF)
        vt = vb_ref[rows, :]
        return update(carry[0], s1, vt), update(carry[1], s2, vt)

    init_one = (jnp.full((tq, 1), NEG_INF, F32), jnp.zeros((tq, 1), F32), jnp.zeros((tq, HEAD_W), F32))
    carry = lax.fori_loop(0, qi, lambda j, cr: tile(j, cr, False), (init_one, init_one))
    (_, l1, acc1), (_, l2, acc2) = tile(qi, carry, True)
    lam = _diff_lambda(lq1_ref, lk1_ref, lq2_ref, lk2_ref, lam_init)
    o = acc1 / l1 - lam * (acc2 / l2)
    o_ref[0] = _diff_finish(o, g_ref, lam_init).astype(o_ref.dtype)


def _diff_attn_prompt(q, k, v, lam_vecs, subln_g, lam_init, *, tq):
    b, s, _ = q.shape
    kern = functools.partial(_diff_attn_prompt_kernel, tq=tq, lam_init=lam_init)
    vec_spec = _const_spec((1, DIFF_HD))
    kv_spec = pl.BlockSpec((1, s, HEAD_W), lambda i, h, j: (i, 0, h))
    q_spec = pl.BlockSpec((1, tq, HEAD_W), lambda i, h, j: (i, j, h))
    return pl.pallas_call(
        kern,
        grid=(b, DIFF_HEADS, s // tq),
        in_specs=[q_spec, kv_spec, kv_spec, vec_spec, vec_spec, vec_spec, vec_spec,
                  _const_spec((1, HEAD_W))],
        out_specs=q_spec,
        out_shape=jax.ShapeDtypeStruct((b, s, MIX_W), BF16),
        scratch_shapes=[pltpu.VMEM((s, HEAD_W), BF16)] * 3,
        compiler_params=_params("arbitrary", "arbitrary", "arbitrary"),
    )(q, k, v, *lam_vecs, subln_g.reshape(1, HEAD_W))


def _diff_attn_sample_kernel(q_ref, kc_ref, vc_ref, kn_ref, vn_ref, lq1_ref, lk1_ref, lq2_ref, lk2_ref,
                             g_ref, o_ref, *, lam_init):
    q = q_ref[0]
    kc1, kc2 = _split_heads(kc_ref[0])
    kn1, kn2 = _split_heads(kn_ref[0])
    vc = vc_ref[0]
    vn = vn_ref[0]

    def one_map(kc, kn):
        sc = _dot_nt(q, kc)
        sn = _dot_nt(q, kn)
        m = jnp.maximum(jnp.max(sc, axis=-1, keepdims=True), jnp.max(sn, axis=-1, keepdims=True))
        pc = jnp.exp(sc - m)
        pn = jnp.exp(sn - m)
        l = jnp.sum(pc, axis=-1, keepdims=True) + jnp.sum(pn, axis=-1, keepdims=True)
        return (_dot(pc, vc) + _dot(pn, vn)) / l

    lam = _diff_lambda(lq1_ref, lk1_ref, lq2_ref, lk2_ref, lam_init)
    o = one_map(kc1, kn1) - lam * one_map(kc2, kn2)
    o_ref[0] = _diff_finish(o, g_ref, lam_init).astype(o_ref.dtype)


def _diff_attn_sample(q, k_cache, v_cache, k_new, v_new, lam_vecs, subln_g, lam_init):
    b, s, _ = q.shape
    past = k_cache.shape[1]
    kern = functools.partial(_diff_attn_sample_kernel, lam_init=lam_init)
    vec_spec = _const_spec((1, DIFF_HD))
    new_spec = pl.BlockSpec((1, s, HEAD_W), lambda i, h: (i, 0, h))
    old_spec = pl.BlockSpec((1, past, HEAD_W), lambda i, h: (i, 0, h))
    return pl.pallas_call(
        kern,
        grid=(b, DIFF_HEADS),
        in_specs=[new_spec, old_spec, old_spec, new_spec, new_spec, vec_spec, vec_spec, vec_spec,
                  vec_spec, _const_spec((1, HEAD_W))],
        out_specs=new_spec,
        out_shape=jax.ShapeDtypeStruct((b, s, MIX_W), BF16),
        compiler_params=_params("arbitrary", "arbitrary"),
    )(q, k_cache, v_cache, k_new, v_new, *lam_vecs, subln_g.reshape(1, HEAD_W))


def _post_kernel(x_ref, o_ref, m_ref, wo_ref, g1_ref, b1_ref, wg_ref, wu_ref, wd_ref, g2_ref, b2_ref,
                 y_ref, h_ref, *, ff_chunk):
    mixed = _dot(o_ref[...], wo_ref[:MIX_W, :]) + _dot(m_ref[...], wo_ref[MIX_W:, :])
    x1 = _layer_norm_rows(ALPHA * x_ref[...] + mixed, g1_ref[...], b1_ref[...])
    x1b = x1.astype(BF16)
    for c in range(D_FF // ff_chunk):
        cols = slice(c * ff_chunk, (c + 1) * ff_chunk)
        gate = _dot(x1b, wg_ref[:, cols])
        up = _dot(x1b, wu_ref[:, cols])
        h_ref[:, cols] = (gate / (1.0 + jnp.exp(-gate)) * up).astype(BF16)
    ff = _dot(h_ref[...], wd_ref[...])
    y_ref[...] = _layer_norm_rows(ALPHA * x1 + ff, g2_ref[...], b2_ref[...])


def _post(x, o, m, wo, g1, b1, wg, wu, wd, g2, b2, *, tm, ff_chunk):
    n = x.shape[0]
    row = lambda w: pl.BlockSpec((tm, w), lambda i: (i, 0))
    vec = _const_spec((1, D_MODEL))
    kern = functools.partial(_post_kernel, ff_chunk=ff_chunk)
    r1 = lambda a: a.reshape(1, D_MODEL)
    return pl.pallas_call(
        kern,
        grid=(n // tm,),
        in_specs=[row(D_MODEL), row(MIX_W), row(MEM_W), _const_spec((D_MODEL, D_MODEL)), vec, vec,
                  _const_spec((D_MODEL, D_FF)), _const_spec((D_MODEL, D_FF)),
                  _const_spec((D_FF, D_MODEL)), vec, vec],
        out_specs=row(D_MODEL),
        out_shape=jax.ShapeDtypeStruct((n, D_MODEL), F32),
        scratch_shapes=[pltpu.VMEM((tm, D_FF), BF16)],
        compiler_params=_params("arbitrary"),
    )(x, o, m, wo, r1(g1), r1(b1), wg, wu, wd, r1(g2), r1(b2))


def kernel(x_prompt, x_sample, mem_prompt, cache_ret_state, cache_diff_k, cache_diff_v, cache_mem_k,
           cache_mem_v, ret_w_in, ret_gn_g, diff_w_in, diff_lambda_q1, diff_lambda_k1, diff_lambda_q2,
           diff_lambda_k2, diff_subln_g, w_mem_kv, w_o, ln1_g, ln1_b, w_gate, w_up, w_down, ln2_g, ln2_b):
    bp, sp, _ = x_prompt.shape
    bs, ss, _ = x_sample.shape
    pos_p = jnp.arange(sp)
    pos_s = PAST_LEN + jnp.arange(ss)
    bf = lambda a: a.astype(BF16)

    mem_k_p, mem_v_p = _mem_kv(mem_prompt.reshape(bp * N_MEM, D_MODEL), bf(w_mem_kv))
    mem_k_p = mem_k_p.reshape(DEPTH, bp, N_MEM, MEM_W)
    mem_v_p = mem_v_p.reshape(DEPTH, bp, N_MEM, MEM_W)
    mem_k_s = cache_mem_k.reshape(DEPTH, bs, N_MEM, MEM_W)
    mem_v_s = cache_mem_v.reshape(DEPTH, bs, N_MEM, MEM_W)

    xp, xs = x_prompt, x_sample
    ret_p, ret_s, dkp, dvp, dks, dvs = [], [], [], [], [], []
    for i in range(DEPTH):
        j = i // 2
        if i % 2 == 0:
            w_in = bf(ret_w_in[j])
            r0 = jnp.zeros((bp, RET_HEADS, RET_HD, RET_HD), F32)
            op, mp, rp = _ret_mixer(xp, pos_p, r0, mem_k_p[i], mem_v_p[i], w_in, ret_gn_g[j],
                                    t_step=512, tile=256)
            os_, ms, rs = _ret_mixer(xs, pos_s, cache_ret_state[j], mem_k_s[i], mem_v_s[i], w_in,
                                     ret_gn_g[j], t_step=ss, tile=ss)
            ret_p.append(rp)
            ret_s.append(rs)
        else:
            lam_init = _lambda_init(i)
            w_in = bf(diff_w_in[j])
            lam_vecs = [a[j].reshape(1, DIFF_HD) for a in
                        (diff_lambda_q1, diff_lambda_k1, diff_lambda_q2, diff_lambda_k2)]
            q, k, v, mp = _diff_project(xp, pos_p, mem_k_p[i], mem_v_p[i], w_in, t_step=512)
            op = _diff_attn_prompt(q, k, v, lam_vecs, diff_subln_g[j], lam_init, tq=256)
            dkp.append(k.reshape(bp, sp, DIFF_HEADS, HEAD_W))
            dvp.append(v.reshape(bp, sp, DIFF_HEADS, HEAD_W))
            q, k, v, ms = _diff_project(xs, pos_s, mem_k_s[i], mem_v_s[i], w_in, t_step=ss)
            os_ = _diff_attn_sample(q, cache_diff_k[j].reshape(bs, PAST_LEN, MIX_W),
                                    cache_diff_v[j].reshape(bs, PAST_LEN, MIX_W), k, v, lam_vecs,
                                    diff_subln_g[j], lam_init)
            dks.append(k.reshape(bs, ss, DIFF_HEADS, HEAD_W))
            dvs.append(v.reshape(bs, ss, DIFF_HEADS, HEAD_W))
        post = functools.partial(_post, wo=bf(w_o[i]), g1=ln1_g[i], b1=ln1_b[i], wg=bf(w_gate[i]),
                                 wu=bf(w_up[i]), wd=bf(w_down[i]), g2=ln2_g[i], b2=ln2_b[i],
                                 tm=512, ff_chunk=256)
        xp = post(xp.reshape(bp * sp, D_MODEL), op.reshape(bp * sp, MIX_W),
                  mp.reshape(bp * sp, MEM_W)).reshape(bp, sp, D_MODEL)
        xs = post(xs.reshape(bs * ss, D_MODEL), os_.reshape(bs * ss, MIX_W),
                  ms.reshape(bs * ss, MEM_W)).reshape(bs, ss, D_MODEL)

    mem_shape = (DEPTH, bp, N_MEM, MEM_HEADS, MEM_HD)
    return (xp, xs, jnp.stack(ret_p), jnp.stack(ret_s), jnp.stack(dkp), jnp.stack(dvp),
            jnp.stack(dks), jnp.stack(dvs), mem_k_p.reshape(mem_shape), mem_v_p.reshape(mem_shape))
```

```python
import functools
import math

import jax
import jax.numpy as jnp
import numpy as np
from jax import lax
from jax.experimental import pallas as pl
from jax.experimental.pallas import tpu as pltpu

D_MODEL = 1024
DEPTH = 2
PAST_LEN = 1024
CHUNK = 64
N_MEM = 256
MEM_HEADS = 4
MEM_HD = 64
MEM_W = MEM_HEADS * MEM_HD
MIX_W = D_MODEL - MEM_W
RET_HEADS = 6
RET_HD = MIX_W // RET_HEADS
RET_THETA = 10000.0
DIFF_HEADS = 6
DIFF_HD = MIX_W // (2 * DIFF_HEADS)
ROPE_THETA = 500000.0
ROT_DIM = DIFF_HD // 4
D_FF = -(-8 * D_MODEL // (3 * 256)) * 256
ALPHA = (2 * DEPTH) ** 0.25
LN_EPS = 1e-5
NEG_INF = -1e30

Q_SCALE = DIFF_HD ** -0.5 * math.log2(math.e)
HEAD_W = 128
VMEM_LIMIT = 56 * 1024 * 1024

F32 = jnp.float32
BF16 = jnp.bfloat16


def _lambda_init(layer_idx):
    return 0.8 - 0.6 * math.exp(-0.3 * layer_idx)


def _dot(a, b):
    return jnp.dot(a.astype(BF16), b.astype(BF16), preferred_element_type=F32)


def _dot_nt(a, b):
    return lax.dot_general(a.astype(BF16), b.astype(BF16), (((1,), (1,)), ((), ())),
                           preferred_element_type=F32)


def _dot_tn(a, b):
    return lax.dot_general(a.astype(BF16), b.astype(BF16), (((0,), (0,)), ((), ())),
                           preferred_element_type=F32)


def _const_spec(shape):
    return pl.BlockSpec(shape, lambda *_: (0,) * len(shape), pipeline_mode=pl.Buffered(1))


def _params(*semantics):
    return pltpu.CompilerParams(dimension_semantics=semantics, vmem_limit_bytes=VMEM_LIMIT)


def _layer_norm_rows(y, g, b):
    mu = jnp.mean(y, axis=-1, keepdims=True)
    d = y - mu
    var = jnp.mean(d * d, axis=-1, keepdims=True)
    return d * lax.rsqrt(var + LN_EPS) * g + b


def _softmax_rows(s):
    s = s - jnp.max(s, axis=-1, keepdims=True)
    p = jnp.exp(s)
    return p / jnp.sum(p, axis=-1, keepdims=True)


def _memory_attention(mq, mk, mv):
    lane = lax.broadcasted_iota(jnp.int32, mk.shape, 1)
    mqb = (mq * MEM_HD ** -0.5).astype(BF16)
    out = None
    for h in range(MEM_HEADS):
        sel = (lane >= h * MEM_HD) & (lane < (h + 1) * MEM_HD)
        p = _softmax_rows(_dot_nt(mqb, jnp.where(sel, mk, 0.0)))
        o = _dot(p, jnp.where(sel, mv, 0.0))
        out = o if out is None else out + o
    return out


def _mem_kv_kernel(x_ref, w_ref, k_ref, v_ref):
    kv = _dot(x_ref[...], w_ref[0])
    k_ref[0] = kv[:, :MEM_W]
    v_ref[0] = kv[:, MEM_W:]


def _mem_kv(mem_flat, w_bf):
    m = mem_flat.shape[0]
    tm = min(m, 512)
    out = jax.ShapeDtypeStruct((DEPTH, m, MEM_W), F32)
    return pl.pallas_call(
        _mem_kv_kernel,
        grid=(DEPTH, m // tm),
        in_specs=[pl.BlockSpec((tm, D_MODEL), lambda l, i: (i, 0)),
                  pl.BlockSpec((1, D_MODEL, 2 * MEM_W), lambda l, i: (l, 0, 0))],
        out_specs=[pl.BlockSpec((1, tm, MEM_W), lambda l, i: (l, i, 0))] * 2,
        out_shape=[out, out],
        name="mem_kv",
        compiler_params=_params("arbitrary", "arbitrary"),
    )(mem_flat, w_bf)


def _ret_tables(tile):
    h = np.arange(RET_HEADS, dtype=np.float64)
    log_g = np.log(1.0 - np.exp2(-5.0 - h))
    idx = np.arange(tile, dtype=np.float64)
    dist = np.abs(idx[:, None] - idx[None, :])
    visible = (idx[None, :] // CHUNK) <= (idx[:, None] // CHUNK)
    decay = np.where(visible[None], np.exp(dist[None] * log_g[:, None, None]), 0.0)
    xi = np.exp((idx + 1.0)[None, :] * log_g[:, None])[:, :, None]
    zeta = np.exp((tile - 1.0 - idx)[None, :] * log_g[:, None])[:, :, None]
    g_tile = np.exp(tile * log_g)
    f = lambda a: jnp.asarray(a, F32)
    return f(decay), f(xi), f(zeta), [float(g) for g in g_tile]


def _ret_kernel(x_ref, w_ref, cos_ref, sin_ref, r0_ref, mk_ref, mv_ref, gn_ref, decay_ref, xi_ref,
                zeta_ref, o_ref, m_ref, r_ref, state_ref, *, tile, g_tile):
    step = pl.program_id(1)

    @pl.when(step == 0)
    def _():
        state_ref[...] = r0_ref[0]

    proj = _dot(x_ref[0], w_ref[...])
    t = proj.shape[0]
    cos = cos_ref[...]
    sin = sin_ref[...]

    def rope(a):
        return a * cos + pltpu.roll(a, RET_HD // 2, 1) * sin

    for h in range(RET_HEADS):
        col = h * HEAD_W
        q = rope(proj[:, col:col + HEAD_W])
        k = rope(proj[:, MIX_W + col:MIX_W + col + HEAD_W]) * RET_HD ** -0.5
        v = proj[:, 2 * MIX_W + col:2 * MIX_W + col + HEAD_W]
        gate = proj[:, 3 * MIX_W + col:3 * MIX_W + col + HEAD_W]
        outs = []
        r = state_ref[h]
        for c in range(t // tile):
            rows = slice(c * tile, (c + 1) * tile)
            qc, kc, vc = q[rows], k[rows], v[rows]
            inner = _dot_nt(qc, kc) * decay_ref[h]
            outs.append(_dot(inner, vc) + _dot(qc, r) * xi_ref[h])
            r = g_tile[h] * r + _dot_tn(kc * zeta_ref[h], vc)
        state_ref[h] = r
        o = outs[0] if len(outs) == 1 else jnp.concatenate(outs, axis=0)
        mu = jnp.mean(o, axis=-1, keepdims=True)
        d = o - mu
        var = jnp.mean(d * d, axis=-1, keepdims=True)
        o = d * lax.rsqrt(var + LN_EPS) * gn_ref[:, col:col + HEAD_W]
        o = o * (gate / (1.0 + jnp.exp(-gate)))
        o_ref[0, :, col:col + HEAD_W] = o.astype(o_ref.dtype)

    m = _memory_attention(proj[:, 4 * MIX_W:], mk_ref[0], mv_ref[0])
    m_ref[0] = m.astype(m_ref.dtype)

    @pl.when(step == pl.num_programs(1) - 1)
    def _():
        r_ref[0] = state_ref[...]


def _ret_mixer(x, pos, r0, mk, mv, w_bf, gn_g, *, t_step, tile):
    b, s, _ = x.shape
    half = RET_HD // 2
    inv_freq = jnp.exp(-math.log(RET_THETA) * jnp.arange(half, dtype=F32) * 2.0 / RET_HD)
    ang = pos.astype(F32)[:, None] * inv_freq[None, :]
    cos = jnp.concatenate([jnp.cos(ang), jnp.cos(ang)], axis=-1)
    sin = jnp.concatenate([-jnp.sin(ang), jnp.sin(ang)], axis=-1)
    decay, xi, zeta, g_tile = _ret_tables(tile)
    cols = w_bf.shape[1]
    kern = functools.partial(_ret_kernel, tile=tile, g_tile=g_tile)
    return pl.pallas_call(
        kern,
        grid=(b, s // t_step),
        in_specs=[
            pl.BlockSpec((1, t_step, D_MODEL), lambda i, j: (i, j, 0)),
            _const_spec((D_MODEL, cols)),
            pl.BlockSpec((t_step, HEAD_W), lambda i, j: (j, 0)),
            pl.BlockSpec((t_step, HEAD_W), lambda i, j: (j, 0)),
            pl.BlockSpec((1, RET_HEADS, RET_HD, RET_HD), lambda i, j: (i, 0, 0, 0)),
            pl.BlockSpec((1, N_MEM, MEM_W), lambda i, j: (i, 0, 0)),
            pl.BlockSpec((1, N_MEM, MEM_W), lambda i, j: (i, 0, 0)),
            _const_spec((1, MIX_W)),
            _const_spec((RET_HEADS, tile, tile)),
            _const_spec((RET_HEADS, tile, 1)),
            _const_spec((RET_HEADS, tile, 1)),
        ],
        out_specs=[
            pl.BlockSpec((1, t_step, MIX_W), lambda i, j: (i, j, 0)),
            pl.BlockSpec((1, t_step, MEM_W), lambda i, j: (i, j, 0)),
            pl.BlockSpec((1, RET_HEADS, RET_HD, RET_HD), lambda i, j: (i, 0, 0, 0)),
        ],
        out_shape=[
            jax.ShapeDtypeStruct((b, s, MIX_W), BF16),
            jax.ShapeDtypeStruct((b, s, MEM_W), BF16),
            jax.ShapeDtypeStruct((b, RET_HEADS, RET_HD, RET_HD), F32),
        ],
        scratch_shapes=[pltpu.VMEM((RET_HEADS, RET_HD, RET_HD), F32)],
        name="ret_mixer",
        compiler_params=_params("arbitrary", "arbitrary"),
    )(x, w_bf, cos, sin, r0, mk, mv, gn_g.reshape(1, MIX_W), decay, xi, zeta)


def _diff_proj_kernel(x_ref, w_ref, c_ref, sa_ref, sb_ref, mk_ref, mv_ref, q_ref, k_ref, v_ref, m_ref):
    proj = _dot(x_ref[0], w_ref[...])
    c = c_ref[...]
    sa = sa_ref[...]
    sb = sb_ref[...]

    def rope(a):
        return (a * c + pltpu.roll(a, ROT_DIM // 2, 1) * sa
                + pltpu.roll(a, HEAD_W - ROT_DIM // 2, 1) * sb)

    for h in range(DIFF_HEADS):
        col = h * HEAD_W
        q = rope(proj[:, col:col + HEAD_W]) * Q_SCALE
        q_ref[0, :, col:col + HEAD_W] = q.astype(q_ref.dtype)
        k_ref[0, :, col:col + HEAD_W] = rope(proj[:, MIX_W + col:MIX_W + col + HEAD_W])
    v_ref[0] = proj[:, 2 * MIX_W:3 * MIX_W]
    m = _memory_attention(proj[:, 3 * MIX_W:], mk_ref[0], mv_ref[0])
    m_ref[0] = m.astype(m_ref.dtype)


def _diff_project(x, pos, mk, mv, w_bf, *, t_step):
    b, s, _ = x.shape
    half = ROT_DIM // 2
    inv_freq = jnp.exp(-math.log(ROPE_THETA) * jnp.arange(half, dtype=F32) * 2.0 / ROT_DIM)
    ang = pos.astype(F32)[:, None] * inv_freq[None, :]
    cos, sin = jnp.cos(ang), jnp.sin(ang)
    ones = jnp.ones((s, DIFF_HD - ROT_DIM), F32)
    zeros_rest = jnp.zeros((s, DIFF_HD - ROT_DIM), F32)
    zeros_half = jnp.zeros((s, half), F32)
    c64 = jnp.concatenate([cos, cos, ones], axis=-1)
    sa64 = jnp.concatenate([zeros_half, sin, zeros_rest], axis=-1)
    sb64 = jnp.concatenate([-sin, zeros_half, zeros_rest], axis=-1)
    tables = [jnp.concatenate([a, a], axis=-1) for a in (c64, sa64, sb64)]
    cols = w_bf.shape[1]
    tab_spec = pl.BlockSpec((t_step, HEAD_W), lambda i, j: (j, 0))
    tok_spec = lambda w: pl.BlockSpec((1, t_step, w), lambda i, j: (i, j, 0))
    mem_spec = pl.BlockSpec((1, N_MEM, MEM_W), lambda i, j: (i, 0, 0))
    return pl.pallas_call(
        _diff_proj_kernel,
        grid=(b, s // t_step),
        in_specs=[tok_spec(D_MODEL), _const_spec((D_MODEL, cols)), tab_spec, tab_spec, tab_spec,
                  mem_spec, mem_spec],
        out_specs=[tok_spec(MIX_W), tok_spec(MIX_W), tok_spec(MIX_W), tok_spec(MEM_W)],
        out_shape=[
            jax.ShapeDtypeStruct((b, s, MIX_W), BF16),
            jax.ShapeDtypeStruct((b, s, MIX_W), F32),
            jax.ShapeDtypeStruct((b, s, MIX_W), F32),
            jax.ShapeDtypeStruct((b, s, MEM_W), BF16),
        ],
        name="diff_proj",
        compiler_params=_params("arbitrary", "arbitrary"),
    )(x, w_bf, *tables, mk, mv)


def _diff_lambda(lq1_ref, lk1_ref, lq2_ref, lk2_ref, lam_init):
    a = jnp.sum(lq1_ref[...] * lk1_ref[...], axis=-1, keepdims=True)
    b = jnp.sum(lq2_ref[...] * lk2_ref[...], axis=-1, keepdims=True)
    return jnp.exp(a) - jnp.exp(b) + lam_init


def _diff_finish(o, g_ref, lam_init):
    ms = jnp.mean(o * o, axis=-1, keepdims=True)
    return o * lax.rsqrt(ms + LN_EPS) * g_ref[...] * (1.0 - lam_init)


def _split_heads(k):
    lane = lax.broadcasted_iota(jnp.int32, k.shape, 1)
    first = lane < DIFF_HD
    return jnp.where(first, k, 0.0).astype(BF16), jnp.where(first, 0.0, k).astype(BF16)


def _diff_attn_prompt_kernel(tab_ref, q_ref, k_ref, v_ref, lq1_ref, lk1_ref, lq2_ref, lk2_ref, g_ref,
                             o_ref, kx_ref, vt_ref, qx_ref, s_ref, p_ref, acc_ref, *, tq, tk, lam_init):
    seq = q_ref.shape[1]
    n_q, n_k = seq // tq, seq // tk
    q_per_k = tk // tq
    tiles = [(qi, j) for qi in range(n_q) for j in range(qi // q_per_k + 1)]
    n_tiles = len(tiles)
    chunks_q, chunks_k = tq // CHUNK, tk // CHUNK

    k = k_ref[0]
    lane = lax.broadcasted_iota(jnp.int32, k.shape, 1)
    row_chunk = (lax.broadcasted_iota(jnp.int32, k.shape, 0) // CHUNK) % chunks_k
    chunk_one_hot = jnp.where(lane == row_chunk, 1.0, 0.0).astype(BF16)
    first = lane < DIFF_HD
    kx_ref[0, :, :HEAD_W] = jnp.where(first, k, 0.0).astype(BF16)
    kx_ref[1, :, :HEAD_W] = jnp.where(first, 0.0, k).astype(BF16)
    kx_ref[0, :, HEAD_W:] = chunk_one_hot
    kx_ref[1, :, HEAD_W:] = chunk_one_hot
    for t in range(n_k):
        vt_ref[t] = v_ref[0, t * tk:(t + 1) * tk, :].T.astype(BF16)
    bias_row = lax.broadcasted_iota(jnp.int32, (HEAD_W, tq), 0)
    bias_col = lax.broadcasted_iota(jnp.int32, (HEAD_W, tq), 1) // CHUNK
    for qi in range(n_q):
        q_t = q_ref[0, qi * tq:(qi + 1) * tq, :].astype(F32).T.astype(BF16)
        hidden = (bias_row < chunks_k) & (bias_row > bias_col + chunks_q * (qi % q_per_k))
        for diag in (0, 1):
            qx_ref[2 * qi + diag, :HEAD_W, :] = q_t
            bias = jnp.where(hidden, NEG_INF, 0.0) if diag else jnp.zeros((HEAD_W, tq), F32)
            qx_ref[2 * qi + diag, HEAD_W:, :] = bias.astype(BF16)
    lam = _diff_lambda(lq1_ref, lk1_ref, lq2_ref, lk2_ref, lam_init)

    def tile_of(i):
        if isinstance(i, int):
            qi, j = tiles[i]
            last = j == qi // q_per_k
            return j, 2 * qi + int(last), j == 0, last, qi
        j, qx, is_first, is_last, qi = (tab_ref[i, f] for f in range(5))
        return j, qx, is_first == 1, is_last == 1, qi

    def stage_a(i, slot):
        j, qx, _, _, _ = tile_of(i)
        rows = pl.ds(j * tk if isinstance(j, int) else pl.multiple_of(j * tk, tk), tk)
        tops = []
        for mp in range(2):
            s = _dot(kx_ref[mp, rows, :], qx_ref[qx])
            s_ref[slot, mp] = s
            tops.append(jnp.max(s, axis=0, keepdims=True))
        return tuple(tops)

    def stage_b(i, slot, m, l, top):
        _, _, is_first, _, _ = tile_of(i)
        out = []
        for mp in range(2):
            m_prev = jnp.where(is_first, NEG_INF, m[mp])
            m_new = jnp.maximum(m_prev, top[mp])
            scale = jnp.exp2(m_prev - m_new)
            p = jnp.exp2(s_ref[slot, mp] - m_new)
            p_ref[slot, mp] = p.astype(BF16)
            out.append((m_new, scale * l[mp] + jnp.sum(p, axis=0, keepdims=True), scale))
        return tuple(zip(*out))

    def stage_c(i, slot, scale, l):
        j, _, _, is_last, qi = tile_of(i)
        for mp in range(2):
            acc_ref[mp] = scale[mp] * acc_ref[mp] + _dot(vt_ref[j], p_ref[slot, mp])

        def finish():
            o = (acc_ref[0] / l[0] - lam * (acc_ref[1] / l[1])).T
            rows = pl.ds(qi * tq if isinstance(qi, int) else pl.multiple_of(qi * tq, tq), tq)
            o_ref[0, rows, :] = _diff_finish(o, g_ref, lam_init).astype(o_ref.dtype)

        if isinstance(is_last, bool):
            if is_last:
                finish()
        else:
            pl.when(is_last)(finish)

    def iteration(i, carry, a=True, b=True, c=True):
        slot = i % 2 if isinstance(i, int) else 0
        m, l, top, scale, l_done = carry
        new_scale, new_l_done = scale, l_done
        if b:
            m, l, new_scale = stage_b(i - 1, 1 - slot, m, l, top)
            new_l_done = l
        if a:
            top = stage_a(i, slot)
        if c:
            stage_c(i - 2, slot, scale, l_done)
        return m, l, top, new_scale, new_l_done

    def odd_iteration(i, carry):
        m, l, top, scale, l_done = carry
        m, l, new_scale = stage_b(i - 1, 0, m, l, top)
        top = stage_a(i, 1)
        stage_c(i - 2, 1, scale, l_done)
        return m, l, top, new_scale, l

    def pair(n, carry):
        i = 2 + 2 * n
        return odd_iteration(i + 1, iteration(i, carry))

    zero = (jnp.zeros((1, tq), F32),) * 2
    acc_ref[...] = jnp.zeros(acc_ref.shape, F32)
    carry = (zero, zero, zero, zero, zero)
    carry = iteration(0, carry, b=False, c=False)
    carry = iteration(1, carry, c=False)
    carry = lax.fori_loop(0, (n_tiles - 2) // 2, pair, carry)
    carry = iteration(n_tiles, carry, a=False)
    iteration(n_tiles + 1, carry, a=False, b=False)


def _diff_attn_prompt(q, k, v, lam_vecs, subln_g, lam_init, *, tq, tk):
    b, s, _ = q.shape
    q_per_k = tk // tq
    rows = [(j, 2 * qi + int(j == qi // q_per_k), int(j == 0), int(j == qi // q_per_k), qi)
            for qi in range(s // tq) for j in range(qi // q_per_k + 1)]
    assert len(rows) % 2 == 0
    table = jnp.asarray(np.array(rows, np.int32))
    kern = functools.partial(_diff_attn_prompt_kernel, tq=tq, tk=tk, lam_init=lam_init)
    const = lambda shape: pl.BlockSpec(shape, lambda i, h, tab: (0,) * len(shape))
    head_spec = pl.BlockSpec((1, s, HEAD_W), lambda i, h, tab: (i, 0, h))
    return pl.pallas_call(
        kern,
        grid_spec=pltpu.PrefetchScalarGridSpec(
            num_scalar_prefetch=1,
            grid=(b, DIFF_HEADS),
            in_specs=[head_spec, head_spec, head_spec] + [const((1, DIFF_HD))] * 4 + [const((1, HEAD_W))],
            out_specs=head_spec,
            scratch_shapes=[
                pltpu.VMEM((2, s, 2 * HEAD_W), BF16),
                pltpu.VMEM((s // tk, HEAD_W, tk), BF16),
                pltpu.VMEM((2 * (s // tq), 2 * HEAD_W, tq), BF16),
                pltpu.VMEM((2, 2, tk, tq), F32),
                pltpu.VMEM((2, 2, tk, tq), BF16),
                pltpu.VMEM((2, HEAD_W, tq), F32),
            ]),
        out_shape=jax.ShapeDtypeStruct((b, s, MIX_W), BF16),
        name="diff_attn_prompt",
        compiler_params=_params("arbitrary", "arbitrary"),
    )(table, q, k, v, *lam_vecs, subln_g.reshape(1, HEAD_W))


def _diff_attn_sample_kernel(q_ref, kc_ref, vc_ref, kn_ref, vn_ref, lq1_ref, lk1_ref, lq2_ref, lk2_ref,
                             g_ref, o_ref, *, lam_init):
    q = q_ref[0]
    kc1, kc2 = _split_heads(kc_ref[0])
    kn1, kn2 = _split_heads(kn_ref[0])
    vc = vc_ref[0]
    vn = vn_ref[0]

    def one_map(kc, kn):
        sc = _dot_nt(q, kc)
        sn = _dot_nt(q, kn)
        m = jnp.maximum(jnp.max(sc, axis=-1, keepdims=True), jnp.max(sn, axis=-1, keepdims=True))
        pc = jnp.exp2(sc - m)
        pn = jnp.exp2(sn - m)
        l = jnp.sum(pc, axis=-1, keepdims=True) + jnp.sum(pn, axis=-1, keepdims=True)
        return (_dot(pc, vc) + _dot(pn, vn)) / l

    lam = _diff_lambda(lq1_ref, lk1_ref, lq2_ref, lk2_ref, lam_init)
    o = one_map(kc1, kn1) - lam * one_map(kc2, kn2)
    o_ref[0] = _diff_finish(o, g_ref, lam_init).astype(o_ref.dtype)


def _diff_attn_sample(q, k_cache, v_cache, k_new, v_new, lam_vecs, subln_g, lam_init):
    b, s, _ = q.shape
    past = k_cache.shape[1]
    kern = functools.partial(_diff_attn_sample_kernel, lam_init=lam_init)
    vec_spec = _const_spec((1, DIFF_HD))
    new_spec = pl.BlockSpec((1, s, HEAD_W), lambda i, h: (i, 0, h))
    old_spec = pl.BlockSpec((1, past, HEAD_W), lambda i, h: (i, 0, h))
    return pl.pallas_call(
        kern,
        grid=(b, DIFF_HEADS),
        in_specs=[new_spec, old_spec, old_spec, new_spec, new_spec, vec_spec, vec_spec, vec_spec,
                  vec_spec, _const_spec((1, HEAD_W))],
        out_specs=new_spec,
        out_shape=jax.ShapeDtypeStruct((b, s, MIX_W), BF16),
        name="diff_attn_sample",
        compiler_params=_params("arbitrary", "arbitrary"),
    )(q, k_cache, v_cache, k_new, v_new, *lam_vecs, subln_g.reshape(1, HEAD_W))


def _post_kernel(x_ref, o_ref, m_ref, wo_ref, g1_ref, b1_ref, wg_ref, wu_ref, wd_ref, g2_ref, b2_ref,
                 y_ref, h_ref, *, ff_chunk):
    mixed = _dot(o_ref[...], wo_ref[:MIX_W, :]) + _dot(m_ref[...], wo_ref[MIX_W:, :])
    x1 = _layer_norm_rows(ALPHA * x_ref[...] + mixed, g1_ref[...], b1_ref[...])
    x1b = x1.astype(BF16)
    for c in range(D_FF // ff_chunk):
        cols = slice(c * ff_chunk, (c + 1) * ff_chunk)
        gate = _dot(x1b, wg_ref[:, cols])
        up = _dot(x1b, wu_ref[:, cols])
        h_ref[:, cols] = (gate / (1.0 + jnp.exp(-gate)) * up).astype(BF16)
    ff = _dot(h_ref[...], wd_ref[...])
    y_ref[...] = _layer_norm_rows(ALPHA * x1 + ff, g2_ref[...], b2_ref[...])


def _post(x, o, m, wo, g1, b1, wg, wu, wd, g2, b2, *, tm, ff_chunk):
    n = x.shape[0]
    row = lambda w: pl.BlockSpec((tm, w), lambda i: (i, 0))
    vec = _const_spec((1, D_MODEL))
    kern = functools.partial(_post_kernel, ff_chunk=ff_chunk)
    r1 = lambda a: a.reshape(1, D_MODEL)
    return pl.pallas_call(
        kern,
        grid=(n // tm,),
        in_specs=[row(D_MODEL), row(MIX_W), row(MEM_W), _const_spec((D_MODEL, D_MODEL)), vec, vec,
                  _const_spec((D_MODEL, D_FF)), _const_spec((D_MODEL, D_FF)),
                  _const_spec((D_FF, D_MODEL)), vec, vec],
        out_specs=row(D_MODEL),
        out_shape=jax.ShapeDtypeStruct((n, D_MODEL), F32),
        scratch_shapes=[pltpu.VMEM((tm, D_FF), BF16)],
        name="post_mixer",
        compiler_params=_params("arbitrary"),
    )(x, o, m, wo, r1(g1), r1(b1), wg, wu, wd, r1(g2), r1(b2))


def kernel(x_prompt, x_sample, mem_prompt, cache_ret_state, cache_diff_k, cache_diff_v, cache_mem_k,
           cache_mem_v, ret_w_in, ret_gn_g, diff_w_in, diff_lambda_q1, diff_lambda_k1, diff_lambda_q2,
           diff_lambda_k2, diff_subln_g, w_mem_kv, w_o, ln1_g, ln1_b, w_gate, w_up, w_down, ln2_g, ln2_b):
    bp, sp, _ = x_prompt.shape
    bs, ss, _ = x_sample.shape
    pos_p = jnp.arange(sp)
    pos_s = PAST_LEN + jnp.arange(ss)
    bf = lambda a: a.astype(BF16)

    mem_k_p, mem_v_p = _mem_kv(mem_prompt.reshape(bp * N_MEM, D_MODEL), bf(w_mem_kv))
    mem_k_p = mem_k_p.reshape(DEPTH, bp, N_MEM, MEM_W)
    mem_v_p = mem_v_p.reshape(DEPTH, bp, N_MEM, MEM_W)
    mem_k_s = cache_mem_k.reshape(DEPTH, bs, N_MEM, MEM_W)
    mem_v_s = cache_mem_v.reshape(DEPTH, bs, N_MEM, MEM_W)

    xp, xs = x_prompt, x_sample
    ret_p, ret_s, dkp, dvp, dks, dvs = [], [], [], [], [], []
    for i in range(DEPTH):
        j = i // 2
        if i % 2 == 0:
            w_in = bf(ret_w_in[j])
            r0 = jnp.zeros((bp, RET_HEADS, RET_HD, RET_HD), F32)
            op, mp, rp = _ret_mixer(xp, pos_p, r0, mem_k_p[i], mem_v_p[i], w_in, ret_gn_g[j],
                                    t_step=512, tile=256)
            os_, ms, rs = _ret_mixer(xs, pos_s, cache_ret_state[j], mem_k_s[i], mem_v_s[i], w_in,
                                     ret_gn_g[j], t_step=ss, tile=ss)
            ret_p.append(rp)
            ret_s.append(rs)
        else:
            lam_init = _lambda_init(i)
            w_in = bf(diff_w_in[j])
            lam_vecs = [a[j].reshape(1, DIFF_HD) for a in
                        (diff_lambda_q1, diff_lambda_k1, diff_lambda_q2, diff_lambda_k2)]
            q, k, v, mp = _diff_project(xp, pos_p, mem_k_p[i], mem_v_p[i], w_in, t_step=512)
            op = _diff_attn_prompt(q, k, v, lam_vecs, diff_subln_g[j], lam_init, tq=256, tk=512)
            dkp.append(k.reshape(bp, sp, DIFF_HEADS, HEAD_W))
            dvp.append(v.reshape(bp, sp, DIFF_HEADS, HEAD_W))
            q, k, v, ms = _diff_project(xs, pos_s, mem_k_s[i], mem_v_s[i], w_in, t_step=ss)
            os_ = _diff_attn_sample(q, cache_diff_k[j].reshape(bs, PAST_LEN, MIX_W),
                                    cache_diff_v[j].reshape(bs, PAST_LEN, MIX_W), k, v, lam_vecs,
                                    diff_subln_g[j], lam_init)
            dks.append(k.reshape(bs, ss, DIFF_HEADS, HEAD_W))
            dvs.append(v.reshape(bs, ss, DIFF_HEADS, HEAD_W))
        post = functools.partial(_post, wo=bf(w_o[i]), g1=ln1_g[i], b1=ln1_b[i], wg=bf(w_gate[i]),
                                 wu=bf(w_up[i]), wd=bf(w_down[i]), g2=ln2_g[i], b2=ln2_b[i],
                                 tm=512, ff_chunk=256)
        xp = post(xp.reshape(bp * sp, D_MODEL), op.reshape(bp * sp, MIX_W),
                  mp.reshape(bp * sp, MEM_W)).reshape(bp, sp, D_MODEL)
        xs = post(xs.reshape(bs * ss, D_MODEL), os_.reshape(bs * ss, MIX_W),
                  ms.reshape(bs * ss, MEM_W)).reshape(bs, ss, D_MODEL)

    mem_shape = (DEPTH, bp, N_MEM, MEM_HEADS, MEM_HD)
    return (xp, xs, jnp.stack(ret_p), jnp.stack(ret_s), jnp.stack(dkp), jnp.stack(dvp),
            jnp.stack(dks), jnp.stack(dvs), mem_k_p.reshape(mem_shape), mem_v_p.reshape(mem_shape))
```

```python
import functools
import math

import jax
import jax.numpy as jnp
import numpy as np
from jax import lax
from jax.experimental import pallas as pl
from jax.experimental.pallas import tpu as pltpu

D_MODEL = 1024
DEPTH = 2
PAST_LEN = 1024
CHUNK = 64
N_MEM = 256
MEM_HEADS = 4
MEM_HD = 64
MEM_W = MEM_HEADS * MEM_HD
MIX_W = D_MODEL - MEM_W
RET_HEADS = 6
RET_HD = MIX_W // RET_HEADS
RET_THETA = 10000.0
DIFF_HEADS = 6
DIFF_HD = MIX_W // (2 * DIFF_HEADS)
ROPE_THETA = 500000.0
ROT_DIM = DIFF_HD // 4
D_FF = -(-8 * D_MODEL // (3 * 256)) * 256
ALPHA = (2 * DEPTH) ** 0.25
LN_EPS = 1e-5
NEG_INF = -1e30

Q_SCALE = DIFF_HD ** -0.5 * math.log2(math.e)
HEAD_W = 128
VMEM_LIMIT = 56 * 1024 * 1024

F32 = jnp.float32
BF16 = jnp.bfloat16


def _lambda_init(layer_idx):
    return 0.8 - 0.6 * math.exp(-0.3 * layer_idx)


def _dot(a, b):
    return jnp.dot(a.astype(BF16), b.astype(BF16), preferred_element_type=F32)


def _dot_nt(a, b):
    return lax.dot_general(a.astype(BF16), b.astype(BF16), (((1,), (1,)), ((), ())),
                           preferred_element_type=F32)


def _dot_tn(a, b):
    return lax.dot_general(a.astype(BF16), b.astype(BF16), (((0,), (0,)), ((), ())),
                           preferred_element_type=F32)


def _fixed_spec(shape, index):
    return pl.BlockSpec(shape, lambda *_: index, pipeline_mode=pl.Buffered(1))


def _const_spec(shape):
    return _fixed_spec(shape, (0,) * len(shape))


def _layer_spec(shape, layer):
    return _fixed_spec((1,) + shape, (layer,) + (0,) * len(shape))


def _params(*semantics):
    return pltpu.CompilerParams(dimension_semantics=semantics, vmem_limit_bytes=VMEM_LIMIT)


def _layer_norm_rows(y, g, b):
    mu = jnp.mean(y, axis=-1, keepdims=True)
    d = y - mu
    var = jnp.mean(d * d, axis=-1, keepdims=True)
    return d * lax.rsqrt(var + LN_EPS) * g + b


def _softmax_rows(s):
    s = s - jnp.max(s, axis=-1, keepdims=True)
    p = jnp.exp(s)
    return p / jnp.sum(p, axis=-1, keepdims=True)


def _memory_attention(mq, mk_t, mv_t):
    row = lax.broadcasted_iota(jnp.int32, mk_t.shape, 0)
    mqb = (mq * MEM_HD ** -0.5).astype(BF16)
    out = None
    for h in range(MEM_HEADS):
        sel = (row >= h * MEM_HD) & (row < (h + 1) * MEM_HD)
        p = _softmax_rows(_dot(mqb, jnp.where(sel, mk_t, 0.0)))
        o = _dot_nt(p, jnp.where(sel, mv_t, 0.0))
        out = o if out is None else out + o
    return out


def _mem_kv_kernel(x_ref, w_ref, k_ref, v_ref):
    kv_t = _dot_nt(w_ref[0], x_ref[0])
    k_ref[0, 0] = kv_t[:MEM_W]
    v_ref[0, 0] = kv_t[MEM_W:]


def _mem_kv(mem, w_t_bf):
    b = mem.shape[0]
    out = jax.ShapeDtypeStruct((DEPTH, b, MEM_W, N_MEM), F32)
    return pl.pallas_call(
        _mem_kv_kernel,
        grid=(DEPTH, b),
        in_specs=[pl.BlockSpec((1, N_MEM, D_MODEL), lambda l, i: (i, 0, 0)),
                  pl.BlockSpec((1, 2 * MEM_W, D_MODEL), lambda l, i: (l, 0, 0))],
        out_specs=[pl.BlockSpec((1, 1, MEM_W, N_MEM), lambda l, i: (l, i, 0, 0))] * 2,
        out_shape=[out, out],
        name="mem_kv",
        compiler_params=_params("arbitrary", "arbitrary"),
    )(mem, w_t_bf)


def _ret_tables(tile):
    h = np.arange(RET_HEADS, dtype=np.float64)
    log_g = np.log(1.0 - np.exp2(-5.0 - h))
    idx = np.arange(tile, dtype=np.float64)
    dist = np.abs(idx[:, None] - idx[None, :])
    visible = (idx[None, :] // CHUNK) <= (idx[:, None] // CHUNK)
    decay = np.where(visible[None], np.exp(dist[None] * log_g[:, None, None]), 0.0)
    xi = np.exp((idx + 1.0)[None, :] * log_g[:, None])[:, :, None]
    zeta = np.exp((tile - 1.0 - idx)[None, :] * log_g[:, None])[:, :, None]
    g_tile = np.exp(tile * log_g)
    f = lambda a: jnp.asarray(a, F32)
    return f(decay), f(xi), f(zeta), [float(g) for g in g_tile]


def _ret_kernel(x_ref, w_ref, cos_ref, sin_ref, r0_ref, mk_ref, mv_ref, gn_ref, decay_ref, xi_ref,
                zeta_ref, o_ref, m_ref, r_ref, state_ref, *, tile, g_tile):
    step = pl.program_id(1)

    @pl.when(step == 0)
    def _():
        state_ref[...] = r0_ref[0]

    proj = _dot(x_ref[0], w_ref[0])
    t = proj.shape[0]
    cos = cos_ref[...]
    sin = sin_ref[...]

    def rope(a):
        return a * cos + pltpu.roll(a, RET_HD // 2, 1) * sin

    for h in range(RET_HEADS):
        col = h * HEAD_W
        q = rope(proj[:, col:col + HEAD_W])
        k = rope(proj[:, MIX_W + col:MIX_W + col + HEAD_W]) * RET_HD ** -0.5
        v = proj[:, 2 * MIX_W + col:2 * MIX_W + col + HEAD_W]
        gate = proj[:, 3 * MIX_W + col:3 * MIX_W + col + HEAD_W]
        outs = []
        r = state_ref[h]
        for c in range(t // tile):
            rows = slice(c * tile, (c + 1) * tile)
            qc, kc, vc = q[rows], k[rows], v[rows]
            inner = _dot_nt(qc, kc) * decay_ref[h]
            outs.append(_dot(inner, vc) + _dot(qc, r) * xi_ref[h])
            r = g_tile[h] * r + _dot_tn(kc * zeta_ref[h], vc)
        state_ref[h] = r
        o = outs[0] if len(outs) == 1 else jnp.concatenate(outs, axis=0)
        mu = jnp.mean(o, axis=-1, keepdims=True)
        d = o - mu
        var = jnp.mean(d * d, axis=-1, keepdims=True)
        o = d * lax.rsqrt(var + LN_EPS) * gn_ref[:, col:col + HEAD_W]
        o = o * (gate / (1.0 + jnp.exp(-gate)))
        o_ref[0, :, col:col + HEAD_W] = o.astype(o_ref.dtype)

    m = _memory_attention(proj[:, 4 * MIX_W:], mk_ref[0, 0], mv_ref[0, 0])
    m_ref[0] = m.astype(m_ref.dtype)

    @pl.when(step == pl.num_programs(1) - 1)
    def _():
        r_ref[0] = state_ref[...]


def _ret_mixer(x, pos, r0, mk_t, mv_t, layer, w_bf, w_layer, gn_g, *, t_step, tile):
    b, s, _ = x.shape
    half = RET_HD // 2
    lane_freq = jnp.arange(HEAD_W, dtype=jnp.int32) % half
    inv_freq = jnp.exp(-math.log(RET_THETA) * lane_freq.astype(F32) * 2.0 / RET_HD)
    ang = pos.astype(F32)[:, None] * inv_freq[None, :]
    cos = jnp.cos(ang)
    sin = jnp.where(jnp.arange(HEAD_W) < half, -jnp.sin(ang), jnp.sin(ang))
    decay, xi, zeta, g_tile = _ret_tables(tile)
    cols = w_bf.shape[-1]
    kern = functools.partial(_ret_kernel, tile=tile, g_tile=g_tile)
    mem_spec = pl.BlockSpec((1, 1, MEM_W, N_MEM), lambda i, j: (layer, i, 0, 0))
    return pl.pallas_call(
        kern,
        grid=(b, s // t_step),
        in_specs=[
            pl.BlockSpec((1, t_step, D_MODEL), lambda i, j: (i, j, 0)),
            _layer_spec((D_MODEL, cols), w_layer),
            pl.BlockSpec((t_step, HEAD_W), lambda i, j: (j, 0)),
            pl.BlockSpec((t_step, HEAD_W), lambda i, j: (j, 0)),
            pl.BlockSpec((1, RET_HEADS, RET_HD, RET_HD), lambda i, j: (i, 0, 0, 0)),
            mem_spec, mem_spec,
            _const_spec((1, MIX_W)),
            _const_spec((RET_HEADS, tile, tile)),
            _const_spec((RET_HEADS, tile, 1)),
            _const_spec((RET_HEADS, tile, 1)),
        ],
        out_specs=[
            pl.BlockSpec((1, t_step, MIX_W), lambda i, j: (i, j, 0)),
            pl.BlockSpec((1, t_step, MEM_W), lambda i, j: (i, j, 0)),
            pl.BlockSpec((1, RET_HEADS, RET_HD, RET_HD), lambda i, j: (i, 0, 0, 0)),
        ],
        out_shape=[
            jax.ShapeDtypeStruct((b, s, MIX_W), BF16),
            jax.ShapeDtypeStruct((b, s, MEM_W), BF16),
            jax.ShapeDtypeStruct((b, RET_HEADS, RET_HD, RET_HD), F32),
        ],
        scratch_shapes=[pltpu.VMEM((RET_HEADS, RET_HD, RET_HD), F32)],
        name="ret_mixer",
        compiler_params=_params("arbitrary", "arbitrary"),
    )(x, w_bf, cos, sin, r0, mk_t, mv_t, gn_g.reshape(1, MIX_W), decay, xi, zeta)


def _diff_proj_kernel(x_ref, w_ref, c_ref, sa_ref, sb_ref, mk_ref, mv_ref, q_ref, k_ref, v_ref, m_ref):
    proj = _dot(x_ref[0], w_ref[0])
    c = c_ref[...]
    sa = sa_ref[...]
    sb = sb_ref[...]

    def rope(a):
        return (a * c + pltpu.roll(a, ROT_DIM // 2, 1) * sa
                + pltpu.roll(a, HEAD_W - ROT_DIM // 2, 1) * sb)

    for h in range(DIFF_HEADS):
        col = h * HEAD_W
        q = rope(proj[:, col:col + HEAD_W]) * Q_SCALE
        q_ref[0, h] = q.astype(q_ref.dtype)
        k_ref[0, h] = rope(proj[:, MIX_W + col:MIX_W + col + HEAD_W])
        v_ref[0, h] = proj[:, 2 * MIX_W + col:2 * MIX_W + col + HEAD_W]
    m = _memory_attention(proj[:, 3 * MIX_W:], mk_ref[0, 0], mv_ref[0, 0])
    m_ref[0] = m.astype(m_ref.dtype)


def _diff_project(x, pos, mk_t, mv_t, layer, w_bf, w_layer, *, t_step):
    b, s, _ = x.shape
    half = ROT_DIM // 2
    lane = jnp.arange(HEAD_W, dtype=jnp.int32) % DIFF_HD
    inv_freq = jnp.exp(-math.log(ROPE_THETA) * (lane % half).astype(F32) * 2.0 / ROT_DIM)
    ang = pos.astype(F32)[:, None] * inv_freq[None, :]
    cos, sin = jnp.cos(ang), jnp.sin(ang)
    tables = [jnp.where(lane < ROT_DIM, cos, 1.0),
              jnp.where((lane >= half) & (lane < ROT_DIM), sin, 0.0),
              jnp.where(lane < half, -sin, 0.0)]
    cols = w_bf.shape[-1]
    tab_spec = pl.BlockSpec((t_step, HEAD_W), lambda i, j: (j, 0))
    tok_spec = lambda w: pl.BlockSpec((1, t_step, w), lambda i, j: (i, j, 0))
    head_spec = pl.BlockSpec((1, DIFF_HEADS, t_step, HEAD_W), lambda i, j: (i, 0, j, 0))
    mem_spec = pl.BlockSpec((1, 1, MEM_W, N_MEM), lambda i, j: (layer, i, 0, 0))
    heads = lambda dt: jax.ShapeDtypeStruct((b, DIFF_HEADS, s, HEAD_W), dt)
    return pl.pallas_call(
        _diff_proj_kernel,
        grid=(b, s // t_step),
        in_specs=[tok_spec(D_MODEL), _layer_spec((D_MODEL, cols), w_layer), tab_spec, tab_spec,
                  tab_spec, mem_spec, mem_spec],
        out_specs=[head_spec, head_spec, head_spec, tok_spec(MEM_W)],
        out_shape=[
            heads(BF16),
            heads(F32),
            heads(F32),
            jax.ShapeDtypeStruct((b, s, MEM_W), BF16),
        ],
        name="diff_proj",
        compiler_params=_params("arbitrary", "arbitrary"),
    )(x, w_bf, *tables, mk_t, mv_t)


def _diff_lambda(lq1_ref, lk1_ref, lq2_ref, lk2_ref, lam_init):
    a = jnp.sum(lq1_ref[...] * lk1_ref[...], axis=-1, keepdims=True)
    b = jnp.sum(lq2_ref[...] * lk2_ref[...], axis=-1, keepdims=True)
    return jnp.exp(a) - jnp.exp(b) + lam_init


def _diff_finish(o, g_ref, lam_init):
    ms = jnp.mean(o * o, axis=-1, keepdims=True)
    return o * lax.rsqrt(ms + LN_EPS) * g_ref[...] * (1.0 - lam_init)


def _split_heads(k):
    lane = lax.broadcasted_iota(jnp.int32, k.shape, 1)
    first = lane < DIFF_HD
    return jnp.where(first, k, 0.0).astype(BF16), jnp.where(first, 0.0, k).astype(BF16)


def _causal_tiles(n_q, q_per_k):
    return [(j, 2 * qi + int(j == qi // q_per_k), int(j == 0), int(j == qi // q_per_k), qi)
            for qi in range(n_q) for j in range(qi // q_per_k + 1)]


def _diff_attn_prompt_kernel(tab_ref, q_ref, k_ref, v_ref, lq1_ref, lk1_ref, lq2_ref, lk2_ref, g_ref,
                             o_ref, kx_ref, vt_ref, qx_ref, s_ref, p_ref, acc_ref, *, tq, tk, lam_init):
    seq = q_ref.shape[2]
    n_q, n_k = seq // tq, seq // tk
    q_per_k = tk // tq
    tiles = _causal_tiles(n_q, q_per_k)
    n_tiles = len(tiles)
    chunks_q, chunks_k = tq // CHUNK, tk // CHUNK

    k = k_ref[0, 0]
    lane = lax.broadcasted_iota(jnp.int32, k.shape, 1)
    row_chunk = (lax.broadcasted_iota(jnp.int32, k.shape, 0) // CHUNK) % chunks_k
    chunk_one_hot = jnp.where(lane == row_chunk, 1.0, 0.0).astype(BF16)
    first = lane < DIFF_HD
    kx_ref[0, :, :HEAD_W] = jnp.where(first, k, 0.0).astype(BF16)
    kx_ref[1, :, :HEAD_W] = jnp.where(first, 0.0, k).astype(BF16)
    kx_ref[0, :, HEAD_W:] = chunk_one_hot
    kx_ref[1, :, HEAD_W:] = chunk_one_hot
    for t in range(n_k):
        vt_ref[t] = v_ref[0, 0, t * tk:(t + 1) * tk, :].T.astype(BF16)
    bias_row = lax.broadcasted_iota(jnp.int32, (HEAD_W, tq), 0)
    bias_col = lax.broadcasted_iota(jnp.int32, (HEAD_W, tq), 1) // CHUNK
    for qi in range(n_q):
        q_t = q_ref[0, 0, qi * tq:(qi + 1) * tq, :].astype(F32).T.astype(BF16)
        hidden = (bias_row < chunks_k) & (bias_row > bias_col + chunks_q * (qi % q_per_k))
        for diag in (0, 1):
            qx_ref[2 * qi + diag, :HEAD_W, :] = q_t
            bias = jnp.where(hidden, NEG_INF, 0.0) if diag else jnp.zeros((HEAD_W, tq), F32)
            qx_ref[2 * qi + diag, HEAD_W:, :] = bias.astype(BF16)
    lam = _diff_lambda(lq1_ref, lk1_ref, lq2_ref, lk2_ref, lam_init)

    def tile_of(i):
        if isinstance(i, int):
            j, qx, is_first, is_last, qi = tiles[i]
            return j, qx, bool(is_first), bool(is_last), qi
        j, qx, is_first, is_last, qi = (tab_ref[i, f] for f in range(5))
        return j, qx, is_first == 1, is_last == 1, qi

    def stage_a(i, slot):
        j, qx, _, _, _ = tile_of(i)
        rows = pl.ds(j * tk if isinstance(j, int) else pl.multiple_of(j * tk, tk), tk)
        tops = []
        for mp in range(2):
            s = _dot(kx_ref[mp, rows, :], qx_ref[qx])
            s_ref[slot, mp] = s
            tops.append(jnp.max(s, axis=0, keepdims=True))
        return tuple(tops)

    def stage_b(i, slot, m, l, top):
        _, _, is_first, _, _ = tile_of(i)
        out = []
        for mp in range(2):
            m_prev = jnp.where(is_first, NEG_INF, m[mp])
            m_new = jnp.maximum(m_prev, top[mp])
            scale = jnp.exp2(m_prev - m_new)
            p = jnp.exp2(s_ref[slot, mp] - m_new)
            p_ref[slot, mp] = p.astype(BF16)
            out.append((m_new, scale * l[mp] + jnp.sum(p, axis=0, keepdims=True), scale))
        return tuple(zip(*out))

    def stage_c(i, slot, scale):
        j = tile_of(i)[0]
        for mp in range(2):
            acc_ref[mp] = scale[mp] * acc_ref[mp] + _dot(vt_ref[j], p_ref[slot, mp])

    def finish_if_last(i, l):
        _, _, _, is_last, qi = tile_of(i)

        def finish():
            o = (acc_ref[0] / l[0] - lam * (acc_ref[1] / l[1])).T
            rows = pl.ds(qi * tq if isinstance(qi, int) else pl.multiple_of(qi * tq, tq), tq)
            o_ref[0, rows, :] = _diff_finish(o, g_ref, lam_init).astype(o_ref.dtype)

        if isinstance(is_last, bool):
            if is_last:
                finish()
        else:
            pl.when(is_last)(finish)

    def iteration(i, slot, carry, a=True, b=True, c=True):
        m, l, top, scale, l_done = carry
        new_scale, new_l_done = scale, l_done
        if c:
            stage_c(i - 2, slot, scale)
        if b:
            m, l, new_scale = stage_b(i - 1, 1 - slot, m, l, top)
            new_l_done = l
        if a:
            top = stage_a(i, slot)
        if c:
            finish_if_last(i - 2, l_done)
        return m, l, top, new_scale, new_l_done

    def pair(n, carry):
        i = 2 + 2 * n
        return iteration(i + 1, 1, iteration(i, 0, carry))

    zero = (jnp.zeros((1, tq), F32),) * 2
    acc_ref[...] = jnp.zeros(acc_ref.shape, F32)
    carry = (zero, zero, zero, zero, zero)
    carry = iteration(0, 0, carry, b=False, c=False)
    carry = iteration(1, 1, carry, c=False)
    carry = lax.fori_loop(0, (n_tiles - 2) // 2, pair, carry)
    carry = iteration(n_tiles, 0, carry, a=False)
    iteration(n_tiles + 1, 1, carry, a=False, b=False)


def _diff_attn_prompt(q, k, v, lam_vecs, subln_g, lam_init, *, tq, tk):
    b, _, s, _ = q.shape
    rows = _causal_tiles(s // tq, tk // tq)
    assert len(rows) % 2 == 0
    table = jnp.asarray(np.array(rows, np.int32))
    kern = functools.partial(_diff_attn_prompt_kernel, tq=tq, tk=tk, lam_init=lam_init)
    const = lambda shape: pl.BlockSpec(shape, lambda i, h, tab: (0,) * len(shape))
    head_spec = pl.BlockSpec((1, 1, s, HEAD_W), lambda i, h, tab: (i, h, 0, 0))
    return pl.pallas_call(
        kern,
        grid_spec=pltpu.PrefetchScalarGridSpec(
            num_scalar_prefetch=1,
            grid=(b, DIFF_HEADS),
            in_specs=[head_spec, head_spec, head_spec] + [const((1, DIFF_HD))] * 4 + [const((1, HEAD_W))],
            out_specs=pl.BlockSpec((1, s, HEAD_W), lambda i, h, tab: (i, 0, h)),
            scratch_shapes=[
                pltpu.VMEM((2, s, 2 * HEAD_W), BF16),
                pltpu.VMEM((s // tk, HEAD_W, tk), BF16),
                pltpu.VMEM((2 * (s // tq), 2 * HEAD_W, tq), BF16),
                pltpu.VMEM((2, 2, tk, tq), F32),
                pltpu.VMEM((2, 2, tk, tq), BF16),
                pltpu.VMEM((2, HEAD_W, tq), F32),
            ]),
        out_shape=jax.ShapeDtypeStruct((b, s, MIX_W), BF16),
        name="diff_attn_prompt",
        compiler_params=_params("arbitrary", "arbitrary"),
    )(table, q, k, v, *lam_vecs, subln_g.reshape(1, HEAD_W))


def _diff_attn_sample_kernel(q_ref, kc_ref, vc_ref, kn_ref, vn_ref, lq1_ref, lk1_ref, lq2_ref, lk2_ref,
                             g_ref, o_ref, *, lam_init):
    q = q_ref[0, 0]
    kc1, kc2 = _split_heads(kc_ref[0, 0])
    kn1, kn2 = _split_heads(kn_ref[0, 0])
    vc = vc_ref[0, 0]
    vn = vn_ref[0, 0]

    def one_map(kc, kn):
        sc = _dot_nt(q, kc)
        sn = _dot_nt(q, kn)
        m = jnp.maximum(jnp.max(sc, axis=-1, keepdims=True), jnp.max(sn, axis=-1, keepdims=True))
        pc = jnp.exp2(sc - m)
        pn = jnp.exp2(sn - m)
        l = jnp.sum(pc, axis=-1, keepdims=True) + jnp.sum(pn, axis=-1, keepdims=True)
        return (_dot(pc, vc) + _dot(pn, vn)) / l

    lam = _diff_lambda(lq1_ref, lk1_ref, lq2_ref, lk2_ref, lam_init)
    o = one_map(kc1, kn1) - lam * one_map(kc2, kn2)
    o_ref[0] = _diff_finish(o, g_ref, lam_init).astype(o_ref.dtype)


def _diff_attn_sample(q, k_cache, v_cache, k_new, v_new, lam_vecs, subln_g, lam_init):
    b, _, s, _ = q.shape
    past = k_cache.shape[2]
    kern = functools.partial(_diff_attn_sample_kernel, lam_init=lam_init)
    vec_spec = _const_spec((1, DIFF_HD))
    new_spec = pl.BlockSpec((1, 1, s, HEAD_W), lambda i, h: (i, h, 0, 0))
    old_spec = pl.BlockSpec((1, 1, past, HEAD_W), lambda i, h: (i, h, 0, 0))
    return pl.pallas_call(
        kern,
        grid=(b, DIFF_HEADS),
        in_specs=[new_spec, old_spec, old_spec, new_spec, new_spec, vec_spec, vec_spec, vec_spec,
                  vec_spec, _const_spec((1, HEAD_W))],
        out_specs=pl.BlockSpec((1, s, HEAD_W), lambda i, h: (i, 0, h)),
        out_shape=jax.ShapeDtypeStruct((b, s, MIX_W), BF16),
        name="diff_attn_sample",
        compiler_params=_params("arbitrary", "arbitrary"),
    )(q, k_cache, v_cache, k_new, v_new, *lam_vecs, subln_g.reshape(1, HEAD_W))


def _post_kernel(x_ref, o_ref, m_ref, wo_ref, g1_ref, b1_ref, wg_ref, wu_ref, wd_ref, g2_ref, b2_ref,
                 y_ref, h_ref, *, ff_chunk):
    mixed = _dot(o_ref[...], wo_ref[0, :MIX_W, :]) + _dot(m_ref[...], wo_ref[0, MIX_W:, :])
    x1 = _layer_norm_rows(ALPHA * x_ref[...] + mixed, g1_ref[0], b1_ref[0])
    x1b = x1.astype(BF16)
    for c in range(D_FF // ff_chunk):
        cols = slice(c * ff_chunk, (c + 1) * ff_chunk)
        gate = _dot(x1b, wg_ref[0, :, cols])
        up = _dot(x1b, wu_ref[0, :, cols])
        h_ref[:, cols] = (gate / (1.0 + jnp.exp(-gate)) * up).astype(BF16)
    ff = _dot(h_ref[...], wd_ref[0])
    y_ref[...] = _layer_norm_rows(ALPHA * x1 + ff, g2_ref[0], b2_ref[0])


def _post(x, o, m, layer, wo, g1, b1, wg, wu, wd, g2, b2, *, tm, ff_chunk):
    n = x.shape[0]
    row = lambda w: pl.BlockSpec((tm, w), lambda i: (i, 0))
    vec = _layer_spec((1, D_MODEL), layer)
    kern = functools.partial(_post_kernel, ff_chunk=ff_chunk)
    r1 = lambda a: a.reshape(DEPTH, 1, D_MODEL)
    return pl.pallas_call(
        kern,
        grid=(n // tm,),
        in_specs=[row(D_MODEL), row(MIX_W), row(MEM_W), _layer_spec((D_MODEL, D_MODEL), layer), vec, vec,
                  _layer_spec((D_MODEL, D_FF), layer), _layer_spec((D_MODEL, D_FF), layer),
                  _layer_spec((D_FF, D_MODEL), layer), vec, vec],
        out_specs=row(D_MODEL),
        out_shape=jax.ShapeDtypeStruct((n, D_MODEL), F32),
        scratch_shapes=[pltpu.VMEM((tm, D_FF), BF16)],
        name="post_mixer",
        compiler_params=_params("arbitrary"),
    )(x, o, m, wo, r1(g1), r1(b1), wg, wu, wd, r1(g2), r1(b2))


def kernel(x_prompt, x_sample, mem_prompt, cache_ret_state, cache_diff_k, cache_diff_v, cache_mem_k,
           cache_mem_v, ret_w_in, ret_gn_g, diff_w_in, diff_lambda_q1, diff_lambda_k1, diff_lambda_q2,
           diff_lambda_k2, diff_subln_g, w_mem_kv, w_o, ln1_g, ln1_b, w_gate, w_up, w_down, ln2_g, ln2_b):
    bp, sp, _ = x_prompt.shape
    bs, ss, _ = x_sample.shape
    pos_p = jnp.arange(sp)
    pos_s = PAST_LEN + jnp.arange(ss)
    bf = lambda a: a.astype(BF16)

    mem_k_p, mem_v_p = _mem_kv(mem_prompt, bf(w_mem_kv.transpose(0, 2, 1)))
    mem_t = lambda a: a.transpose(0, 1, 3, 4, 2).reshape(DEPTH, bs, MEM_W, N_MEM)
    mem_k_s, mem_v_s = mem_t(cache_mem_k), mem_t(cache_mem_v)

    ret_w, diff_w = bf(ret_w_in), bf(diff_w_in)
    wo, wg, wu, wd = bf(w_o), bf(w_gate), bf(w_up), bf(w_down)

    xp, xs = x_prompt, x_sample
    ret_p, ret_s, dkp, dvp, dks, dvs = [], [], [], [], [], []
    to_seq_major = lambda a: a.transpose(0, 2, 1, 3)
    for i in range(DEPTH):
        j = i // 2
        if i % 2 == 0:
            r0 = jnp.zeros((bp, RET_HEADS, RET_HD, RET_HD), F32)
            op, mp, rp = _ret_mixer(xp, pos_p, r0, mem_k_p, mem_v_p, i, ret_w, j, ret_gn_g[j],
                                    t_step=512, tile=256)
            os_, ms, rs = _ret_mixer(xs, pos_s, cache_ret_state[j], mem_k_s, mem_v_s, i, ret_w, j,
                                     ret_gn_g[j], t_step=ss, tile=ss)
            ret_p.append(rp)
            ret_s.append(rs)
        else:
            lam_init = _lambda_init(i)
            lam_vecs = [a[j].reshape(1, DIFF_HD) for a in
                        (diff_lambda_q1, diff_lambda_k1, diff_lambda_q2, diff_lambda_k2)]
            q, k, v, mp = _diff_project(xp, pos_p, mem_k_p, mem_v_p, i, diff_w, j, t_step=512)
            op = _diff_attn_prompt(q, k, v, lam_vecs, diff_subln_g[j], lam_init, tq=256, tk=512)
            dkp.append(to_seq_major(k))
            dvp.append(to_seq_major(v))
            q, k, v, ms = _diff_project(xs, pos_s, mem_k_s, mem_v_s, i, diff_w, j, t_step=ss)
            os_ = _diff_attn_sample(q, cache_diff_k[j].transpose(0, 2, 1, 3),
                                    cache_diff_v[j].transpose(0, 2, 1, 3), k, v, lam_vecs,
                                    diff_subln_g[j], lam_init)
            dks.append(to_seq_major(k))
            dvs.append(to_seq_major(v))
        post = functools.partial(_post, layer=i, wo=wo, g1=ln1_g, b1=ln1_b, wg=wg, wu=wu, wd=wd,
                                 g2=ln2_g, b2=ln2_b, tm=512, ff_chunk=256)
        xp = post(xp.reshape(bp * sp, D_MODEL), op.reshape(bp * sp, MIX_W),
                  mp.reshape(bp * sp, MEM_W)).reshape(bp, sp, D_MODEL)
        xs = post(xs.reshape(bs * ss, D_MODEL), os_.reshape(bs * ss, MIX_W),
                  ms.reshape(bs * ss, MEM_W)).reshape(bs, ss, D_MODEL)

    mem_out = lambda a: a.reshape(DEPTH, bp, MEM_HEADS, MEM_HD, N_MEM).transpose(0, 1, 4, 2, 3)
    return (xp, xs, jnp.stack(ret_p), jnp.stack(ret_s), jnp.stack(dkp), jnp.stack(dvp),
            jnp.stack(dks), jnp.stack(dvs), mem_out(mem_k_p), mem_out(mem_v_p))
```

```python
import functools
import math

import jax
import jax.numpy as jnp
import numpy as np
from jax import lax
from jax.experimental import pallas as pl
from jax.experimental.pallas import tpu as pltpu

D_MODEL = 1024
DEPTH = 2
PAST_LEN = 1024
CHUNK = 64
N_MEM = 256
MEM_HEADS = 4
MEM_HD = 64
MEM_W = MEM_HEADS * MEM_HD
MIX_W = D_MODEL - MEM_W
RET_HEADS = 6
RET_HD = MIX_W // RET_HEADS
RET_THETA = 10000.0
DIFF_HEADS = 6
DIFF_HD = MIX_W // (2 * DIFF_HEADS)
ROPE_THETA = 500000.0
ROT_DIM = DIFF_HD // 4
D_FF = -(-8 * D_MODEL // (3 * 256)) * 256
ALPHA = (2 * DEPTH) ** 0.25
LN_EPS = 1e-5
NEG_INF = -1e30

Q_SCALE = DIFF_HD ** -0.5 * math.log2(math.e)
SUB_ROWS = 512
HEAD_W = 128
VMEM_LIMIT = 56 * 1024 * 1024

F32 = jnp.float32
BF16 = jnp.bfloat16


def _lambda_init(layer_idx):
    return 0.8 - 0.6 * math.exp(-0.3 * layer_idx)


def _dot(a, b):
    return jnp.dot(a.astype(BF16), b.astype(BF16), preferred_element_type=F32)


def _dot_nt(a, b):
    return lax.dot_general(a.astype(BF16), b.astype(BF16), (((1,), (1,)), ((), ())),
                           preferred_element_type=F32)


def _dot_tn(a, b):
    return lax.dot_general(a.astype(BF16), b.astype(BF16), (((0,), (0,)), ((), ())),
                           preferred_element_type=F32)


def _fixed_spec(shape, index):
    return pl.BlockSpec(shape, lambda *_: index, pipeline_mode=pl.Buffered(1))


def _const_spec(shape):
    return _fixed_spec(shape, (0,) * len(shape))


def _layer_spec(shape, layer):
    return _fixed_spec((1,) + shape, (layer,) + (0,) * len(shape))


def _params(*semantics):
    return pltpu.CompilerParams(dimension_semantics=semantics, vmem_limit_bytes=VMEM_LIMIT)


def _layer_norm_rows(y, g, b):
    mu = jnp.mean(y, axis=-1, keepdims=True)
    d = y - mu
    var = jnp.mean(d * d, axis=-1, keepdims=True)
    return d * lax.rsqrt(var + LN_EPS) * g + b


def _softmax_rows(s):
    s = s - jnp.max(s, axis=-1, keepdims=True)
    p = jnp.exp(s)
    return p / jnp.sum(p, axis=-1, keepdims=True)


def _memory_attention(mq, mk_t, mv_t):
    row = lax.broadcasted_iota(jnp.int32, mk_t.shape, 0)
    mqb = (mq * MEM_HD ** -0.5).astype(BF16)
    out = None
    for h in range(MEM_HEADS):
        sel = (row >= h * MEM_HD) & (row < (h + 1) * MEM_HD)
        p = _softmax_rows(_dot(mqb, jnp.where(sel, mk_t, 0.0)))
        o = _dot_nt(p, jnp.where(sel, mv_t, 0.0))
        out = o if out is None else out + o
    return out


def _mem_kv_kernel(x_ref, w_ref, k_ref, v_ref):
    kv_t = _dot_nt(w_ref[0], x_ref[0])
    k_ref[0, 0] = kv_t[:MEM_W]
    v_ref[0, 0] = kv_t[MEM_W:]


def _mem_kv(mem, w_t_bf):
    b = mem.shape[0]
    out = jax.ShapeDtypeStruct((DEPTH, b, MEM_W, N_MEM), F32)
    return pl.pallas_call(
        _mem_kv_kernel,
        grid=(DEPTH, b),
        in_specs=[pl.BlockSpec((1, N_MEM, D_MODEL), lambda l, i: (i, 0, 0)),
                  pl.BlockSpec((1, 2 * MEM_W, D_MODEL), lambda l, i: (l, 0, 0))],
        out_specs=[pl.BlockSpec((1, 1, MEM_W, N_MEM), lambda l, i: (l, i, 0, 0))] * 2,
        out_shape=[out, out],
        name="mem_kv",
        compiler_params=_params("arbitrary", "arbitrary"),
    )(mem, w_t_bf)


def _ret_tables(tile):
    h = np.arange(RET_HEADS, dtype=np.float64)
    log_g = np.log(1.0 - np.exp2(-5.0 - h))
    idx = np.arange(tile, dtype=np.float64)
    dist = np.abs(idx[:, None] - idx[None, :])
    visible = (idx[None, :] // CHUNK) <= (idx[:, None] // CHUNK)
    decay = np.where(visible[None], np.exp(dist[None] * log_g[:, None, None]), 0.0)
    xi = np.exp((idx + 1.0)[None, :] * log_g[:, None])[:, :, None]
    zeta = np.exp((tile - 1.0 - idx)[None, :] * log_g[:, None])[:, :, None]
    g_tile = np.exp(tile * log_g)
    f = lambda a: jnp.asarray(a, F32)
    return f(decay), f(xi), f(zeta), [float(g) for g in g_tile]


def _ret_kernel(x_ref, w_ref, cos_ref, sin_ref, r0_ref, mk_ref, mv_ref, gn_ref, decay_ref, xi_ref,
                zeta_ref, o_ref, m_ref, r_ref, state_ref, *, tile, g_tile):
    step = pl.program_id(1)

    @pl.when(step == 0)
    def _():
        state_ref[...] = r0_ref[0]

    proj = _dot(x_ref[0], w_ref[0])
    t = proj.shape[0]
    cos = cos_ref[...]
    sin = sin_ref[...]

    def rope(a):
        return a * cos + pltpu.roll(a, RET_HD // 2, 1) * sin

    for h in range(RET_HEADS):
        col = h * HEAD_W
        q = rope(proj[:, col:col + HEAD_W])
        k = rope(proj[:, MIX_W + col:MIX_W + col + HEAD_W]) * RET_HD ** -0.5
        v = proj[:, 2 * MIX_W + col:2 * MIX_W + col + HEAD_W]
        gate = proj[:, 3 * MIX_W + col:3 * MIX_W + col + HEAD_W]
        outs = []
        r = state_ref[h]
        for c in range(t // tile):
            rows = slice(c * tile, (c + 1) * tile)
            qc, kc, vc = q[rows], k[rows], v[rows]
            inner = _dot_nt(qc, kc) * decay_ref[h]
            outs.append(_dot(inner, vc) + _dot(qc, r) * xi_ref[h])
            r = g_tile[h] * r + _dot_tn(kc * zeta_ref[h], vc)
        state_ref[h] = r
        o = outs[0] if len(outs) == 1 else jnp.concatenate(outs, axis=0)
        mu = jnp.mean(o, axis=-1, keepdims=True)
        d = o - mu
        var = jnp.mean(d * d, axis=-1, keepdims=True)
        o = d * lax.rsqrt(var + LN_EPS) * gn_ref[:, col:col + HEAD_W]
        o = o * (gate / (1.0 + jnp.exp(-gate)))
        o_ref[0, :, col:col + HEAD_W] = o.astype(o_ref.dtype)

    m = _memory_attention(proj[:, 4 * MIX_W:], mk_ref[0, 0], mv_ref[0, 0])
    m_ref[0] = m.astype(m_ref.dtype)

    @pl.when(step == pl.num_programs(1) - 1)
    def _():
        r_ref[0] = state_ref[...]


def _ret_mixer(x, pos, r0, mk_t, mv_t, layer, w_bf, w_layer, gn_g, *, t_step, tile):
    b, s, _ = x.shape
    half = RET_HD // 2
    lane_freq = jnp.arange(HEAD_W, dtype=jnp.int32) % half
    inv_freq = jnp.exp(-math.log(RET_THETA) * lane_freq.astype(F32) * 2.0 / RET_HD)
    ang = pos.astype(F32)[:, None] * inv_freq[None, :]
    cos = jnp.cos(ang)
    sin = jnp.where(jnp.arange(HEAD_W) < half, -jnp.sin(ang), jnp.sin(ang))
    decay, xi, zeta, g_tile = _ret_tables(tile)
    cols = w_bf.shape[-1]
    kern = functools.partial(_ret_kernel, tile=tile, g_tile=g_tile)
    mem_spec = pl.BlockSpec((1, 1, MEM_W, N_MEM), lambda i, j: (layer, i, 0, 0))
    return pl.pallas_call(
        kern,
        grid=(b, s // t_step),
        in_specs=[
            pl.BlockSpec((1, t_step, D_MODEL), lambda i, j: (i, j, 0)),
            _layer_spec((D_MODEL, cols), w_layer),
            pl.BlockSpec((t_step, HEAD_W), lambda i, j: (j, 0)),
            pl.BlockSpec((t_step, HEAD_W), lambda i, j: (j, 0)),
            pl.BlockSpec((1, RET_HEADS, RET_HD, RET_HD), lambda i, j: (i, 0, 0, 0)),
            mem_spec, mem_spec,
            _const_spec((1, MIX_W)),
            _const_spec((RET_HEADS, tile, tile)),
            _const_spec((RET_HEADS, tile, 1)),
            _const_spec((RET_HEADS, tile, 1)),
        ],
        out_specs=[
            pl.BlockSpec((1, t_step, MIX_W), lambda i, j: (i, j, 0)),
            pl.BlockSpec((1, t_step, MEM_W), lambda i, j: (i, j, 0)),
            pl.BlockSpec((1, RET_HEADS, RET_HD, RET_HD), lambda i, j: (i, 0, 0, 0)),
        ],
        out_shape=[
            jax.ShapeDtypeStruct((b, s, MIX_W), BF16),
            jax.ShapeDtypeStruct((b, s, MEM_W), BF16),
            jax.ShapeDtypeStruct((b, RET_HEADS, RET_HD, RET_HD), F32),
        ],
        scratch_shapes=[pltpu.VMEM((RET_HEADS, RET_HD, RET_HD), F32)],
        name="ret_mixer",
        compiler_params=_params("arbitrary", "arbitrary"),
    )(x, w_bf, cos, sin, r0, mk_t, mv_t, gn_g.reshape(1, MIX_W), decay, xi, zeta)


def _diff_proj_kernel(x_ref, w_ref, c_ref, sa_ref, sb_ref, mk_ref, mv_ref, q_ref, k_ref, v_ref, m_ref,
                      *, row_groups):
    t = x_ref.shape[1]
    for r in range(row_groups):
        rows = slice(r * t // row_groups, (r + 1) * t // row_groups)
        proj = _dot(x_ref[0, rows, :], w_ref[0])
        c = c_ref[rows, :]
        sa = sa_ref[rows, :]
        sb = sb_ref[rows, :]

        def rope(a):
            return (a * c + pltpu.roll(a, ROT_DIM // 2, 1) * sa
                    + pltpu.roll(a, HEAD_W - ROT_DIM // 2, 1) * sb)

        for h in range(DIFF_HEADS):
            col = h * HEAD_W
            q = rope(proj[:, col:col + HEAD_W]) * Q_SCALE
            q_ref[0, h, rows, :] = q.astype(q_ref.dtype)
            k_ref[0, h, rows, :] = rope(proj[:, MIX_W + col:MIX_W + col + HEAD_W])
            v_ref[0, h, rows, :] = proj[:, 2 * MIX_W + col:2 * MIX_W + col + HEAD_W]
        m = _memory_attention(proj[:, 3 * MIX_W:], mk_ref[0, 0], mv_ref[0, 0])
        m_ref[0, rows, :] = m.astype(m_ref.dtype)


def _diff_project(x, pos, mk_t, mv_t, layer, w_bf, w_layer, *, t_step, row_groups):
    b, s, _ = x.shape
    half = ROT_DIM // 2
    lane = jnp.arange(HEAD_W, dtype=jnp.int32) % DIFF_HD
    inv_freq = jnp.exp(-math.log(ROPE_THETA) * (lane % half).astype(F32) * 2.0 / ROT_DIM)
    ang = pos.astype(F32)[:, None] * inv_freq[None, :]
    cos, sin = jnp.cos(ang), jnp.sin(ang)
    tables = [jnp.where(lane < ROT_DIM, cos, 1.0),
              jnp.where((lane >= half) & (lane < ROT_DIM), sin, 0.0),
              jnp.where(lane < half, -sin, 0.0)]
    cols = w_bf.shape[-1]
    tab_spec = pl.BlockSpec((t_step, HEAD_W), lambda i, j: (j, 0))
    tok_spec = lambda w: pl.BlockSpec((1, t_step, w), lambda i, j: (i, j, 0))
    head_spec = pl.BlockSpec((1, DIFF_HEADS, t_step, HEAD_W), lambda i, j: (i, 0, j, 0))
    mem_spec = pl.BlockSpec((1, 1, MEM_W, N_MEM), lambda i, j: (layer, i, 0, 0))
    heads = lambda dt: jax.ShapeDtypeStruct((b, DIFF_HEADS, s, HEAD_W), dt)
    return pl.pallas_call(
        functools.partial(_diff_proj_kernel, row_groups=row_groups),
        grid=(b, s // t_step),
        in_specs=[tok_spec(D_MODEL), _layer_spec((D_MODEL, cols), w_layer), tab_spec, tab_spec,
                  tab_spec, mem_spec, mem_spec],
        out_specs=[head_spec, head_spec, head_spec, tok_spec(MEM_W)],
        out_shape=[
            heads(BF16),
            heads(F32),
            heads(F32),
            jax.ShapeDtypeStruct((b, s, MEM_W), BF16),
        ],
        name="diff_proj",
        compiler_params=_params("arbitrary", "arbitrary"),
    )(x, w_bf, *tables, mk_t, mv_t)


def _diff_lambda(lq1_ref, lk1_ref, lq2_ref, lk2_ref, lam_init):
    a = jnp.sum(lq1_ref[...] * lk1_ref[...], axis=-1, keepdims=True)
    b = jnp.sum(lq2_ref[...] * lk2_ref[...], axis=-1, keepdims=True)
    return jnp.exp(a) - jnp.exp(b) + lam_init


def _diff_finish(o, g_ref, lam_init):
    ms = jnp.mean(o * o, axis=-1, keepdims=True)
    return o * lax.rsqrt(ms + LN_EPS) * g_ref[...] * (1.0 - lam_init)


def _split_heads(k):
    lane = lax.broadcasted_iota(jnp.int32, k.shape, 1)
    first = lane < DIFF_HD
    return jnp.where(first, k, 0.0).astype(BF16), jnp.where(first, 0.0, k).astype(BF16)


def _causal_tiles(n_q, q_per_k):
    return [(j, 2 * qi + int(j == qi // q_per_k), int(j == 0), int(j == qi // q_per_k), qi)
            for qi in range(n_q) for j in range(qi // q_per_k + 1)]


def _diff_attn_prompt_kernel(tab_ref, q_ref, k_ref, v_ref, lq1_ref, lk1_ref, lq2_ref, lk2_ref, g_ref,
                             o_ref, kx_ref, vt_ref, qx_ref, s_ref, p_ref, acc_ref, *, tq, tk, lam_init):
    seq = q_ref.shape[2]
    n_q, n_k = seq // tq, seq // tk
    q_per_k = tk // tq
    tiles = _causal_tiles(n_q, q_per_k)
    n_tiles = len(tiles)
    chunks_q, chunks_k = tq // CHUNK, tk // CHUNK

    k = k_ref[0, 0]
    lane = lax.broadcasted_iota(jnp.int32, k.shape, 1)
    row_chunk = (lax.broadcasted_iota(jnp.int32, k.shape, 0) // CHUNK) % chunks_k
    chunk_one_hot = jnp.where(lane == row_chunk, 1.0, 0.0).astype(BF16)
    first = lane < DIFF_HD
    kx_ref[0, :, :HEAD_W] = jnp.where(first, k, 0.0).astype(BF16)
    kx_ref[1, :, :HEAD_W] = jnp.where(first, 0.0, k).astype(BF16)
    kx_ref[0, :, HEAD_W:] = chunk_one_hot
    kx_ref[1, :, HEAD_W:] = chunk_one_hot
    for t in range(n_k):
        vt_ref[t] = v_ref[0, 0, t * tk:(t + 1) * tk, :].T.astype(BF16)
    bias_row = lax.broadcasted_iota(jnp.int32, (HEAD_W, tq), 0)
    bias_col = lax.broadcasted_iota(jnp.int32, (HEAD_W, tq), 1) // CHUNK
    for qi in range(n_q):
        q_t = q_ref[0, 0, qi * tq:(qi + 1) * tq, :].astype(F32).T.astype(BF16)
        hidden = (bias_row < chunks_k) & (bias_row > bias_col + chunks_q * (qi % q_per_k))
        for diag in (0, 1):
            qx_ref[2 * qi + diag, :HEAD_W, :] = q_t
            bias = jnp.where(hidden, NEG_INF, 0.0) if diag else jnp.zeros((HEAD_W, tq), F32)
            qx_ref[2 * qi + diag, HEAD_W:, :] = bias.astype(BF16)
    lam = _diff_lambda(lq1_ref, lk1_ref, lq2_ref, lk2_ref, lam_init)

    def tile_of(i):
        if isinstance(i, int):
            j, qx, is_first, is_last, qi = tiles[i]
            return j, qx, bool(is_first), bool(is_last), qi
        j, qx, is_first, is_last, qi = (tab_ref[i, f] for f in range(5))
        return j, qx, is_first == 1, is_last == 1, qi

    def stage_a(i, slot):
        j, qx, _, _, _ = tile_of(i)
        base = j * tk if isinstance(j, int) else pl.multiple_of(j * tk, tk)
        tops = []
        for mp in range(2):
            top = None
            for r in range(0, tk, SUB_ROWS):
                s = _dot(kx_ref[mp, pl.ds(base + r, SUB_ROWS), :], qx_ref[qx])
                s_ref[slot, mp, r:r + SUB_ROWS, :] = s
                t = jnp.max(s, axis=0, keepdims=True)
                top = t if top is None else jnp.maximum(top, t)
            tops.append(top)
        return tuple(tops)

    def stage_b(i, slot, m, l, top):
        _, _, is_first, _, _ = tile_of(i)
        out = []
        for mp in range(2):
            m_prev = jnp.where(is_first, NEG_INF, m[mp])
            m_new = jnp.maximum(m_prev, top[mp])
            scale = jnp.exp2(m_prev - m_new)
            l_new = scale * l[mp]
            for r in range(0, tk, SUB_ROWS):
                p = jnp.exp2(s_ref[slot, mp, r:r + SUB_ROWS, :] - m_new)
                p_ref[slot, mp, r:r + SUB_ROWS, :] = p.astype(BF16)
                l_new = l_new + jnp.sum(p, axis=0, keepdims=True)
            out.append((m_new, l_new, scale))
        return tuple(zip(*out))

    def stage_c(i, slot, scale):
        j = tile_of(i)[0]
        for mp in range(2):
            acc_ref[mp] = scale[mp] * acc_ref[mp] + _dot(vt_ref[j], p_ref[slot, mp])

    def finish_if_last(i, l):
        _, _, _, is_last, qi = tile_of(i)

        def finish():
            o = (acc_ref[0] / l[0] - lam * (acc_ref[1] / l[1])).T
            rows = pl.ds(qi * tq if isinstance(qi, int) else pl.multiple_of(qi * tq, tq), tq)
            o_ref[0, rows, :] = _diff_finish(o, g_ref, lam_init).astype(o_ref.dtype)

        if isinstance(is_last, bool):
            if is_last:
                finish()
        else:
            pl.when(is_last)(finish)

    def iteration(i, slot, carry, a=True, b=True, c=True):
        m, l, top, scale, l_done = carry
        new_scale, new_l_done = scale, l_done
        if b:
            m, l, new_scale = stage_b(i - 1, 1 - slot, m, l, top)
            new_l_done = l
        if a:
            top = stage_a(i, slot)
        if c:
            stage_c(i - 2, slot, scale)
            finish_if_last(i - 2, l_done)
        return m, l, top, new_scale, new_l_done

    def pair(n, carry):
        i = 2 + 2 * n
        return iteration(i + 1, 1, iteration(i, 0, carry))

    zero = (jnp.zeros((1, tq), F32),) * 2
    acc_ref[...] = jnp.zeros(acc_ref.shape, F32)
    carry = (zero, zero, zero, zero, zero)
    carry = iteration(0, 0, carry, b=False, c=False)
    carry = iteration(1, 1, carry, c=False)
    carry = lax.fori_loop(0, (n_tiles - 2) // 2, pair, carry)
    carry = iteration(n_tiles, 0, carry, a=False)
    iteration(n_tiles + 1, 1, carry, a=False, b=False)


def _diff_attn_prompt(q, k, v, lam_vecs, subln_g, lam_init, *, tq, tk):
    b, _, s, _ = q.shape
    rows = _causal_tiles(s // tq, tk // tq)
    assert len(rows) % 2 == 0
    table = jnp.asarray(np.array(rows, np.int32))
    kern = functools.partial(_diff_attn_prompt_kernel, tq=tq, tk=tk, lam_init=lam_init)
    const = lambda shape: pl.BlockSpec(shape, lambda i, h, tab: (0,) * len(shape))
    head_spec = pl.BlockSpec((1, 1, s, HEAD_W), lambda i, h, tab: (i, h, 0, 0))
    return pl.pallas_call(
        kern,
        grid_spec=pltpu.PrefetchScalarGridSpec(
            num_scalar_prefetch=1,
            grid=(b, DIFF_HEADS),
            in_specs=[head_spec, head_spec, head_spec] + [const((1, DIFF_HD))] * 4 + [const((1, HEAD_W))],
            out_specs=pl.BlockSpec((1, s, HEAD_W), lambda i, h, tab: (i, 0, h)),
            scratch_shapes=[
                pltpu.VMEM((2, s, 2 * HEAD_W), BF16),
                pltpu.VMEM((s // tk, HEAD_W, tk), BF16),
                pltpu.VMEM((2 * (s // tq), 2 * HEAD_W, tq), BF16),
                pltpu.VMEM((2, 2, tk, tq), F32),
                pltpu.VMEM((2, 2, tk, tq), BF16),
                pltpu.VMEM((2, HEAD_W, tq), F32),
            ]),
        out_shape=jax.ShapeDtypeStruct((b, s, MIX_W), BF16),
        name="diff_attn_prompt",
        compiler_params=_params("arbitrary", "arbitrary"),
    )(table, q, k, v, *lam_vecs, subln_g.reshape(1, HEAD_W))


def _diff_attn_sample_kernel(q_ref, kc_ref, vc_ref, kn_ref, vn_ref, lq1_ref, lk1_ref, lq2_ref, lk2_ref,
                             g_ref, o_ref, *, lam_init):
    q = q_ref[0, 0]
    s = q.shape[0]
    q1, q2 = _split_heads(q)
    qq = jnp.concatenate([q1, q2], axis=0)
    sc = _dot_nt(qq, kc_ref[0, 0])
    sn = _dot_nt(qq, kn_ref[0, 0])
    m = jnp.maximum(jnp.max(sc, axis=-1, keepdims=True), jnp.max(sn, axis=-1, keepdims=True))
    pc = jnp.exp2(sc - m)
    pn = jnp.exp2(sn - m)
    l = jnp.sum(pc, axis=-1, keepdims=True) + jnp.sum(pn, axis=-1, keepdims=True)
    o2 = (_dot(pc, vc_ref[0, 0]) + _dot(pn, vn_ref[0, 0])) / l
    lam = _diff_lambda(lq1_ref, lk1_ref, lq2_ref, lk2_ref, lam_init)
    o = o2[:s] - lam * o2[s:]
    o_ref[0] = _diff_finish(o, g_ref, lam_init).astype(o_ref.dtype)


def _diff_attn_sample(q, k_cache, v_cache, k_new, v_new, lam_vecs, subln_g, lam_init):
    b, _, s, _ = q.shape
    past = k_cache.shape[2]
    kern = functools.partial(_diff_attn_sample_kernel, lam_init=lam_init)
    vec_spec = _const_spec((1, DIFF_HD))
    new_spec = pl.BlockSpec((1, 1, s, HEAD_W), lambda i, h: (i, h, 0, 0))
    old_spec = pl.BlockSpec((1, 1, past, HEAD_W), lambda i, h: (i, h, 0, 0))
    return pl.pallas_call(
        kern,
        grid=(b, DIFF_HEADS),
        in_specs=[new_spec, old_spec, old_spec, new_spec, new_spec, vec_spec, vec_spec, vec_spec,
                  vec_spec, _const_spec((1, HEAD_W))],
        out_specs=pl.BlockSpec((1, s, HEAD_W), lambda i, h: (i, 0, h)),
        out_shape=jax.ShapeDtypeStruct((b, s, MIX_W), BF16),
        name="diff_attn_sample",
        compiler_params=_params("arbitrary", "arbitrary"),
    )(q, k_cache, v_cache, k_new, v_new, *lam_vecs, subln_g.reshape(1, HEAD_W))


def _post_kernel(x_ref, o_ref, m_ref, wo_ref, g1_ref, b1_ref, wg_ref, wu_ref, wd_ref, g2_ref, b2_ref,
                 y_ref, h_ref, *, ff_chunk, row_groups):
    tm = x_ref.shape[0]
    groups = [slice(r * tm // row_groups, (r + 1) * tm // row_groups) for r in range(row_groups)]
    x1, x1b = [], []
    for rows in groups:
        mixed = (_dot(o_ref[rows, :], wo_ref[0, :MIX_W, :])
                 + _dot(m_ref[rows, :], wo_ref[0, MIX_W:, :]))
        x1.append(_layer_norm_rows(ALPHA * x_ref[rows, :] + mixed, g1_ref[0], b1_ref[0]))
        x1b.append(x1[-1].astype(BF16))
    for c in range(D_FF // ff_chunk):
        cols = slice(c * ff_chunk, (c + 1) * ff_chunk)
        for r, rows in enumerate(groups):
            gate = _dot(x1b[r], wg_ref[0, :, cols])
            up = _dot(x1b[r], wu_ref[0, :, cols])
            h_ref[rows, cols] = (gate / (1.0 + jnp.exp(-gate)) * up).astype(BF16)
    for r, rows in enumerate(groups):
        ff = _dot(h_ref[rows, :], wd_ref[0])
        y_ref[rows, :] = _layer_norm_rows(ALPHA * x1[r] + ff, g2_ref[0], b2_ref[0])


def _post(x, o, m, layer, wo, g1, b1, wg, wu, wd, g2, b2, *, tm, ff_chunk, row_groups):
    n = x.shape[0]
    row = lambda w: pl.BlockSpec((tm, w), lambda i: (i, 0))
    vec = _layer_spec((1, D_MODEL), layer)
    kern = functools.partial(_post_kernel, ff_chunk=ff_chunk, row_groups=row_groups)
    r1 = lambda a: a.reshape(DEPTH, 1, D_MODEL)
    return pl.pallas_call(
        kern,
        grid=(n // tm,),
        in_specs=[row(D_MODEL), row(MIX_W), row(MEM_W), _layer_spec((D_MODEL, D_MODEL), layer), vec, vec,
                  _layer_spec((D_MODEL, D_FF), layer), _layer_spec((D_MODEL, D_FF), layer),
                  _layer_spec((D_FF, D_MODEL), layer), vec, vec],
        out_specs=row(D_MODEL),
        out_shape=jax.ShapeDtypeStruct((n, D_MODEL), F32),
        scratch_shapes=[pltpu.VMEM((tm, D_FF), BF16)],
        name="post_mixer",
        compiler_params=_params("arbitrary"),
    )(x, o, m, wo, r1(g1), r1(b1), wg, wu, wd, r1(g2), r1(b2))


def kernel(x_prompt, x_sample, mem_prompt, cache_ret_state, cache_diff_k, cache_diff_v, cache_mem_k,
           cache_mem_v, ret_w_in, ret_gn_g, diff_w_in, diff_lambda_q1, diff_lambda_k1, diff_lambda_q2,
           diff_lambda_k2, diff_subln_g, w_mem_kv, w_o, ln1_g, ln1_b, w_gate, w_up, w_down, ln2_g, ln2_b):
    bp, sp, _ = x_prompt.shape
    bs, ss, _ = x_sample.shape
    pos_p = jnp.arange(sp)
    pos_s = PAST_LEN + jnp.arange(ss)
    bf = lambda a: a.astype(BF16)

    mem_k_p, mem_v_p = _mem_kv(mem_prompt, bf(w_mem_kv.transpose(0, 2, 1)))
    mem_t = lambda a: a.transpose(0, 1, 3, 4, 2).reshape(DEPTH, bs, MEM_W, N_MEM)
    mem_k_s, mem_v_s = mem_t(cache_mem_k), mem_t(cache_mem_v)

    ret_w, diff_w = bf(ret_w_in), bf(diff_w_in)
    wo, wg, wu, wd = bf(w_o), bf(w_gate), bf(w_up), bf(w_down)

    xp, xs = x_prompt, x_sample
    ret_p, ret_s, dkp, dvp, dks, dvs = [], [], [], [], [], []
    to_seq_major = lambda a: a.transpose(0, 2, 1, 3)
    for i in range(DEPTH):
        j = i // 2
        if i % 2 == 0:
            r0 = jnp.zeros((bp, RET_HEADS, RET_HD, RET_HD), F32)
            op, mp, rp = _ret_mixer(xp, pos_p, r0, mem_k_p, mem_v_p, i, ret_w, j, ret_gn_g[j],
                                    t_step=512, tile=256)
            os_, ms, rs = _ret_mixer(xs, pos_s, cache_ret_state[j], mem_k_s, mem_v_s, i, ret_w, j,
                                     ret_gn_g[j], t_step=ss, tile=ss)
            ret_p.append(rp)
            ret_s.append(rs)
        else:
            lam_init = _lambda_init(i)
            lam_vecs = [a[j].reshape(1, DIFF_HD) for a in
                        (diff_lambda_q1, diff_lambda_k1, diff_lambda_q2, diff_lambda_k2)]
            q, k, v, mp = _diff_project(xp, pos_p, mem_k_p, mem_v_p, i, diff_w, j, t_step=512, row_groups=1)
            op = _diff_attn_prompt(q, k, v, lam_vecs, diff_subln_g[j], lam_init, tq=256, tk=512)
            dkp.append(to_seq_major(k))
            dvp.append(to_seq_major(v))
            q, k, v, ms = _diff_project(xs, pos_s, mem_k_s, mem_v_s, i, diff_w, j, t_step=ss, row_groups=1)
            os_ = _diff_attn_sample(q, cache_diff_k[j].transpose(0, 2, 1, 3),
                                    cache_diff_v[j].transpose(0, 2, 1, 3), k, v, lam_vecs,
                                    diff_subln_g[j], lam_init)
            dks.append(to_seq_major(k))
            dvs.append(to_seq_major(v))
        post = functools.partial(_post, layer=i, wo=wo, g1=ln1_g, b1=ln1_b, wg=wg, wu=wu, wd=wd,
                                 g2=ln2_g, b2=ln2_b, tm=512, ff_chunk=256, row_groups=2)
        xp = post(xp.reshape(bp * sp, D_MODEL), op.reshape(bp * sp, MIX_W),
                  mp.reshape(bp * sp, MEM_W)).reshape(bp, sp, D_MODEL)
        xs = post(xs.reshape(bs * ss, D_MODEL), os_.reshape(bs * ss, MIX_W),
                  ms.reshape(bs * ss, MEM_W)).reshape(bs, ss, D_MODEL)

    mem_out = lambda a: a.reshape(DEPTH, bp, MEM_HEADS, MEM_HD, N_MEM).transpose(0, 1, 4, 2, 3)
    return (xp, xs, jnp.stack(ret_p), jnp.stack(ret_s), jnp.stack(dkp), jnp.stack(dvp),
            jnp.stack(dks), jnp.stack(dvs), mem_out(mem_k_p), mem_out(mem_v_p))
```

```python
import functools
import math

import jax
import jax.numpy as jnp
import numpy as np
from jax import lax
from jax.experimental import pallas as pl
from jax.experimental.pallas import tpu as pltpu

D_MODEL = 1024
DEPTH = 2
PAST_LEN = 1024
CHUNK = 64
N_MEM = 256
MEM_HEADS = 4
MEM_HD = 64
MEM_W = MEM_HEADS * MEM_HD
MIX_W = D_MODEL - MEM_W
RET_HEADS = 6
RET_HD = MIX_W // RET_HEADS
RET_THETA = 10000.0
DIFF_HEADS = 6
DIFF_HD = MIX_W // (2 * DIFF_HEADS)
ROPE_THETA = 500000.0
ROT_DIM = DIFF_HD // 4
D_FF = -(-8 * D_MODEL // (3 * 256)) * 256
ALPHA = (2 * DEPTH) ** 0.25
LN_EPS = 1e-5
NEG_INF = -1e30

Q_SCALE = DIFF_HD ** -0.5 * math.log2(math.e)
HEAD_W = 128
VMEM_LIMIT = 56 * 1024 * 1024

F32 = jnp.float32
BF16 = jnp.bfloat16


def _lambda_init(layer_idx):
    return 0.8 - 0.6 * math.exp(-0.3 * layer_idx)


def _dot(a, b):
    return jnp.dot(a.astype(BF16), b.astype(BF16), preferred_element_type=F32)


def _dot_nt(a, b):
    return lax.dot_general(a.astype(BF16), b.astype(BF16), (((1,), (1,)), ((), ())),
                           preferred_element_type=F32)


def _dot_tn(a, b):
    return lax.dot_general(a.astype(BF16), b.astype(BF16), (((0,), (0,)), ((), ())),
                           preferred_element_type=F32)


def _fixed_spec(shape, index):
    return pl.BlockSpec(shape, lambda *_: index, pipeline_mode=pl.Buffered(1))


def _const_spec(shape):
    return _fixed_spec(shape, (0,) * len(shape))


def _layer_spec(shape, layer):
    return _fixed_spec((1,) + shape, (layer,) + (0,) * len(shape))


def _params(*semantics):
    return pltpu.CompilerParams(dimension_semantics=semantics, vmem_limit_bytes=VMEM_LIMIT)


def _layer_norm_rows(y, g, b):
    mu = jnp.mean(y, axis=-1, keepdims=True)
    d = y - mu
    var = jnp.mean(d * d, axis=-1, keepdims=True)
    return d * lax.rsqrt(var + LN_EPS) * g + b


def _softmax_rows(s):
    s = s - jnp.max(s, axis=-1, keepdims=True)
    p = jnp.exp(s)
    return p / jnp.sum(p, axis=-1, keepdims=True)


def _memory_attention(mq, mk_t, mv_t):
    row = lax.broadcasted_iota(jnp.int32, mk_t.shape, 0)
    mqb = (mq * MEM_HD ** -0.5).astype(BF16)
    out = None
    for h in range(MEM_HEADS):
        sel = (row >= h * MEM_HD) & (row < (h + 1) * MEM_HD)
        p = _softmax_rows(_dot(mqb, jnp.where(sel, mk_t, 0.0)))
        o = _dot_nt(p, jnp.where(sel, mv_t, 0.0))
        out = o if out is None else out + o
    return out


def _mem_kv_kernel(x_ref, w_ref, k_ref, v_ref):
    kv_t = _dot_nt(w_ref[0], x_ref[0])
    k_ref[0, 0] = kv_t[:MEM_W]
    v_ref[0, 0] = kv_t[MEM_W:]


def _mem_kv(mem, w_t_bf):
    b = mem.shape[0]
    out = jax.ShapeDtypeStruct((DEPTH, b, MEM_W, N_MEM), F32)
    return pl.pallas_call(
        _mem_kv_kernel,
        grid=(DEPTH, b),
        in_specs=[pl.BlockSpec((1, N_MEM, D_MODEL), lambda l, i: (i, 0, 0)),
                  pl.BlockSpec((1, 2 * MEM_W, D_MODEL), lambda l, i: (l, 0, 0))],
        out_specs=[pl.BlockSpec((1, 1, MEM_W, N_MEM), lambda l, i: (l, i, 0, 0))] * 2,
        out_shape=[out, out],
        name="mem_kv",
        compiler_params=_params("arbitrary", "arbitrary"),
    )(mem, w_t_bf)


def _ret_tables(tile):
    h = np.arange(RET_HEADS, dtype=np.float64)
    log_g = np.log(1.0 - np.exp2(-5.0 - h))
    idx = np.arange(tile, dtype=np.float64)
    dist = np.abs(idx[:, None] - idx[None, :])
    visible = (idx[None, :] // CHUNK) <= (idx[:, None] // CHUNK)
    decay = np.where(visible[None], np.exp(dist[None] * log_g[:, None, None]), 0.0)
    xi = np.exp((idx + 1.0)[None, :] * log_g[:, None])[:, :, None]
    zeta = np.exp((tile - 1.0 - idx)[None, :] * log_g[:, None])[:, :, None]
    g_tile = np.exp(tile * log_g)
    f = lambda a: jnp.asarray(a, F32)
    return f(decay), f(xi), f(zeta), [float(g) for g in g_tile]


def _ret_kernel(x_ref, w_ref, cos_ref, sin_ref, r0_ref, mk_ref, mv_ref, gn_ref, decay_ref, xi_ref,
                zeta_ref, o_ref, m_ref, r_ref, state_ref, *, tile, g_tile):
    step = pl.program_id(1)

    @pl.when(step == 0)
    def _():
        state_ref[...] = r0_ref[0]

    proj = _dot(x_ref[0], w_ref[0])
    t = proj.shape[0]
    cos = cos_ref[...]
    sin = sin_ref[...]

    def rope(a):
        return a * cos + pltpu.roll(a, RET_HD // 2, 1) * sin

    for h in range(RET_HEADS):
        col = h * HEAD_W
        q = rope(proj[:, col:col + HEAD_W])
        k = rope(proj[:, MIX_W + col:MIX_W + col + HEAD_W]) * RET_HD ** -0.5
        v = proj[:, 2 * MIX_W + col:2 * MIX_W + col + HEAD_W]
        gate = proj[:, 3 * MIX_W + col:3 * MIX_W + col + HEAD_W]
        outs = []
        r = state_ref[h]
        for c in range(t // tile):
            rows = slice(c * tile, (c + 1) * tile)
            qc, kc, vc = q[rows], k[rows], v[rows]
            inner = _dot_nt(qc, kc) * decay_ref[h]
            outs.append(_dot(inner, vc) + _dot(qc, r) * xi_ref[h])
            r = g_tile[h] * r + _dot_tn(kc * zeta_ref[h], vc)
        state_ref[h] = r
        o = outs[0] if len(outs) == 1 else jnp.concatenate(outs, axis=0)
        mu = jnp.mean(o, axis=-1, keepdims=True)
        d = o - mu
        var = jnp.mean(d * d, axis=-1, keepdims=True)
        o = d * lax.rsqrt(var + LN_EPS) * gn_ref[:, col:col + HEAD_W]
        o = o * (gate / (1.0 + jnp.exp(-gate)))
        o_ref[0, :, col:col + HEAD_W] = o.astype(o_ref.dtype)

    m = _memory_attention(proj[:, 4 * MIX_W:], mk_ref[0, 0], mv_ref[0, 0])
    m_ref[0] = m.astype(m_ref.dtype)

    @pl.when(step == pl.num_programs(1) - 1)
    def _():
        r_ref[0] = state_ref[...]


def _ret_mixer(x, pos, r0, mk_t, mv_t, layer, w_bf, w_layer, gn_g, *, t_step, tile):
    b, s, _ = x.shape
    half = RET_HD // 2
    lane_freq = jnp.arange(HEAD_W, dtype=jnp.int32) % half
    inv_freq = jnp.exp(-math.log(RET_THETA) * lane_freq.astype(F32) * 2.0 / RET_HD)
    ang = pos.astype(F32)[:, None] * inv_freq[None, :]
    cos = jnp.cos(ang)
    sin = jnp.where(jnp.arange(HEAD_W) < half, -jnp.sin(ang), jnp.sin(ang))
    decay, xi, zeta, g_tile = _ret_tables(tile)
    cols = w_bf.shape[-1]
    kern = functools.partial(_ret_kernel, tile=tile, g_tile=g_tile)
    mem_spec = pl.BlockSpec((1, 1, MEM_W, N_MEM), lambda i, j: (layer, i, 0, 0))
    return pl.pallas_call(
        kern,
        grid=(b, s // t_step),
        in_specs=[
            pl.BlockSpec((1, t_step, D_MODEL), lambda i, j: (i, j, 0)),
            _layer_spec((D_MODEL, cols), w_layer),
            pl.BlockSpec((t_step, HEAD_W), lambda i, j: (j, 0)),
            pl.BlockSpec((t_step, HEAD_W), lambda i, j: (j, 0)),
            pl.BlockSpec((1, RET_HEADS, RET_HD, RET_HD), lambda i, j: (i, 0, 0, 0)),
            mem_spec, mem_spec,
            _const_spec((1, MIX_W)),
            _const_spec((RET_HEADS, tile, tile)),
            _const_spec((RET_HEADS, tile, 1)),
            _const_spec((RET_HEADS, tile, 1)),
        ],
        out_specs=[
            pl.BlockSpec((1, t_step, MIX_W), lambda i, j: (i, j, 0)),
            pl.BlockSpec((1, t_step, MEM_W), lambda i, j: (i, j, 0)),
            pl.BlockSpec((1, RET_HEADS, RET_HD, RET_HD), lambda i, j: (i, 0, 0, 0)),
        ],
        out_shape=[
            jax.ShapeDtypeStruct((b, s, MIX_W), BF16),
            jax.ShapeDtypeStruct((b, s, MEM_W), BF16),
            jax.ShapeDtypeStruct((b, RET_HEADS, RET_HD, RET_HD), F32),
        ],
        scratch_shapes=[pltpu.VMEM((RET_HEADS, RET_HD, RET_HD), F32)],
        name="ret_mixer",
        compiler_params=_params("arbitrary", "arbitrary"),
    )(x, w_bf, cos, sin, r0, mk_t, mv_t, gn_g.reshape(1, MIX_W), decay, xi, zeta)


def _diff_proj_kernel(x_ref, w_ref, c_ref, sa_ref, sb_ref, mk_ref, mv_ref, q_ref, k_ref, v_ref, m_ref,
                      *, row_groups):
    t = x_ref.shape[1]
    for r in range(row_groups):
        rows = slice(r * t // row_groups, (r + 1) * t // row_groups)
        proj = _dot(x_ref[0, rows, :], w_ref[0])
        c = c_ref[rows, :]
        sa = sa_ref[rows, :]
        sb = sb_ref[rows, :]

        def rope(a):
            return (a * c + pltpu.roll(a, ROT_DIM // 2, 1) * sa
                    + pltpu.roll(a, HEAD_W - ROT_DIM // 2, 1) * sb)

        for h in range(DIFF_HEADS):
            col = h * HEAD_W
            q = rope(proj[:, col:col + HEAD_W]) * Q_SCALE
            q_ref[0, h, rows, :] = q.astype(q_ref.dtype)
            k_ref[0, h, rows, :] = rope(proj[:, MIX_W + col:MIX_W + col + HEAD_W])
            v_ref[0, h, rows, :] = proj[:, 2 * MIX_W + col:2 * MIX_W + col + HEAD_W]
        m = _memory_attention(proj[:, 3 * MIX_W:], mk_ref[0, 0], mv_ref[0, 0])
        m_ref[0, rows, :] = m.astype(m_ref.dtype)


def _diff_project(x, pos, mk_t, mv_t, layer, w_bf, w_layer, *, t_step, row_groups):
    b, s, _ = x.shape
    half = ROT_DIM // 2
    lane = jnp.arange(HEAD_W, dtype=jnp.int32) % DIFF_HD
    inv_freq = jnp.exp(-math.log(ROPE_THETA) * (lane % half).astype(F32) * 2.0 / ROT_DIM)
    ang = pos.astype(F32)[:, None] * inv_freq[None, :]
    cos, sin = jnp.cos(ang), jnp.sin(ang)
    tables = [jnp.where(lane < ROT_DIM, cos, 1.0),
              jnp.where((lane >= half) & (lane < ROT_DIM), sin, 0.0),
              jnp.where(lane < half, -sin, 0.0)]
    cols = w_bf.shape[-1]
    tab_spec = pl.BlockSpec((t_step, HEAD_W), lambda i, j: (j, 0))
    tok_spec = lambda w: pl.BlockSpec((1, t_step, w), lambda i, j: (i, j, 0))
    head_spec = pl.BlockSpec((1, DIFF_HEADS, t_step, HEAD_W), lambda i, j: (i, 0, j, 0))
    mem_spec = pl.BlockSpec((1, 1, MEM_W, N_MEM), lambda i, j: (layer, i, 0, 0))
    heads = lambda dt: jax.ShapeDtypeStruct((b, DIFF_HEADS, s, HEAD_W), dt)
    return pl.pallas_call(
        functools.partial(_diff_proj_kernel, row_groups=row_groups),
        grid=(b, s // t_step),
        in_specs=[tok_spec(D_MODEL), _layer_spec((D_MODEL, cols), w_layer), tab_spec, tab_spec,
                  tab_spec, mem_spec, mem_spec],
        out_specs=[head_spec, head_spec, head_spec, tok_spec(MEM_W)],
        out_shape=[
            heads(BF16),
            heads(F32),
            heads(F32),
            jax.ShapeDtypeStruct((b, s, MEM_W), BF16),
        ],
        name="diff_proj",
        compiler_params=_params("arbitrary", "arbitrary"),
    )(x, w_bf, *tables, mk_t, mv_t)


def _diff_lambda(lq1_ref, lk1_ref, lq2_ref, lk2_ref, lam_init):
    a = jnp.sum(lq1_ref[...] * lk1_ref[...], axis=-1, keepdims=True)
    b = jnp.sum(lq2_ref[...] * lk2_ref[...], axis=-1, keepdims=True)
    return jnp.exp(a) - jnp.exp(b) + lam_init


def _diff_finish(o, g_ref, lam_init):
    ms = jnp.mean(o * o, axis=-1, keepdims=True)
    return o * lax.rsqrt(ms + LN_EPS) * g_ref[...] * (1.0 - lam_init)


def _split_heads(k):
    lane = lax.broadcasted_iota(jnp.int32, k.shape, 1)
    first = lane < DIFF_HD
    return jnp.where(first, k, 0.0).astype(BF16), jnp.where(first, 0.0, k).astype(BF16)


def _causal_tiles(n_q, q_per_k):
    return [(j, (1 + qi % q_per_k) * int(j == qi // q_per_k), int(j == 0), int(j == qi // q_per_k), qi)
            for qi in range(n_q) for j in range(qi // q_per_k + 1)]


def _diff_attn_prompt_kernel(tab_ref, q_ref, k_ref, v_ref, lq1_ref, lk1_ref, lq2_ref, lk2_ref, g_ref,
                             o_ref, kx_ref, vt_ref, qt_ref, mask_ref, s_ref, p_ref, acc_ref, *,
                             tq, tk, heads, lam_init):
    seq = q_ref.shape[2]
    n_q, n_k = seq // tq, seq // tk
    q_per_k = tk // tq
    tiles = _causal_tiles(n_q, q_per_k)
    n_tiles = len(tiles)
    chunks_q, chunks_k = tq // CHUNK, tk // CHUNK
    streams = [(hd, mp) for hd in range(heads) for mp in range(2)]

    lane = lax.broadcasted_iota(jnp.int32, (seq, HEAD_W), 1)
    row_chunk = (lax.broadcasted_iota(jnp.int32, (seq, HEAD_W), 0) // CHUNK) % chunks_k
    chunk_one_hot = jnp.where(lane == row_chunk, 1.0, 0.0).astype(BF16)
    first = lane < DIFF_HD
    for hd in range(heads):
        k = k_ref[0, hd]
        kx_ref[hd, 0, :, :HEAD_W] = jnp.where(first, k, 0.0).astype(BF16)
        kx_ref[hd, 1, :, :HEAD_W] = jnp.where(first, 0.0, k).astype(BF16)
        kx_ref[hd, 0, :, HEAD_W:] = chunk_one_hot
        kx_ref[hd, 1, :, HEAD_W:] = chunk_one_hot
        for t in range(n_k):
            vt_ref[hd, t] = v_ref[0, hd, t * tk:(t + 1) * tk, :].T.astype(BF16)
        for qi in range(n_q):
            qt_ref[hd, qi] = q_ref[0, hd, qi * tq:(qi + 1) * tq, :].astype(F32).T.astype(BF16)
    bias_row = lax.broadcasted_iota(jnp.int32, (HEAD_W, tq), 0)
    bias_col = lax.broadcasted_iota(jnp.int32, (HEAD_W, tq), 1) // CHUNK
    mask_ref[0] = jnp.zeros((HEAD_W, tq), BF16)
    for r in range(q_per_k):
        hidden = (bias_row < chunks_k) & (bias_row > bias_col + chunks_q * r)
        mask_ref[1 + r] = jnp.where(hidden, NEG_INF, 0.0).astype(BF16)
    lam = _diff_lambda(lq1_ref, lk1_ref, lq2_ref, lk2_ref, lam_init)

    def tile_of(i):
        if isinstance(i, int):
            j, mask, is_first, is_last, qi = tiles[i]
            return j, mask, bool(is_first), bool(is_last), qi
        j, mask, is_first, is_last, qi = (tab_ref[i, f] for f in range(5))
        return j, mask, is_first == 1, is_last == 1, qi

    def stage_a(i, slot):
        j, mask, _, _, qi = tile_of(i)
        base = j * tk if isinstance(j, int) else pl.multiple_of(j * tk, tk)
        tops = []
        for hd, mp in streams:
            q_op = jnp.concatenate([qt_ref[hd, qi], mask_ref[mask]], axis=0)
            s = _dot(kx_ref[hd, mp, pl.ds(base, tk), :], q_op)
            s_ref[hd, slot, mp] = s
            tops.append(jnp.max(s, axis=0, keepdims=True))
        return tuple(tops)

    def stage_b(i, slot, m, l, top):
        _, _, is_first, _, _ = tile_of(i)
        out = []
        for n, (hd, mp) in enumerate(streams):
            m_prev = jnp.where(is_first, NEG_INF, m[n])
            m_new = jnp.maximum(m_prev, top[n])
            scale = jnp.exp2(m_prev - m_new)
            p = jnp.exp2(s_ref[hd, slot, mp] - m_new)
            p_ref[hd, slot, mp] = p.astype(BF16)
            out.append((m_new, scale * l[n] + jnp.sum(p, axis=0, keepdims=True), scale))
        return tuple(zip(*out))

    def stage_c(i, slot, scale):
        j = tile_of(i)[0]
        for n, (hd, mp) in enumerate(streams):
            acc_ref[hd, mp] = scale[n] * acc_ref[hd, mp] + _dot(vt_ref[hd, j], p_ref[hd, slot, mp])

    def finish_if_last(i, l):
        _, _, _, is_last, qi = tile_of(i)

        def finish():
            rows = pl.ds(qi * tq if isinstance(qi, int) else pl.multiple_of(qi * tq, tq), tq)
            for hd in range(heads):
                o = (acc_ref[hd, 0] / l[2 * hd] - lam * (acc_ref[hd, 1] / l[2 * hd + 1])).T
                o_ref[0, rows, hd * HEAD_W:(hd + 1) * HEAD_W] = (
                    _diff_finish(o, g_ref, lam_init).astype(o_ref.dtype))

        if isinstance(is_last, bool):
            if is_last:
                finish()
        else:
            pl.when(is_last)(finish)

    def iteration(i, slot, carry, a=True, b=True, c=True):
        m, l, top, scale, l_done = carry
        new_scale, new_l_done = scale, l_done
        if b:
            m, l, new_scale = stage_b(i - 1, 1 - slot, m, l, top)
            new_l_done = l
        if a:
            top = stage_a(i, slot)
        if c:
            stage_c(i - 2, slot, scale)
            finish_if_last(i - 2, l_done)
        return m, l, top, new_scale, new_l_done

    def pair(n, carry):
        i = 2 + 2 * n
        return iteration(i + 1, 1, iteration(i, 0, carry))

    zero = (jnp.zeros((1, tq), F32),) * len(streams)
    acc_ref[...] = jnp.zeros(acc_ref.shape, F32)
    carry = (zero, zero, zero, zero, zero)
    carry = iteration(0, 0, carry, b=False, c=False)
    carry = iteration(1, 1, carry, c=False)
    carry = lax.fori_loop(0, (n_tiles - 2) // 2, pair, carry)
    carry = iteration(n_tiles, 0, carry, a=False)
    iteration(n_tiles + 1, 1, carry, a=False, b=False)


def _diff_attn_prompt(q, k, v, lam_vecs, subln_g, lam_init, *, tq, tk, heads):
    b, _, s, _ = q.shape
    rows = _causal_tiles(s // tq, tk // tq)
    assert len(rows) % 2 == 0
    table = jnp.asarray(np.array(rows, np.int32))
    kern = functools.partial(_diff_attn_prompt_kernel, tq=tq, tk=tk, heads=heads, lam_init=lam_init)
    const = lambda shape: pl.BlockSpec(shape, lambda i, h, tab: (0,) * len(shape))
    head_spec = pl.BlockSpec((1, heads, s, HEAD_W), lambda i, h, tab: (i, h, 0, 0))
    return pl.pallas_call(
        kern,
        grid_spec=pltpu.PrefetchScalarGridSpec(
            num_scalar_prefetch=1,
            grid=(b, DIFF_HEADS // heads),
            in_specs=[head_spec, head_spec, head_spec] + [const((1, DIFF_HD))] * 4 + [const((1, HEAD_W))],
            out_specs=pl.BlockSpec((1, s, heads * HEAD_W), lambda i, h, tab: (i, 0, h)),
            scratch_shapes=[
                pltpu.VMEM((heads, 2, s, 2 * HEAD_W), BF16),
                pltpu.VMEM((heads, s // tk, HEAD_W, tk), BF16),
                pltpu.VMEM((heads, s // tq, HEAD_W, tq), BF16),
                pltpu.VMEM((1 + tk // tq, HEAD_W, tq), BF16),
                pltpu.VMEM((heads, 2, 2, tk, tq), F32),
                pltpu.VMEM((heads, 2, 2, tk, tq), BF16),
                pltpu.VMEM((heads, 2, HEAD_W, tq), F32),
            ]),
        out_shape=jax.ShapeDtypeStruct((b, s, MIX_W), BF16),
        name="diff_attn_prompt",
        compiler_params=_params("arbitrary", "arbitrary"),
    )(table, q, k, v, *lam_vecs, subln_g.reshape(1, HEAD_W))


def _diff_attn_sample_kernel(q_ref, kc_ref, vc_ref, kn_ref, vn_ref, lq1_ref, lk1_ref, lq2_ref, lk2_ref,
                             g_ref, o_ref, *, lam_init):
    lam = _diff_lambda(lq1_ref, lk1_ref, lq2_ref, lk2_ref, lam_init)
    for h in range(q_ref.shape[1]):
        q = q_ref[0, h]
        s = q.shape[0]
        q1, q2 = _split_heads(q)
        qq = jnp.concatenate([q1, q2], axis=0)
        sc = _dot_nt(qq, kc_ref[0, h])
        sn = _dot_nt(qq, kn_ref[0, h])
        m = jnp.maximum(jnp.max(sc, axis=-1, keepdims=True), jnp.max(sn, axis=-1, keepdims=True))
        pc = jnp.exp2(sc - m)
        pn = jnp.exp2(sn - m)
        l = jnp.sum(pc, axis=-1, keepdims=True) + jnp.sum(pn, axis=-1, keepdims=True)
        o2 = (_dot(pc, vc_ref[0, h]) + _dot(pn, vn_ref[0, h])) / l
        o = o2[:s] - lam * o2[s:]
        o_ref[0, :, h * HEAD_W:(h + 1) * HEAD_W] = _diff_finish(o, g_ref, lam_init).astype(o_ref.dtype)


def _diff_attn_sample(q, k_cache, v_cache, k_new, v_new, lam_vecs, subln_g, lam_init):
    b, heads, s, _ = q.shape
    past = k_cache.shape[2]
    kern = functools.partial(_diff_attn_sample_kernel, lam_init=lam_init)
    vec_spec = _const_spec((1, DIFF_HD))
    new_spec = pl.BlockSpec((1, heads, s, HEAD_W), lambda i: (i, 0, 0, 0))
    old_spec = pl.BlockSpec((1, heads, past, HEAD_W), lambda i: (i, 0, 0, 0))
    return pl.pallas_call(
        kern,
        grid=(b,),
        in_specs=[new_spec, old_spec, old_spec, new_spec, new_spec, vec_spec, vec_spec, vec_spec,
                  vec_spec, _const_spec((1, HEAD_W))],
        out_specs=pl.BlockSpec((1, s, heads * HEAD_W), lambda i: (i, 0, 0)),
        out_shape=jax.ShapeDtypeStruct((b, s, MIX_W), BF16),
        name="diff_attn_sample",
        compiler_params=_params("arbitrary"),
    )(q, k_cache, v_cache, k_new, v_new, *lam_vecs, subln_g.reshape(1, HEAD_W))


def _post_kernel(x_ref, o_ref, m_ref, wo_ref, g1_ref, b1_ref, wg_ref, wu_ref, wd_ref, g2_ref, b2_ref,
                 y_ref, h_ref, *, ff_chunk, row_groups):
    tm = x_ref.shape[0]
    groups = [slice(r * tm // row_groups, (r + 1) * tm // row_groups) for r in range(row_groups)]
    x1, x1b = [], []
    for rows in groups:
        mixed = (_dot(o_ref[rows, :], wo_ref[0, :MIX_W, :])
                 + _dot(m_ref[rows, :], wo_ref[0, MIX_W:, :]))
        x1.append(_layer_norm_rows(ALPHA * x_ref[rows, :] + mixed, g1_ref[0], b1_ref[0]))
        x1b.append(x1[-1].astype(BF16))
    for c in range(D_FF // ff_chunk):
        cols = slice(c * ff_chunk, (c + 1) * ff_chunk)
        for r, rows in enumerate(groups):
            gate = _dot(x1b[r], wg_ref[0, :, cols])
            up = _dot(x1b[r], wu_ref[0, :, cols])
            h_ref[rows, cols] = (gate / (1.0 + jnp.exp(-gate)) * up).astype(BF16)
    for r, rows in enumerate(groups):
        ff = _dot(h_ref[rows, :], wd_ref[0])
        y_ref[rows, :] = _layer_norm_rows(ALPHA * x1[r] + ff, g2_ref[0], b2_ref[0])


def _post(x, o, m, layer, wo, g1, b1, wg, wu, wd, g2, b2, *, tm, ff_chunk, row_groups):
    n = x.shape[0]
    row = lambda w: pl.BlockSpec((tm, w), lambda i: (i, 0))
    vec = _layer_spec((1, D_MODEL), layer)
    kern = functools.partial(_post_kernel, ff_chunk=ff_chunk, row_groups=row_groups)
    r1 = lambda a: a.reshape(DEPTH, 1, D_MODEL)
    return pl.pallas_call(
        kern,
        grid=(n // tm,),
        in_specs=[row(D_MODEL), row(MIX_W), row(MEM_W), _layer_spec((D_MODEL, D_MODEL), layer), vec, vec,
                  _layer_spec((D_MODEL, D_FF), layer), _layer_spec((D_MODEL, D_FF), layer),
                  _layer_spec((D_FF, D_MODEL), layer), vec, vec],
        out_specs=row(D_MODEL),
        out_shape=jax.ShapeDtypeStruct((n, D_MODEL), F32),
        scratch_shapes=[pltpu.VMEM((tm, D_FF), BF16)],
        name="post_mixer",
        compiler_params=_params("arbitrary"),
    )(x, o, m, wo, r1(g1), r1(b1), wg, wu, wd, r1(g2), r1(b2))


def kernel(x_prompt, x_sample, mem_prompt, cache_ret_state, cache_diff_k, cache_diff_v, cache_mem_k,
           cache_mem_v, ret_w_in, ret_gn_g, diff_w_in, diff_lambda_q1, diff_lambda_k1, diff_lambda_q2,
           diff_lambda_k2, diff_subln_g, w_mem_kv, w_o, ln1_g, ln1_b, w_gate, w_up, w_down, ln2_g, ln2_b):
    bp, sp, _ = x_prompt.shape
    bs, ss, _ = x_sample.shape
    pos_p = jnp.arange(sp)
    pos_s = PAST_LEN + jnp.arange(ss)
    bf = lambda a: a.astype(BF16)

    mem_k_p, mem_v_p = _mem_kv(mem_prompt, bf(w_mem_kv.transpose(0, 2, 1)))
    mem_t = lambda a: a.transpose(0, 1, 3, 4, 2).reshape(DEPTH, bs, MEM_W, N_MEM)
    mem_k_s, mem_v_s = mem_t(cache_mem_k), mem_t(cache_mem_v)

    ret_w, diff_w = bf(ret_w_in), bf(diff_w_in)
    wo, wg, wu, wd = bf(w_o), bf(w_gate), bf(w_up), bf(w_down)

    xp, xs = x_prompt, x_sample
    ret_p, ret_s, dkp, dvp, dks, dvs = [], [], [], [], [], []
    to_seq_major = lambda a: a.transpose(0, 2, 1, 3)
    for i in range(DEPTH):
        j = i // 2
        if i % 2 == 0:
            r0 = jnp.zeros((bp, RET_HEADS, RET_HD, RET_HD), F32)
            op, mp, rp = _ret_mixer(xp, pos_p, r0, mem_k_p, mem_v_p, i, ret_w, j, ret_gn_g[j],
                                    t_step=512, tile=256)
            os_, ms, rs = _ret_mixer(xs, pos_s, cache_ret_state[j], mem_k_s, mem_v_s, i, ret_w, j,
                                     ret_gn_g[j], t_step=ss, tile=ss)
            ret_p.append(rp)
            ret_s.append(rs)
        else:
            lam_init = _lambda_init(i)
            lam_vecs = [a[j].reshape(1, DIFF_HD) for a in
                        (diff_lambda_q1, diff_lambda_k1, diff_lambda_q2, diff_lambda_k2)]
            q, k, v, mp = _diff_project(xp, pos_p, mem_k_p, mem_v_p, i, diff_w, j, t_step=512, row_groups=1)
            op = _diff_attn_prompt(q, k, v, lam_vecs, diff_subln_g[j], lam_init, tq=256, tk=512, heads=2)
            dkp.append(to_seq_major(k))
            dvp.append(to_seq_major(v))
            q, k, v, ms = _diff_project(xs, pos_s, mem_k_s, mem_v_s, i, diff_w, j, t_step=ss, row_groups=1)
            os_ = _diff_attn_sample(q, cache_diff_k[j].transpose(0, 2, 1, 3),
                                    cache_diff_v[j].transpose(0, 2, 1, 3), k, v, lam_vecs,
                                    diff_subln_g[j], lam_init)
            dks.append(to_seq_major(k))
            dvs.append(to_seq_major(v))
        post = functools.partial(_post, layer=i, wo=wo, g1=ln1_g, b1=ln1_b, wg=wg, wu=wu, wd=wd,
                                 g2=ln2_g, b2=ln2_b, tm=512, ff_chunk=256, row_groups=2)
        xp = post(xp.reshape(bp * sp, D_MODEL), op.reshape(bp * sp, MIX_W),
                  mp.reshape(bp * sp, MEM_W)).reshape(bp, sp, D_MODEL)
        xs = post(xs.reshape(bs * ss, D_MODEL), os_.reshape(bs * ss, MIX_W),
                  ms.reshape(bs * ss, MEM_W)).reshape(bs, ss, D_MODEL)

    mem_out = lambda a: a.reshape(DEPTH, bp, MEM_HEADS, MEM_HD, N_MEM).transpose(0, 1, 4, 2, 3)
    return (xp, xs, jnp.stack(ret_p), jnp.stack(ret_s), jnp.stack(dkp), jnp.stack(dvp),
            jnp.stack(dks), jnp.stack(dvs), mem_out(mem_k_p), mem_out(mem_v_p))
```

```python
import functools
import math

import jax
import jax.numpy as jnp
import numpy as np
from jax import lax
from jax.experimental import pallas as pl
from jax.experimental.pallas import tpu as pltpu

D_MODEL = 1024
DEPTH = 2
PAST_LEN = 1024
CHUNK = 64
N_MEM = 256
MEM_HEADS = 4
MEM_HD = 64
MEM_W = MEM_HEADS * MEM_HD
MIX_W = D_MODEL - MEM_W
RET_HEADS = 6
RET_HD = MIX_W // RET_HEADS
RET_THETA = 10000.0
DIFF_HEADS = 6
DIFF_HD = MIX_W // (2 * DIFF_HEADS)
ROPE_THETA = 500000.0
ROT_DIM = DIFF_HD // 4
D_FF = -(-8 * D_MODEL // (3 * 256)) * 256
ALPHA = (2 * DEPTH) ** 0.25
LN_EPS = 1e-5
NEG_INF = -1e30

Q_SCALE = DIFF_HD ** -0.5 * math.log2(math.e)
SUM_ROWS = 16
HEAD_W = 128
VMEM_LIMIT = 56 * 1024 * 1024

F32 = jnp.float32
BF16 = jnp.bfloat16


def _lambda_init(layer_idx):
    return 0.8 - 0.6 * math.exp(-0.3 * layer_idx)


def _dot(a, b):
    return jnp.dot(a.astype(BF16), b.astype(BF16), preferred_element_type=F32)


def _dot_nt(a, b):
    return lax.dot_general(a.astype(BF16), b.astype(BF16), (((1,), (1,)), ((), ())),
                           preferred_element_type=F32)


def _dot_tn(a, b):
    return lax.dot_general(a.astype(BF16), b.astype(BF16), (((0,), (0,)), ((), ())),
                           preferred_element_type=F32)


def _fixed_spec(shape, index):
    return pl.BlockSpec(shape, lambda *_: index, pipeline_mode=pl.Buffered(1))


def _const_spec(shape):
    return _fixed_spec(shape, (0,) * len(shape))


def _layer_spec(shape, layer):
    return _fixed_spec((1,) + shape, (layer,) + (0,) * len(shape))


def _params(*semantics):
    return pltpu.CompilerParams(dimension_semantics=semantics, vmem_limit_bytes=VMEM_LIMIT)


def _layer_norm_rows(y, g, b):
    mu = jnp.mean(y, axis=-1, keepdims=True)
    d = y - mu
    var = jnp.mean(d * d, axis=-1, keepdims=True)
    return d * lax.rsqrt(var + LN_EPS) * g + b


def _softmax_rows(s):
    s = s - jnp.max(s, axis=-1, keepdims=True)
    p = jnp.exp(s)
    return p / jnp.sum(p, axis=-1, keepdims=True)


def _memory_attention(mq, mk_t, mv_t):
    row = lax.broadcasted_iota(jnp.int32, mk_t.shape, 0)
    mqb = (mq * MEM_HD ** -0.5).astype(BF16)
    out = None
    for h in range(MEM_HEADS):
        sel = (row >= h * MEM_HD) & (row < (h + 1) * MEM_HD)
        p = _softmax_rows(_dot(mqb, jnp.where(sel, mk_t, 0.0)))
        o = _dot_nt(p, jnp.where(sel, mv_t, 0.0))
        out = o if out is None else out + o
    return out


def _mem_kv_kernel(x_ref, w_ref, k_ref, v_ref):
    kv_t = _dot_nt(w_ref[0], x_ref[0])
    k_ref[0, 0] = kv_t[:MEM_W]
    v_ref[0, 0] = kv_t[MEM_W:]


def _mem_kv(mem, w_t_bf):
    b = mem.shape[0]
    out = jax.ShapeDtypeStruct((DEPTH, b, MEM_W, N_MEM), F32)
    return pl.pallas_call(
        _mem_kv_kernel,
        grid=(DEPTH, b),
        in_specs=[pl.BlockSpec((1, N_MEM, D_MODEL), lambda l, i: (i, 0, 0)),
                  pl.BlockSpec((1, 2 * MEM_W, D_MODEL), lambda l, i: (l, 0, 0))],
        out_specs=[pl.BlockSpec((1, 1, MEM_W, N_MEM), lambda l, i: (l, i, 0, 0))] * 2,
        out_shape=[out, out],
        name="mem_kv",
        compiler_params=_params("arbitrary", "arbitrary"),
    )(mem, w_t_bf)


def _ret_tables(tile):
    h = np.arange(RET_HEADS, dtype=np.float64)
    log_g = np.log(1.0 - np.exp2(-5.0 - h))
    idx = np.arange(tile, dtype=np.float64)
    dist = np.abs(idx[:, None] - idx[None, :])
    visible = (idx[None, :] // CHUNK) <= (idx[:, None] // CHUNK)
    decay = np.where(visible[None], np.exp(dist[None] * log_g[:, None, None]), 0.0)
    xi = np.exp((idx + 1.0)[None, :] * log_g[:, None])[:, :, None]
    zeta = np.exp((tile - 1.0 - idx)[None, :] * log_g[:, None])[:, :, None]
    g_tile = np.exp(tile * log_g)
    f = lambda a: jnp.asarray(a, F32)
    return f(decay), f(xi), f(zeta), [float(g) for g in g_tile]


def _ret_kernel(x_ref, w_ref, cos_ref, sin_ref, r0_ref, mk_ref, mv_ref, gn_ref, decay_ref, xi_ref,
                zeta_ref, o_ref, m_ref, r_ref, state_ref, *, tile, g_tile):
    step = pl.program_id(1)

    @pl.when(step == 0)
    def _():
        state_ref[...] = r0_ref[0]

    proj = _dot(x_ref[0], w_ref[0])
    t = proj.shape[0]
    cos = cos_ref[...]
    sin = sin_ref[...]

    def rope(a):
        return a * cos + pltpu.roll(a, RET_HD // 2, 1) * sin

    for h in range(RET_HEADS):
        col = h * HEAD_W
        q = rope(proj[:, col:col + HEAD_W])
        k = rope(proj[:, MIX_W + col:MIX_W + col + HEAD_W]) * RET_HD ** -0.5
        v = proj[:, 2 * MIX_W + col:2 * MIX_W + col + HEAD_W]
        gate = proj[:, 3 * MIX_W + col:3 * MIX_W + col + HEAD_W]
        outs = []
        r = state_ref[h]
        for c in range(t // tile):
            rows = slice(c * tile, (c + 1) * tile)
            qc, kc, vc = q[rows], k[rows], v[rows]
            inner = _dot_nt(qc, kc) * decay_ref[h]
            outs.append(_dot(inner, vc) + _dot(qc, r) * xi_ref[h])
            r = g_tile[h] * r + _dot_tn(kc * zeta_ref[h], vc)
        state_ref[h] = r
        o = outs[0] if len(outs) == 1 else jnp.concatenate(outs, axis=0)
        mu = jnp.mean(o, axis=-1, keepdims=True)
        d = o - mu
        var = jnp.mean(d * d, axis=-1, keepdims=True)
        o = d * lax.rsqrt(var + LN_EPS) * gn_ref[:, col:col + HEAD_W]
        o = o * (gate / (1.0 + jnp.exp(-gate)))
        o_ref[0, :, col:col + HEAD_W] = o.astype(o_ref.dtype)

    m = _memory_attention(proj[:, 4 * MIX_W:], mk_ref[0, 0], mv_ref[0, 0])
    m_ref[0] = m.astype(m_ref.dtype)

    @pl.when(step == pl.num_programs(1) - 1)
    def _():
        r_ref[0] = state_ref[...]


def _ret_mixer(x, pos, r0, mk_t, mv_t, layer, w_bf, w_layer, gn_g, *, t_step, tile):
    b, s, _ = x.shape
    half = RET_HD // 2
    lane_freq = jnp.arange(HEAD_W, dtype=jnp.int32) % half
    inv_freq = jnp.exp(-math.log(RET_THETA) * lane_freq.astype(F32) * 2.0 / RET_HD)
    ang = pos.astype(F32)[:, None] * inv_freq[None, :]
    cos = jnp.cos(ang)
    sin = jnp.where(jnp.arange(HEAD_W) < half, -jnp.sin(ang), jnp.sin(ang))
    decay, xi, zeta, g_tile = _ret_tables(tile)
    cols = w_bf.shape[-1]
    kern = functools.partial(_ret_kernel, tile=tile, g_tile=g_tile)
    mem_spec = pl.BlockSpec((1, 1, MEM_W, N_MEM), lambda i, j: (layer, i, 0, 0))
    return pl.pallas_call(
        kern,
        grid=(b, s // t_step),
        in_specs=[
            pl.BlockSpec((1, t_step, D_MODEL), lambda i, j: (i, j, 0)),
            _layer_spec((D_MODEL, cols), w_layer),
            pl.BlockSpec((t_step, HEAD_W), lambda i, j: (j, 0)),
            pl.BlockSpec((t_step, HEAD_W), lambda i, j: (j, 0)),
            pl.BlockSpec((1, RET_HEADS, RET_HD, RET_HD), lambda i, j: (i, 0, 0, 0)),
            mem_spec, mem_spec,
            _const_spec((1, MIX_W)),
            _const_spec((RET_HEADS, tile, tile)),
            _const_spec((RET_HEADS, tile, 1)),
            _const_spec((RET_HEADS, tile, 1)),
        ],
        out_specs=[
            pl.BlockSpec((1, t_step, MIX_W), lambda i, j: (i, j, 0)),
            pl.BlockSpec((1, t_step, MEM_W), lambda i, j: (i, j, 0)),
            pl.BlockSpec((1, RET_HEADS, RET_HD, RET_HD), lambda i, j: (i, 0, 0, 0)),
        ],
        out_shape=[
            jax.ShapeDtypeStruct((b, s, MIX_W), BF16),
            jax.ShapeDtypeStruct((b, s, MEM_W), BF16),
            jax.ShapeDtypeStruct((b, RET_HEADS, RET_HD, RET_HD), F32),
        ],
        scratch_shapes=[pltpu.VMEM((RET_HEADS, RET_HD, RET_HD), F32)],
        name="ret_mixer",
        compiler_params=_params("arbitrary", "arbitrary"),
    )(x, w_bf, cos, sin, r0, mk_t, mv_t, gn_g.reshape(1, MIX_W), decay, xi, zeta)


def _diff_proj_kernel(x_ref, w_ref, c_ref, sa_ref, sb_ref, mk_ref, mv_ref, q_ref, k_ref, v_ref, m_ref):
    proj = _dot(x_ref[0], w_ref[0])
    c = c_ref[...]
    sa = sa_ref[...]
    sb = sb_ref[...]

    def rope(a):
        return (a * c + pltpu.roll(a, ROT_DIM // 2, 1) * sa
                + pltpu.roll(a, HEAD_W - ROT_DIM // 2, 1) * sb)

    for h in range(DIFF_HEADS):
        col = h * HEAD_W
        q = rope(proj[:, col:col + HEAD_W]) * Q_SCALE
        q_ref[0, h] = q.astype(q_ref.dtype)
        k_ref[0, h] = rope(proj[:, MIX_W + col:MIX_W + col + HEAD_W])
        v_ref[0, h] = proj[:, 2 * MIX_W + col:2 * MIX_W + col + HEAD_W]
    m = _memory_attention(proj[:, 3 * MIX_W:], mk_ref[0, 0], mv_ref[0, 0])
    m_ref[0] = m.astype(m_ref.dtype)


def _diff_project(x, pos, mk_t, mv_t, layer, w_bf, w_layer, *, t_step):
    b, s, _ = x.shape
    half = ROT_DIM // 2
    lane = jnp.arange(HEAD_W, dtype=jnp.int32) % DIFF_HD
    inv_freq = jnp.exp(-math.log(ROPE_THETA) * (lane % half).astype(F32) * 2.0 / ROT_DIM)
    ang = pos.astype(F32)[:, None] * inv_freq[None, :]
    cos, sin = jnp.cos(ang), jnp.sin(ang)
    tables = [jnp.where(lane < ROT_DIM, cos, 1.0),
              jnp.where((lane >= half) & (lane < ROT_DIM), sin, 0.0),
              jnp.where(lane < half, -sin, 0.0)]
    cols = w_bf.shape[-1]
    tab_spec = pl.BlockSpec((t_step, HEAD_W), lambda i, j: (j, 0))
    tok_spec = lambda w: pl.BlockSpec((1, t_step, w), lambda i, j: (i, j, 0))
    head_spec = pl.BlockSpec((1, DIFF_HEADS, t_step, HEAD_W), lambda i, j: (i, 0, j, 0))
    mem_spec = pl.BlockSpec((1, 1, MEM_W, N_MEM), lambda i, j: (layer, i, 0, 0))
    heads = lambda dt: jax.ShapeDtypeStruct((b, DIFF_HEADS, s, HEAD_W), dt)
    return pl.pallas_call(
        _diff_proj_kernel,
        grid=(b, s // t_step),
        in_specs=[tok_spec(D_MODEL), _layer_spec((D_MODEL, cols), w_layer), tab_spec, tab_spec,
                  tab_spec, mem_spec, mem_spec],
        out_specs=[head_spec, head_spec, head_spec, tok_spec(MEM_W)],
        out_shape=[
            heads(BF16),
            heads(F32),
            heads(F32),
            jax.ShapeDtypeStruct((b, s, MEM_W), BF16),
        ],
        name="diff_proj",
        compiler_params=_params("arbitrary", "arbitrary"),
    )(x, w_bf, *tables, mk_t, mv_t)


def _diff_lambda(lq1_ref, lk1_ref, lq2_ref, lk2_ref, lam_init):
    a = jnp.sum(lq1_ref[...] * lk1_ref[...], axis=-1, keepdims=True)
    b = jnp.sum(lq2_ref[...] * lk2_ref[...], axis=-1, keepdims=True)
    return jnp.exp(a) - jnp.exp(b) + lam_init


def _diff_finish(o, g_ref, lam_init):
    ms = jnp.mean(o * o, axis=-1, keepdims=True)
    return o * lax.rsqrt(ms + LN_EPS) * g_ref[...] * (1.0 - lam_init)


def _split_heads(k):
    lane = lax.broadcasted_iota(jnp.int32, k.shape, 1)
    first = lane < DIFF_HD
    return jnp.where(first, k, 0.0).astype(BF16), jnp.where(first, 0.0, k).astype(BF16)


def _causal_tiles(n_q, q_per_k):
    return [(j, (1 + qi % q_per_k) * int(j == qi // q_per_k), int(j == 0), int(j == qi // q_per_k), qi)
            for qi in range(n_q) for j in range(qi // q_per_k + 1)]


def _diff_attn_prompt_kernel(tab_ref, q_ref, k_ref, v_ref, lq1_ref, lk1_ref, lq2_ref, lk2_ref, g_ref,
                             o_ref, kx_ref, vt_ref, qt_ref, mask_ref, s_ref, p_ref, acc_ref, *,
                             tq, tk, heads, lam_init):
    seq = q_ref.shape[2]
    n_q, n_k = seq // tq, seq // tk
    q_per_k = tk // tq
    tiles = _causal_tiles(n_q, q_per_k)
    n_tiles = len(tiles)
    chunks_q, chunks_k = tq // CHUNK, tk // CHUNK
    streams = [(hd, mp) for hd in range(heads) for mp in range(2)]

    lane = lax.broadcasted_iota(jnp.int32, (seq, HEAD_W), 1)
    row_chunk = (lax.broadcasted_iota(jnp.int32, (seq, HEAD_W), 0) // CHUNK) % chunks_k
    chunk_one_hot = jnp.where(lane == row_chunk, 1.0, 0.0).astype(BF16)
    first = lane < DIFF_HD
    ones_row = jnp.where(lax.broadcasted_iota(jnp.int32, (SUM_ROWS, tk), 0) == 0, 1.0, 0.0).astype(BF16)
    for hd in range(heads):
        k = k_ref[0, hd]
        kx_ref[hd, 0, :, :HEAD_W] = jnp.where(first, k, 0.0).astype(BF16)
        kx_ref[hd, 1, :, :HEAD_W] = jnp.where(first, 0.0, k).astype(BF16)
        kx_ref[hd, 0, :, HEAD_W:] = chunk_one_hot
        kx_ref[hd, 1, :, HEAD_W:] = chunk_one_hot
        for t in range(n_k):
            vt_ref[hd, t, :HEAD_W, :] = v_ref[0, hd, t * tk:(t + 1) * tk, :].T.astype(BF16)
            vt_ref[hd, t, HEAD_W:, :] = ones_row
        for qi in range(n_q):
            qt_ref[hd, qi] = q_ref[0, hd, qi * tq:(qi + 1) * tq, :].astype(F32).T.astype(BF16)
    bias_row = lax.broadcasted_iota(jnp.int32, (HEAD_W, tq), 0)
    bias_col = lax.broadcasted_iota(jnp.int32, (HEAD_W, tq), 1) // CHUNK
    mask_ref[0] = jnp.zeros((HEAD_W, tq), BF16)
    for r in range(q_per_k):
        hidden = (bias_row < chunks_k) & (bias_row > bias_col + chunks_q * r)
        mask_ref[1 + r] = jnp.where(hidden, NEG_INF, 0.0).astype(BF16)
    lam = _diff_lambda(lq1_ref, lk1_ref, lq2_ref, lk2_ref, lam_init)

    def tile_of(i):
        if isinstance(i, int):
            j, mask, is_first, is_last, qi = tiles[i]
            return j, mask, bool(is_first), bool(is_last), qi
        j, mask, is_first, is_last, qi = (tab_ref[i, f] for f in range(5))
        return j, mask, is_first == 1, is_last == 1, qi

    def stage_a(i, slot):
        j, mask, _, _, qi = tile_of(i)
        base = j * tk if isinstance(j, int) else pl.multiple_of(j * tk, tk)
        tops = []
        for hd, mp in streams:
            q_op = jnp.concatenate([qt_ref[hd, qi], mask_ref[mask]], axis=0)
            s = _dot(kx_ref[hd, mp, pl.ds(base, tk), :], q_op)
            s_ref[hd, slot, mp] = s
            tops.append(jnp.max(s, axis=0, keepdims=True))
        return tuple(tops)

    def stage_b(i, slot, m, top):
        _, _, is_first, _, _ = tile_of(i)
        out = []
        for n, (hd, mp) in enumerate(streams):
            m_prev = jnp.where(is_first, NEG_INF, m[n])
            m_new = jnp.maximum(m_prev, top[n])
            scale = jnp.exp2(m_prev - m_new)
            p_ref[hd, slot, mp] = jnp.exp2(s_ref[hd, slot, mp] - m_new).astype(BF16)
            out.append((m_new, scale))
        return tuple(zip(*out))

    def stage_c(i, slot, scale):
        j = tile_of(i)[0]
        for n, (hd, mp) in enumerate(streams):
            acc_ref[hd, mp] = scale[n] * acc_ref[hd, mp] + _dot(vt_ref[hd, j], p_ref[hd, slot, mp])

    def finish_if_last(i):
        _, _, _, is_last, qi = tile_of(i)

        def finish():
            rows = pl.ds(qi * tq if isinstance(qi, int) else pl.multiple_of(qi * tq, tq), tq)
            for hd in range(heads):
                w1 = 1.0 / acc_ref[hd, 0, HEAD_W:HEAD_W + 1, :]
                w2 = lam / acc_ref[hd, 1, HEAD_W:HEAD_W + 1, :]
                o_t = acc_ref[hd, 0, :HEAD_W, :] * w1 - acc_ref[hd, 1, :HEAD_W, :] * w2
                ms = jnp.mean(o_t * o_t, axis=0, keepdims=True)
                o = (o_t * (lax.rsqrt(ms + LN_EPS) * (1.0 - lam_init))).T * g_ref[...]
                o_ref[0, rows, hd * HEAD_W:(hd + 1) * HEAD_W] = o.astype(o_ref.dtype)

        if isinstance(is_last, bool):
            if is_last:
                finish()
        else:
            pl.when(is_last)(finish)

    def iteration(i, slot, carry, a=True, b=True, c=True):
        m, top, scale = carry
        new_scale = scale
        if b:
            m, new_scale = stage_b(i - 1, 1 - slot, m, top)
        if a:
            top = stage_a(i, slot)
        if c:
            stage_c(i - 2, slot, scale)
            finish_if_last(i - 2)
        return m, top, new_scale

    def pair(n, carry):
        i = 2 + 2 * n
        return iteration(i + 1, 1, iteration(i, 0, carry))

    zero = (jnp.zeros((1, tq), F32),) * len(streams)
    acc_ref[...] = jnp.zeros(acc_ref.shape, F32)
    carry = (zero, zero, zero)
    carry = iteration(0, 0, carry, b=False, c=False)
    carry = iteration(1, 1, carry, c=False)
    carry = lax.fori_loop(0, (n_tiles - 2) // 2, pair, carry)
    carry = iteration(n_tiles, 0, carry, a=False)
    iteration(n_tiles + 1, 1, carry, a=False, b=False)


def _diff_attn_prompt(q, k, v, lam_vecs, subln_g, lam_init, *, tq, tk, heads):
    b, _, s, _ = q.shape
    rows = _causal_tiles(s // tq, tk // tq)
    assert len(rows) % 2 == 0
    table = jnp.asarray(np.array(rows, np.int32))
    kern = functools.partial(_diff_attn_prompt_kernel, tq=tq, tk=tk, heads=heads, lam_init=lam_init)
    const = lambda shape: pl.BlockSpec(shape, lambda i, h, tab: (0,) * len(shape))
    head_spec = pl.BlockSpec((1, heads, s, HEAD_W), lambda i, h, tab: (i, h, 0, 0))
    return pl.pallas_call(
        kern,
        grid_spec=pltpu.PrefetchScalarGridSpec(
            num_scalar_prefetch=1,
            grid=(b, DIFF_HEADS // heads),
            in_specs=[head_spec, head_spec, head_spec] + [const((1, DIFF_HD))] * 4 + [const((1, HEAD_W))],
            out_specs=pl.BlockSpec((1, s, heads * HEAD_W), lambda i, h, tab: (i, 0, h)),
            scratch_shapes=[
                pltpu.VMEM((heads, 2, s, 2 * HEAD_W), BF16),
                pltpu.VMEM((heads, s // tk, HEAD_W + SUM_ROWS, tk), BF16),
                pltpu.VMEM((heads, s // tq, HEAD_W, tq), BF16),
                pltpu.VMEM((1 + tk // tq, HEAD_W, tq), BF16),
                pltpu.VMEM((heads, 2, 2, tk, tq), F32),
                pltpu.VMEM((heads, 2, 2, tk, tq), BF16),
                pltpu.VMEM((heads, 2, HEAD_W + SUM_ROWS, tq), F32),
            ]),
        out_shape=jax.ShapeDtypeStruct((b, s, MIX_W), BF16),
        name="diff_attn_prompt",
        compiler_params=_params("arbitrary", "arbitrary"),
    )(table, q, k, v, *lam_vecs, subln_g.reshape(1, HEAD_W))


def _diff_attn_sample_kernel(q_ref, kc_ref, vc_ref, kn_ref, vn_ref, lq1_ref, lk1_ref, lq2_ref, lk2_ref,
                             g_ref, o_ref, *, lam_init):
    lam = _diff_lambda(lq1_ref, lk1_ref, lq2_ref, lk2_ref, lam_init)
    for h in range(q_ref.shape[1]):
        q = q_ref[0, h]
        s = q.shape[0]
        q1, q2 = _split_heads(q)
        qq = jnp.concatenate([q1, q2], axis=0)
        sc = _dot_nt(qq, kc_ref[0, h])
        sn = _dot_nt(qq, kn_ref[0, h])
        m = jnp.maximum(jnp.max(sc, axis=-1, keepdims=True), jnp.max(sn, axis=-1, keepdims=True))
        pc = jnp.exp2(sc - m)
        pn = jnp.exp2(sn - m)
        l = jnp.sum(pc, axis=-1, keepdims=True) + jnp.sum(pn, axis=-1, keepdims=True)
        o2 = (_dot(pc, vc_ref[0, h]) + _dot(pn, vn_ref[0, h])) / l
        o = o2[:s] - lam * o2[s:]
        o_ref[0, :, h * HEAD_W:(h + 1) * HEAD_W] = _diff_finish(o, g_ref, lam_init).astype(o_ref.dtype)


def _diff_attn_sample(q, k_cache, v_cache, k_new, v_new, lam_vecs, subln_g, lam_init):
    b, heads, s, _ = q.shape
    past = k_cache.shape[2]
    kern = functools.partial(_diff_attn_sample_kernel, lam_init=lam_init)
    vec_spec = _const_spec((1, DIFF_HD))
    new_spec = pl.BlockSpec((1, heads, s, HEAD_W), lambda i: (i, 0, 0, 0))
    old_spec = pl.BlockSpec((1, heads, past, HEAD_W), lambda i: (i, 0, 0, 0))
    return pl.pallas_call(
        kern,
        grid=(b,),
        in_specs=[new_spec, old_spec, old_spec, new_spec, new_spec, vec_spec, vec_spec, vec_spec,
                  vec_spec, _const_spec((1, HEAD_W))],
        out_specs=pl.BlockSpec((1, s, heads * HEAD_W), lambda i: (i, 0, 0)),
        out_shape=jax.ShapeDtypeStruct((b, s, MIX_W), BF16),
        name="diff_attn_sample",
        compiler_params=_params("arbitrary"),
    )(q, k_cache, v_cache, k_new, v_new, *lam_vecs, subln_g.reshape(1, HEAD_W))


def _post_kernel(x_ref, o_ref, m_ref, wo_ref, g1_ref, b1_ref, wg_ref, wu_ref, wd_ref, g2_ref, b2_ref,
                 y_ref, h_ref, *, ff_chunk, row_groups):
    tm = x_ref.shape[0]
    groups = [slice(r * tm // row_groups, (r + 1) * tm // row_groups) for r in range(row_groups)]
    x1, x1b = [], []
    for rows in groups:
        mixed = (_dot(o_ref[rows, :], wo_ref[0, :MIX_W, :])
                 + _dot(m_ref[rows, :], wo_ref[0, MIX_W:, :]))
        x1.append(_layer_norm_rows(ALPHA * x_ref[rows, :] + mixed, g1_ref[0], b1_ref[0]))
        x1b.append(x1[-1].astype(BF16))
    for c in range(D_FF // ff_chunk):
        cols = slice(c * ff_chunk, (c + 1) * ff_chunk)
        for r, rows in enumerate(groups):
            gate = _dot(x1b[r], wg_ref[0, :, cols])
            up = _dot(x1b[r], wu_ref[0, :, cols])
            h_ref[rows, cols] = (gate / (1.0 + jnp.exp(-gate)) * up).astype(BF16)
    for r, rows in enumerate(groups):
        ff = _dot(h_ref[rows, :], wd_ref[0])
        y_ref[rows, :] = _layer_norm_rows(ALPHA * x1[r] + ff, g2_ref[0], b2_ref[0])


def _post(x, o, m, layer, wo, g1, b1, wg, wu, wd, g2, b2, *, tm, ff_chunk, row_groups):
    n = x.shape[0]
    row = lambda w: pl.BlockSpec((tm, w), lambda i: (i, 0))
    vec = _layer_spec((1, D_MODEL), layer)
    kern = functools.partial(_post_kernel, ff_chunk=ff_chunk, row_groups=row_groups)
    r1 = lambda a: a.reshape(DEPTH, 1, D_MODEL)
    return pl.pallas_call(
        kern,
        grid=(n // tm,),
        in_specs=[row(D_MODEL), row(MIX_W), row(MEM_W), _layer_spec((D_MODEL, D_MODEL), layer), vec, vec,
                  _layer_spec((D_MODEL, D_FF), layer), _layer_spec((D_MODEL, D_FF), layer),
                  _layer_spec((D_FF, D_MODEL), layer), vec, vec],
        out_specs=row(D_MODEL),
        out_shape=jax.ShapeDtypeStruct((n, D_MODEL), F32),
        scratch_shapes=[pltpu.VMEM((tm, D_FF), BF16)],
        name="post_mixer",
        compiler_params=_params("arbitrary"),
    )(x, o, m, wo, r1(g1), r1(b1), wg, wu, wd, r1(g2), r1(b2))


def kernel(x_prompt, x_sample, mem_prompt, cache_ret_state, cache_diff_k, cache_diff_v, cache_mem_k,
           cache_mem_v, ret_w_in, ret_gn_g, diff_w_in, diff_lambda_q1, diff_lambda_k1, diff_lambda_q2,
           diff_lambda_k2, diff_subln_g, w_mem_kv, w_o, ln1_g, ln1_b, w_gate, w_up, w_down, ln2_g, ln2_b):
    bp, sp, _ = x_prompt.shape
    bs, ss, _ = x_sample.shape
    pos_p = jnp.arange(sp)
    pos_s = PAST_LEN + jnp.arange(ss)
    bf = lambda a: a.astype(BF16)

    mem_k_p, mem_v_p = _mem_kv(mem_prompt, bf(w_mem_kv.transpose(0, 2, 1)))
    mem_t = lambda a: a.transpose(0, 1, 3, 4, 2).reshape(DEPTH, bs, MEM_W, N_MEM)
    mem_k_s, mem_v_s = mem_t(cache_mem_k), mem_t(cache_mem_v)

    ret_w, diff_w = bf(ret_w_in), bf(diff_w_in)
    wo, wg, wu, wd = bf(w_o), bf(w_gate), bf(w_up), bf(w_down)

    xp, xs = x_prompt, x_sample
    ret_p, ret_s, dkp, dvp, dks, dvs = [], [], [], [], [], []
    to_seq_major = lambda a: a.transpose(0, 2, 1, 3)
    for i in range(DEPTH):
        j = i // 2
        if i % 2 == 0:
            r0 = jnp.zeros((bp, RET_HEADS, RET_HD, RET_HD), F32)
            op, mp, rp = _ret_mixer(xp, pos_p, r0, mem_k_p, mem_v_p, i, ret_w, j, ret_gn_g[j],
                                    t_step=512, tile=256)
            os_, ms, rs = _ret_mixer(xs, pos_s, cache_ret_state[j], mem_k_s, mem_v_s, i, ret_w, j,
                                     ret_gn_g[j], t_step=ss, tile=ss)
            ret_p.append(rp)
            ret_s.append(rs)
        else:
            lam_init = _lambda_init(i)
            lam_vecs = [a[j].reshape(1, DIFF_HD) for a in
                        (diff_lambda_q1, diff_lambda_k1, diff_lambda_q2, diff_lambda_k2)]
            q, k, v, mp = _diff_project(xp, pos_p, mem_k_p, mem_v_p, i, diff_w, j, t_step=512)
            op = _diff_attn_prompt(q, k, v, lam_vecs, diff_subln_g[j], lam_init, tq=256, tk=512, heads=2)
            dkp.append(to_seq_major(k))
            dvp.append(to_seq_major(v))
            q, k, v, ms = _diff_project(xs, pos_s, mem_k_s, mem_v_s, i, diff_w, j, t_step=ss)
            os_ = _diff_attn_sample(q, cache_diff_k[j].transpose(0, 2, 1, 3),
                                    cache_diff_v[j].transpose(0, 2, 1, 3), k, v, lam_vecs,
                                    diff_subln_g[j], lam_init)
            dks.append(to_seq_major(k))
            dvs.append(to_seq_major(v))
        post = functools.partial(_post, layer=i, wo=wo, g1=ln1_g, b1=ln1_b, wg=wg, wu=wu, wd=wd,
                                 g2=ln2_g, b2=ln2_b, tm=512, ff_chunk=256, row_groups=2)
        xp = post(xp.reshape(bp * sp, D_MODEL), op.reshape(bp * sp, MIX_W),
                  mp.reshape(bp * sp, MEM_W)).reshape(bp, sp, D_MODEL)
        xs = post(xs.reshape(bs * ss, D_MODEL), os_.reshape(bs * ss, MIX_W),
                  ms.reshape(bs * ss, MEM_W)).reshape(bs, ss, D_MODEL)

    mem_out = lambda a: a.reshape(DEPTH, bp, MEM_HEADS, MEM_HD, N_MEM).transpose(0, 1, 4, 2, 3)
    return (xp, xs, jnp.stack(ret_p), jnp.stack(ret_s), jnp.stack(dkp), jnp.stack(dvp),
            jnp.stack(dks), jnp.stack(dvs), mem_out(mem_k_p), mem_out(mem_v_p))
```

```python
import functools
import math

import jax
import jax.numpy as jnp
import numpy as np
from jax import lax
from jax.experimental import pallas as pl
from jax.experimental.pallas import tpu as pltpu

D_MODEL = 1024
DEPTH = 2
PAST_LEN = 1024
CHUNK = 64
N_MEM = 256
MEM_HEADS = 4
MEM_HD = 64
MEM_W = MEM_HEADS * MEM_HD
MIX_W = D_MODEL - MEM_W
RET_HEADS = 6
RET_HD = MIX_W // RET_HEADS
RET_THETA = 10000.0
DIFF_HEADS = 6
DIFF_HD = MIX_W // (2 * DIFF_HEADS)
ROPE_THETA = 500000.0
ROT_DIM = DIFF_HD // 4
D_FF = -(-8 * D_MODEL // (3 * 256)) * 256
ALPHA = (2 * DEPTH) ** 0.25
LN_EPS = 1e-5
NEG_INF = -1e30

Q_SCALE = DIFF_HD ** -0.5 * math.log2(math.e)
SUM_ROWS = 16
HEAD_W = 128
VMEM_LIMIT = 56 * 1024 * 1024

F32 = jnp.float32
BF16 = jnp.bfloat16


def _lambda_init(layer_idx):
    return 0.8 - 0.6 * math.exp(-0.3 * layer_idx)


def _dot(a, b):
    return jnp.dot(a.astype(BF16), b.astype(BF16), preferred_element_type=F32)


def _dot_nt(a, b):
    return lax.dot_general(a.astype(BF16), b.astype(BF16), (((1,), (1,)), ((), ())),
                           preferred_element_type=F32)


def _dot_tn(a, b):
    return lax.dot_general(a.astype(BF16), b.astype(BF16), (((0,), (0,)), ((), ())),
                           preferred_element_type=F32)


def _fixed_spec(shape, index):
    return pl.BlockSpec(shape, lambda *_: index, pipeline_mode=pl.Buffered(1))


def _const_spec(shape):
    return _fixed_spec(shape, (0,) * len(shape))


def _layer_spec(shape, layer):
    return _fixed_spec((1,) + shape, (layer,) + (0,) * len(shape))


def _params(*semantics):
    return pltpu.CompilerParams(dimension_semantics=semantics, vmem_limit_bytes=VMEM_LIMIT)


def _layer_norm_rows(y, g, b):
    mu = jnp.mean(y, axis=-1, keepdims=True)
    d = y - mu
    var = jnp.mean(d * d, axis=-1, keepdims=True)
    return d * lax.rsqrt(var + LN_EPS) * g + b


def _softmax_rows(s):
    s = s - jnp.max(s, axis=-1, keepdims=True)
    p = jnp.exp(s)
    return p / jnp.sum(p, axis=-1, keepdims=True)


def _memory_attention(mq, mk_t, mv_t):
    row = lax.broadcasted_iota(jnp.int32, mk_t.shape, 0)
    mqb = (mq * MEM_HD ** -0.5).astype(BF16)
    out = None
    for h in range(MEM_HEADS):
        sel = (row >= h * MEM_HD) & (row < (h + 1) * MEM_HD)
        p = _softmax_rows(_dot(mqb, jnp.where(sel, mk_t, 0.0)))
        o = _dot_nt(p, jnp.where(sel, mv_t, 0.0))
        out = o if out is None else out + o
    return out


def _mem_kv_kernel(x_ref, w_ref, k_ref, v_ref):
    kv_t = _dot_nt(w_ref[0], x_ref[0])
    k_ref[0, 0] = kv_t[:MEM_W]
    v_ref[0, 0] = kv_t[MEM_W:]


def _mem_kv(mem, w_t_bf):
    b = mem.shape[0]
    out = jax.ShapeDtypeStruct((DEPTH, b, MEM_W, N_MEM), F32)
    return pl.pallas_call(
        _mem_kv_kernel,
        grid=(DEPTH, b),
        in_specs=[pl.BlockSpec((1, N_MEM, D_MODEL), lambda l, i: (i, 0, 0)),
                  pl.BlockSpec((1, 2 * MEM_W, D_MODEL), lambda l, i: (l, 0, 0))],
        out_specs=[pl.BlockSpec((1, 1, MEM_W, N_MEM), lambda l, i: (l, i, 0, 0))] * 2,
        out_shape=[out, out],
        name="mem_kv",
        compiler_params=_params("arbitrary", "arbitrary"),
    )(mem, w_t_bf)


def _ret_tables(tile):
    h = np.arange(RET_HEADS, dtype=np.float64)
    log_g = np.log(1.0 - np.exp2(-5.0 - h))
    idx = np.arange(tile, dtype=np.float64)
    dist = np.abs(idx[:, None] - idx[None, :])
    visible = (idx[None, :] // CHUNK) <= (idx[:, None] // CHUNK)
    decay = np.where(visible[None], np.exp(dist[None] * log_g[:, None, None]), 0.0)
    xi = np.exp((idx + 1.0)[None, :] * log_g[:, None])[:, :, None]
    zeta = np.exp((tile - 1.0 - idx)[None, :] * log_g[:, None])[:, :, None]
    g_tile = np.exp(tile * log_g)
    f = lambda a: jnp.asarray(a, F32)
    return f(decay), f(xi), f(zeta), [float(g) for g in g_tile]


def _ret_kernel(x_ref, w_ref, cos_ref, sin_ref, r0_ref, mk_ref, mv_ref, gn_ref, decay_ref, xi_ref,
                zeta_ref, o_ref, m_ref, r_ref, state_ref, *, tile, g_tile):
    step = pl.program_id(1)

    @pl.when(step == 0)
    def _():
        state_ref[...] = r0_ref[0]

    proj = _dot(x_ref[0], w_ref[0])
    t = proj.shape[0]
    cos = cos_ref[...]
    sin = sin_ref[...]

    def rope(a):
        return a * cos + pltpu.roll(a, RET_HD // 2, 1) * sin

    for h in range(RET_HEADS):
        col = h * HEAD_W
        q = rope(proj[:, col:col + HEAD_W])
        k = rope(proj[:, MIX_W + col:MIX_W + col + HEAD_W]) * RET_HD ** -0.5
        v = proj[:, 2 * MIX_W + col:2 * MIX_W + col + HEAD_W]
        gate = proj[:, 3 * MIX_W + col:3 * MIX_W + col + HEAD_W]
        outs = []
        r = state_ref[h]
        for c in range(t // tile):
            rows = slice(c * tile, (c + 1) * tile)
            qc, kc, vc = q[rows], k[rows], v[rows]
            inner = _dot_nt(qc, kc) * decay_ref[h]
            outs.append(_dot(inner, vc) + _dot(qc, r) * xi_ref[h])
            r = g_tile[h] * r + _dot_tn(kc * zeta_ref[h], vc)
        state_ref[h] = r
        o = outs[0] if len(outs) == 1 else jnp.concatenate(outs, axis=0)
        mu = jnp.mean(o, axis=-1, keepdims=True)
        d = o - mu
        var = jnp.mean(d * d, axis=-1, keepdims=True)
        o = d * lax.rsqrt(var + LN_EPS) * gn_ref[:, col:col + HEAD_W]
        o = o * (gate / (1.0 + jnp.exp(-gate)))
        o_ref[0, :, col:col + HEAD_W] = o.astype(o_ref.dtype)

    m = _memory_attention(proj[:, 4 * MIX_W:], mk_ref[0, 0], mv_ref[0, 0])
    m_ref[0] = m.astype(m_ref.dtype)

    @pl.when(step == pl.num_programs(1) - 1)
    def _():
        r_ref[0] = state_ref[...]


def _ret_mixer(x, pos, r0, mk_t, mv_t, layer, w_bf, w_layer, gn_g, *, t_step, tile):
    b, s, _ = x.shape
    half = RET_HD // 2
    lane_freq = jnp.arange(HEAD_W, dtype=jnp.int32) % half
    inv_freq = jnp.exp(-math.log(RET_THETA) * lane_freq.astype(F32) * 2.0 / RET_HD)
    ang = pos.astype(F32)[:, None] * inv_freq[None, :]
    cos = jnp.cos(ang)
    sin = jnp.where(jnp.arange(HEAD_W) < half, -jnp.sin(ang), jnp.sin(ang))
    decay, xi, zeta, g_tile = _ret_tables(tile)
    cols = w_bf.shape[-1]
    kern = functools.partial(_ret_kernel, tile=tile, g_tile=g_tile)
    mem_spec = pl.BlockSpec((1, 1, MEM_W, N_MEM), lambda i, j: (layer, i, 0, 0))
    return pl.pallas_call(
        kern,
        grid=(b, s // t_step),
        in_specs=[
            pl.BlockSpec((1, t_step, D_MODEL), lambda i, j: (i, j, 0)),
            _layer_spec((D_MODEL, cols), w_layer),
            pl.BlockSpec((t_step, HEAD_W), lambda i, j: (j, 0)),
            pl.BlockSpec((t_step, HEAD_W), lambda i, j: (j, 0)),
            pl.BlockSpec((1, RET_HEADS, RET_HD, RET_HD), lambda i, j: (i, 0, 0, 0)),
            mem_spec, mem_spec,
            _const_spec((1, MIX_W)),
            _const_spec((RET_HEADS, tile, tile)),
            _const_spec((RET_HEADS, tile, 1)),
            _const_spec((RET_HEADS, tile, 1)),
        ],
        out_specs=[
            pl.BlockSpec((1, t_step, MIX_W), lambda i, j: (i, j, 0)),
            pl.BlockSpec((1, t_step, MEM_W), lambda i, j: (i, j, 0)),
            pl.BlockSpec((1, RET_HEADS, RET_HD, RET_HD), lambda i, j: (i, 0, 0, 0)),
        ],
        out_shape=[
            jax.ShapeDtypeStruct((b, s, MIX_W), BF16),
            jax.ShapeDtypeStruct((b, s, MEM_W), BF16),
            jax.ShapeDtypeStruct((b, RET_HEADS, RET_HD, RET_HD), F32),
        ],
        scratch_shapes=[pltpu.VMEM((RET_HEADS, RET_HD, RET_HD), F32)],
        name="ret_mixer",
        compiler_params=_params("arbitrary", "arbitrary"),
    )(x, w_bf, cos, sin, r0, mk_t, mv_t, gn_g.reshape(1, MIX_W), decay, xi, zeta)


def _diff_proj_kernel(x_ref, w_ref, c_ref, sa_ref, sb_ref, mk_ref, mv_ref, q_ref, k_ref, v_ref, m_ref):
    proj = _dot(x_ref[0], w_ref[0])
    c = c_ref[...]
    sa = sa_ref[...]
    sb = sb_ref[...]

    def rope(a):
        return (a * c + pltpu.roll(a, ROT_DIM // 2, 1) * sa
                + pltpu.roll(a, HEAD_W - ROT_DIM // 2, 1) * sb)

    for h in range(DIFF_HEADS):
        col = h * HEAD_W
        q = rope(proj[:, col:col + HEAD_W]) * Q_SCALE
        q_ref[0, h] = q.astype(q_ref.dtype)
        k_ref[0, h] = rope(proj[:, MIX_W + col:MIX_W + col + HEAD_W])
        v_ref[0, h] = proj[:, 2 * MIX_W + col:2 * MIX_W + col + HEAD_W]
    m = _memory_attention(proj[:, 3 * MIX_W:], mk_ref[0, 0], mv_ref[0, 0])
    m_ref[0] = m.astype(m_ref.dtype)


def _diff_project(x, pos, mk_t, mv_t, layer, w_bf, w_layer, *, t_step):
    b, s, _ = x.shape
    half = ROT_DIM // 2
    lane = jnp.arange(HEAD_W, dtype=jnp.int32) % DIFF_HD
    inv_freq = jnp.exp(-math.log(ROPE_THETA) * (lane % half).astype(F32) * 2.0 / ROT_DIM)
    ang = pos.astype(F32)[:, None] * inv_freq[None, :]
    cos, sin = jnp.cos(ang), jnp.sin(ang)
    tables = [jnp.where(lane < ROT_DIM, cos, 1.0),
              jnp.where((lane >= half) & (lane < ROT_DIM), sin, 0.0),
              jnp.where(lane < half, -sin, 0.0)]
    cols = w_bf.shape[-1]
    tab_spec = pl.BlockSpec((t_step, HEAD_W), lambda i, j: (j, 0))
    tok_spec = lambda w: pl.BlockSpec((1, t_step, w), lambda i, j: (i, j, 0))
    head_spec = pl.BlockSpec((1, DIFF_HEADS, t_step, HEAD_W), lambda i, j: (i, 0, j, 0))
    mem_spec = pl.BlockSpec((1, 1, MEM_W, N_MEM), lambda i, j: (layer, i, 0, 0))
    heads = lambda dt: jax.ShapeDtypeStruct((b, DIFF_HEADS, s, HEAD_W), dt)
    return pl.pallas_call(
        _diff_proj_kernel,
        grid=(b, s // t_step),
        in_specs=[tok_spec(D_MODEL), _layer_spec((D_MODEL, cols), w_layer), tab_spec, tab_spec,
                  tab_spec, mem_spec, mem_spec],
        out_specs=[head_spec, head_spec, head_spec, tok_spec(MEM_W)],
        out_shape=[
            heads(BF16),
            heads(F32),
            heads(F32),
            jax.ShapeDtypeStruct((b, s, MEM_W), BF16),
        ],
        name="diff_proj",
        compiler_params=_params("arbitrary", "arbitrary"),
    )(x, w_bf, *tables, mk_t, mv_t)


def _diff_lambda(lq1_ref, lk1_ref, lq2_ref, lk2_ref, lam_init):
    a = jnp.sum(lq1_ref[...] * lk1_ref[...], axis=-1, keepdims=True)
    b = jnp.sum(lq2_ref[...] * lk2_ref[...], axis=-1, keepdims=True)
    return jnp.exp(a) - jnp.exp(b) + lam_init


def _diff_finish(o, g_ref, lam_init):
    ms = jnp.mean(o * o, axis=-1, keepdims=True)
    return o * lax.rsqrt(ms + LN_EPS) * g_ref[...] * (1.0 - lam_init)


def _split_heads(k):
    lane = lax.broadcasted_iota(jnp.int32, k.shape, 1)
    first = lane < DIFF_HD
    return jnp.where(first, k, 0.0).astype(BF16), jnp.where(first, 0.0, k).astype(BF16)


def _causal_tiles(n_k):
    return [(j, int(j == g), int(j == 0), int(j == g), g) for g in range(n_k) for j in range(g + 1)]


def _diff_attn_prompt_kernel(tab_ref, q_ref, k_ref, v_ref, lq1_ref, lk1_ref, lq2_ref, lk2_ref, g_ref,
                             o_ref, kx_ref, vt_ref, qt_ref, mask_ref, s_ref, p_ref, acc_ref, *,
                             tq, tk, heads, lam_init):
    seq = q_ref.shape[2]
    n_q, n_k = seq // tq, seq // tk
    q_per_k = tk // tq
    tiles = _causal_tiles(n_k)
    n_tiles = len(tiles)
    chunks_q, chunks_k = tq // CHUNK, tk // CHUNK
    streams = [(hd, qs, mp) for hd in range(heads) for qs in range(q_per_k) for mp in range(2)]

    lane = lax.broadcasted_iota(jnp.int32, (seq, HEAD_W), 1)
    row_chunk = (lax.broadcasted_iota(jnp.int32, (seq, HEAD_W), 0) // CHUNK) % chunks_k
    chunk_one_hot = jnp.where(lane == row_chunk, 1.0, 0.0).astype(BF16)
    first = lane < DIFF_HD
    ones_row = jnp.where(lax.broadcasted_iota(jnp.int32, (SUM_ROWS, tk), 0) == 0, 1.0, 0.0).astype(BF16)
    for hd in range(heads):
        k = k_ref[0, hd]
        kx_ref[hd, 0, :, :HEAD_W] = jnp.where(first, k, 0.0).astype(BF16)
        kx_ref[hd, 1, :, :HEAD_W] = jnp.where(first, 0.0, k).astype(BF16)
        kx_ref[hd, 0, :, HEAD_W:] = chunk_one_hot
        kx_ref[hd, 1, :, HEAD_W:] = chunk_one_hot
        for t in range(n_k):
            vt_ref[hd, t, :HEAD_W, :] = v_ref[0, hd, t * tk:(t + 1) * tk, :].T.astype(BF16)
            vt_ref[hd, t, HEAD_W:, :] = ones_row
        for qi in range(n_q):
            qt_ref[hd, qi] = q_ref[0, hd, qi * tq:(qi + 1) * tq, :].astype(F32).T.astype(BF16)
    bias_row = lax.broadcasted_iota(jnp.int32, (HEAD_W, tq), 0)
    bias_col = lax.broadcasted_iota(jnp.int32, (HEAD_W, tq), 1) // CHUNK
    mask_ref[0] = jnp.zeros((HEAD_W, tq), BF16)
    for qs in range(q_per_k):
        hidden = (bias_row < chunks_k) & (bias_row > bias_col + chunks_q * qs)
        mask_ref[1 + qs] = jnp.where(hidden, NEG_INF, 0.0).astype(BF16)
    lam = _diff_lambda(lq1_ref, lk1_ref, lq2_ref, lk2_ref, lam_init)

    def tile_of(i):
        if isinstance(i, int):
            j, diag, is_first, is_last, grp = tiles[i]
            return j, diag, bool(is_first), bool(is_last), grp
        j, diag, is_first, is_last, grp = (tab_ref[i, f] for f in range(5))
        return j, diag, is_first == 1, is_last == 1, grp

    def stage_a(i, slot):
        j, diag, _, _, grp = tile_of(i)
        base = j * tk if isinstance(j, int) else pl.multiple_of(j * tk, tk)
        tops = []
        for hd, qs, mp in streams:
            q_op = jnp.concatenate([qt_ref[hd, grp * q_per_k + qs], mask_ref[diag * (1 + qs)]], axis=0)
            s = _dot(kx_ref[hd, mp, pl.ds(base, tk), :], q_op)
            s_ref[hd, slot, qs, mp] = s
            tops.append(jnp.max(s, axis=0, keepdims=True))
        return tuple(tops)

    def stage_b(i, slot, m, top):
        _, _, is_first, _, _ = tile_of(i)
        out = []
        for n, (hd, qs, mp) in enumerate(streams):
            m_prev = jnp.where(is_first, NEG_INF, m[n])
            m_new = jnp.maximum(m_prev, top[n])
            scale = jnp.exp2(m_prev - m_new)
            p_ref[hd, slot, qs, mp] = jnp.exp2(s_ref[hd, slot, qs, mp] - m_new).astype(BF16)
            out.append((m_new, scale))
        return tuple(zip(*out))

    def stage_c(i, slot, scale):
        j = tile_of(i)[0]
        for n, (hd, qs, mp) in enumerate(streams):
            acc_ref[hd, qs, mp] = (scale[n] * acc_ref[hd, qs, mp]
                                   + _dot(vt_ref[hd, j], p_ref[hd, slot, qs, mp]))

    def finish_if_last(i):
        _, _, _, is_last, grp = tile_of(i)

        def finish():
            for hd in range(heads):
                for qs in range(q_per_k):
                    w1 = 1.0 / acc_ref[hd, qs, 0, HEAD_W:HEAD_W + 1, :]
                    w2 = lam / acc_ref[hd, qs, 1, HEAD_W:HEAD_W + 1, :]
                    o_t = acc_ref[hd, qs, 0, :HEAD_W, :] * w1 - acc_ref[hd, qs, 1, :HEAD_W, :] * w2
                    ms = jnp.mean(o_t * o_t, axis=0, keepdims=True)
                    o = (o_t * (lax.rsqrt(ms + LN_EPS) * (1.0 - lam_init))).T * g_ref[...]
                    start = (grp * q_per_k + qs) * tq
                    rows = pl.ds(start if isinstance(start, int) else pl.multiple_of(start, tq), tq)
                    o_ref[0, rows, hd * HEAD_W:(hd + 1) * HEAD_W] = o.astype(o_ref.dtype)

        if isinstance(is_last, bool):
            if is_last:
                finish()
        else:
            pl.when(is_last)(finish)

    def iteration(i, slot, carry, a=True, b=True, c=True):
        m, top, scale = carry
        new_scale = scale
        if b:
            m, new_scale = stage_b(i - 1, 1 - slot, m, top)
        if a:
            top = stage_a(i, slot)
        if c:
            stage_c(i - 2, slot, scale)
            finish_if_last(i - 2)
        return m, top, new_scale

    def pair(n, carry):
        i = 2 + 2 * n
        return iteration(i + 1, 1, iteration(i, 0, carry))

    zero = (jnp.zeros((1, tq), F32),) * len(streams)
    acc_ref[...] = jnp.zeros(acc_ref.shape, F32)
    carry = (zero, zero, zero)
    carry = iteration(0, 0, carry, b=False, c=False)
    carry = iteration(1, 1, carry, c=False)
    carry = lax.fori_loop(0, (n_tiles - 2) // 2, pair, carry)
    carry = iteration(n_tiles, 0, carry, a=False)
    iteration(n_tiles + 1, 1, carry, a=False, b=False)


def _diff_attn_prompt(q, k, v, lam_vecs, subln_g, lam_init, *, tq, tk, heads):
    b, _, s, _ = q.shape
    q_per_k = tk // tq
    rows = _causal_tiles(s // tk)
    assert len(rows) % 2 == 0
    table = jnp.asarray(np.array(rows, np.int32))
    kern = functools.partial(_diff_attn_prompt_kernel, tq=tq, tk=tk, heads=heads, lam_init=lam_init)
    const = lambda shape: pl.BlockSpec(shape, lambda i, h, tab: (0,) * len(shape))
    head_spec = pl.BlockSpec((1, heads, s, HEAD_W), lambda i, h, tab: (i, h, 0, 0))
    return pl.pallas_call(
        kern,
        grid_spec=pltpu.PrefetchScalarGridSpec(
            num_scalar_prefetch=1,
            grid=(b, DIFF_HEADS // heads),
            in_specs=[head_spec, head_spec, head_spec] + [const((1, DIFF_HD))] * 4 + [const((1, HEAD_W))],
            out_specs=pl.BlockSpec((1, s, heads * HEAD_W), lambda i, h, tab: (i, 0, h)),
            scratch_shapes=[
                pltpu.VMEM((heads, 2, s, 2 * HEAD_W), BF16),
                pltpu.VMEM((heads, s // tk, HEAD_W + SUM_ROWS, tk), BF16),
                pltpu.VMEM((heads, s // tq, HEAD_W, tq), BF16),
                pltpu.VMEM((1 + tk // tq, HEAD_W, tq), BF16),
                pltpu.VMEM((heads, 2, q_per_k, 2, tk, tq), F32),
                pltpu.VMEM((heads, 2, q_per_k, 2, tk, tq), BF16),
                pltpu.VMEM((heads, q_per_k, 2, HEAD_W + SUM_ROWS, tq), F32),
            ]),
        out_shape=jax.ShapeDtypeStruct((b, s, MIX_W), BF16),
        name="diff_attn_prompt",
        compiler_params=_params("arbitrary", "arbitrary"),
    )(table, q, k, v, *lam_vecs, subln_g.reshape(1, HEAD_W))


def _diff_attn_sample_kernel(q_ref, kc_ref, vc_ref, kn_ref, vn_ref, lq1_ref, lk1_ref, lq2_ref, lk2_ref,
                             g_ref, o_ref, *, lam_init):
    lam = _diff_lambda(lq1_ref, lk1_ref, lq2_ref, lk2_ref, lam_init)
    for h in range(q_ref.shape[1]):
        q = q_ref[0, h]
        s = q.shape[0]
        q1, q2 = _split_heads(q)
        qq = jnp.concatenate([q1, q2], axis=0)
        sc = _dot_nt(qq, kc_ref[0, h])
        sn = _dot_nt(qq, kn_ref[0, h])
        m = jnp.maximum(jnp.max(sc, axis=-1, keepdims=True), jnp.max(sn, axis=-1, keepdims=True))
        pc = jnp.exp2(sc - m)
        pn = jnp.exp2(sn - m)
        l = jnp.sum(pc, axis=-1, keepdims=True) + jnp.sum(pn, axis=-1, keepdims=True)
        o2 = (_dot(pc, vc_ref[0, h]) + _dot(pn, vn_ref[0, h])) / l
        o = o2[:s] - lam * o2[s:]
        o_ref[0, :, h * HEAD_W:(h + 1) * HEAD_W] = _diff_finish(o, g_ref, lam_init).astype(o_ref.dtype)


def _diff_attn_sample(q, k_cache, v_cache, k_new, v_new, lam_vecs, subln_g, lam_init):
    b, heads, s, _ = q.shape
    past = k_cache.shape[2]
    kern = functools.partial(_diff_attn_sample_kernel, lam_init=lam_init)
    vec_spec = _const_spec((1, DIFF_HD))
    new_spec = pl.BlockSpec((1, heads, s, HEAD_W), lambda i: (i, 0, 0, 0))
    old_spec = pl.BlockSpec((1, heads, past, HEAD_W), lambda i: (i, 0, 0, 0))
    return pl.pallas_call(
        kern,
        grid=(b,),
        in_specs=[new_spec, old_spec, old_spec, new_spec, new_spec, vec_spec, vec_spec, vec_spec,
                  vec_spec, _const_spec((1, HEAD_W))],
        out_specs=pl.BlockSpec((1, s, heads * HEAD_W), lambda i: (i, 0, 0)),
        out_shape=jax.ShapeDtypeStruct((b, s, MIX_W), BF16),
        name="diff_attn_sample",
        compiler_params=_params("arbitrary"),
    )(q, k_cache, v_cache, k_new, v_new, *lam_vecs, subln_g.reshape(1, HEAD_W))


def _post_kernel(x_ref, o_ref, m_ref, wo_ref, g1_ref, b1_ref, wg_ref, wu_ref, wd_ref, g2_ref, b2_ref,
                 y_ref, h_ref, *, ff_chunk, row_groups):
    tm = x_ref.shape[0]
    groups = [slice(r * tm // row_groups, (r + 1) * tm // row_groups) for r in range(row_groups)]
    x1, x1b = [], []
    for rows in groups:
        mixed = (_dot(o_ref[rows, :], wo_ref[0, :MIX_W, :])
                 + _dot(m_ref[rows, :], wo_ref[0, MIX_W:, :]))
        x1.append(_layer_norm_rows(ALPHA * x_ref[rows, :] + mixed, g1_ref[0], b1_ref[0]))
        x1b.append(x1[-1].astype(BF16))
    for c in range(D_FF // ff_chunk):
        cols = slice(c * ff_chunk, (c + 1) * ff_chunk)
        for r, rows in enumerate(groups):
            gate = _dot(x1b[r], wg_ref[0, :, cols])
            up = _dot(x1b[r], wu_ref[0, :, cols])
            h_ref[rows, cols] = (gate / (1.0 + jnp.exp(-gate)) * up).astype(BF16)
    for r, rows in enumerate(groups):
        ff = _dot(h_ref[rows, :], wd_ref[0])
        y_ref[rows, :] = _layer_norm_rows(ALPHA * x1[r] + ff, g2_ref[0], b2_ref[0])


def _post(x, o, m, layer, wo, g1, b1, wg, wu, wd, g2, b2, *, tm, ff_chunk, row_groups):
    n = x.shape[0]
    row = lambda w: pl.BlockSpec((tm, w), lambda i: (i, 0))
    vec = _layer_spec((1, D_MODEL), layer)
    kern = functools.partial(_post_kernel, ff_chunk=ff_chunk, row_groups=row_groups)
    r1 = lambda a: a.reshape(DEPTH, 1, D_MODEL)
    return pl.pallas_call(
        kern,
        grid=(n // tm,),
        in_specs=[row(D_MODEL), row(MIX_W), row(MEM_W), _layer_spec((D_MODEL, D_MODEL), layer), vec, vec,
                  _layer_spec((D_MODEL, D_FF), layer), _layer_spec((D_MODEL, D_FF), layer),
                  _layer_spec((D_FF, D_MODEL), layer), vec, vec],
        out_specs=row(D_MODEL),
        out_shape=jax.ShapeDtypeStruct((n, D_MODEL), F32),
        scratch_shapes=[pltpu.VMEM((tm, D_FF), BF16)],
        name="post_mixer",
        compiler_params=_params("arbitrary"),
    )(x, o, m, wo, r1(g1), r1(b1), wg, wu, wd, r1(g2), r1(b2))


def kernel(x_prompt, x_sample, mem_prompt, cache_ret_state, cache_diff_k, cache_diff_v, cache_mem_k,
           cache_mem_v, ret_w_in, ret_gn_g, diff_w_in, diff_lambda_q1, diff_lambda_k1, diff_lambda_q2,
           diff_lambda_k2, diff_subln_g, w_mem_kv, w_o, ln1_g, ln1_b, w_gate, w_up, w_down, ln2_g, ln2_b):
    bp, sp, _ = x_prompt.shape
    bs, ss, _ = x_sample.shape
    pos_p = jnp.arange(sp)
    pos_s = PAST_LEN + jnp.arange(ss)
    bf = lambda a: a.astype(BF16)

    mem_k_p, mem_v_p = _mem_kv(mem_prompt, bf(w_mem_kv.transpose(0, 2, 1)))
    mem_t = lambda a: a.transpose(0, 1, 3, 4, 2).reshape(DEPTH, bs, MEM_W, N_MEM)
    mem_k_s, mem_v_s = mem_t(cache_mem_k), mem_t(cache_mem_v)

    ret_w, diff_w = bf(ret_w_in), bf(diff_w_in)
    wo, wg, wu, wd = bf(w_o), bf(w_gate), bf(w_up), bf(w_down)

    xp, xs = x_prompt, x_sample
    ret_p, ret_s, dkp, dvp, dks, dvs = [], [], [], [], [], []
    to_seq_major = lambda a: a.transpose(0, 2, 1, 3)
    for i in range(DEPTH):
        j = i // 2
        if i % 2 == 0:
            r0 = jnp.zeros((bp, RET_HEADS, RET_HD, RET_HD), F32)
            op, mp, rp = _ret_mixer(xp, pos_p, r0, mem_k_p, mem_v_p, i, ret_w, j, ret_gn_g[j],
                                    t_step=512, tile=256)
            os_, ms, rs = _ret_mixer(xs, pos_s, cache_ret_state[j], mem_k_s, mem_v_s, i, ret_w, j,
                                     ret_gn_g[j], t_step=ss, tile=ss)
            ret_p.append(rp)
            ret_s.append(rs)
        else:
            lam_init = _lambda_init(i)
            lam_vecs = [a[j].reshape(1, DIFF_HD) for a in
                        (diff_lambda_q1, diff_lambda_k1, diff_lambda_q2, diff_lambda_k2)]
            q, k, v, mp = _diff_project(xp, pos_p, mem_k_p, mem_v_p, i, diff_w, j, t_step=512)
            op = _diff_attn_prompt(q, k, v, lam_vecs, diff_subln_g[j], lam_init, tq=256, tk=512, heads=2)
            dkp.append(to_seq_major(k))
            dvp.append(to_seq_major(v))
            q, k, v, ms = _diff_project(xs, pos_s, mem_k_s, mem_v_s, i, diff_w, j, t_step=ss)
            os_ = _diff_attn_sample(q, cache_diff_k[j].transpose(0, 2, 1, 3),
                                    cache_diff_v[j].transpose(0, 2, 1, 3), k, v, lam_vecs,
                                    diff_subln_g[j], lam_init)
            dks.append(to_seq_major(k))
            dvs.append(to_seq_major(v))
        post = functools.partial(_post, layer=i, wo=wo, g1=ln1_g, b1=ln1_b, wg=wg, wu=wu, wd=wd,
                                 g2=ln2_g, b2=ln2_b, tm=512, ff_chunk=256, row_groups=2)
        xp = post(xp.reshape(bp * sp, D_MODEL), op.reshape(bp * sp, MIX_W),
                  mp.reshape(bp * sp, MEM_W)).reshape(bp, sp, D_MODEL)
        xs = post(xs.reshape(bs * ss, D_MODEL), os_.reshape(bs * ss, MIX_W),
                  ms.reshape(bs * ss, MEM_W)).reshape(bs, ss, D_MODEL)

    mem_out = lambda a: a.reshape(DEPTH, bp, MEM_HEADS, MEM_HD, N_MEM).transpose(0, 1, 4, 2, 3)
    return (xp, xs, jnp.stack(ret_p), jnp.stack(ret_s), jnp.stack(dkp), jnp.stack(dvp),
            jnp.stack(dks), jnp.stack(dvs), mem_out(mem_k_p), mem_out(mem_v_p))
```

```python
import functools
import math

import jax
import jax.numpy as jnp
import numpy as np
from jax import lax
from jax.experimental import pallas as pl
from jax.experimental.pallas import tpu as pltpu

D_MODEL = 1024
DEPTH = 2
PAST_LEN = 1024
CHUNK = 64
N_MEM = 256
MEM_HEADS = 4
MEM_HD = 64
MEM_W = MEM_HEADS * MEM_HD
MIX_W = D_MODEL - MEM_W
RET_HEADS = 6
RET_HD = MIX_W // RET_HEADS
RET_THETA = 10000.0
DIFF_HEADS = 6
DIFF_HD = MIX_W // (2 * DIFF_HEADS)
ROPE_THETA = 500000.0
ROT_DIM = DIFF_HD // 4
D_FF = -(-8 * D_MODEL // (3 * 256)) * 256
ALPHA = (2 * DEPTH) ** 0.25
LN_EPS = 1e-5
NEG_INF = -1e30

Q_SCALE = DIFF_HD ** -0.5 * math.log2(math.e)
SAMPLE_GROUP = 4
SUM_ROWS = 16
HEAD_W = 128
VMEM_LIMIT = 56 * 1024 * 1024

F32 = jnp.float32
BF16 = jnp.bfloat16


def _lambda_init(layer_idx):
    return 0.8 - 0.6 * math.exp(-0.3 * layer_idx)


def _dot(a, b):
    return jnp.dot(a.astype(BF16), b.astype(BF16), preferred_element_type=F32)


def _dot_nt(a, b):
    return lax.dot_general(a.astype(BF16), b.astype(BF16), (((1,), (1,)), ((), ())),
                           preferred_element_type=F32)


def _dot_tn(a, b):
    return lax.dot_general(a.astype(BF16), b.astype(BF16), (((0,), (0,)), ((), ())),
                           preferred_element_type=F32)


def _fixed_spec(shape, index):
    return pl.BlockSpec(shape, lambda *_: index, pipeline_mode=pl.Buffered(1))


def _const_spec(shape):
    return _fixed_spec(shape, (0,) * len(shape))


def _layer_spec(shape, layer):
    return _fixed_spec((1,) + shape, (layer,) + (0,) * len(shape))


def _params(*semantics):
    return pltpu.CompilerParams(dimension_semantics=semantics, vmem_limit_bytes=VMEM_LIMIT)


def _layer_norm_rows(y, g, b):
    mu = jnp.mean(y, axis=-1, keepdims=True)
    d = y - mu
    var = jnp.mean(d * d, axis=-1, keepdims=True)
    return d * lax.rsqrt(var + LN_EPS) * g + b


def _softmax_rows(s):
    s = s - jnp.max(s, axis=-1, keepdims=True)
    p = jnp.exp(s)
    return p / jnp.sum(p, axis=-1, keepdims=True)


def _memory_attention(mq, mk_t, mv_t):
    row = lax.broadcasted_iota(jnp.int32, mk_t.shape, 0)
    mqb = (mq * MEM_HD ** -0.5).astype(BF16)
    out = None
    for h in range(MEM_HEADS):
        sel = (row >= h * MEM_HD) & (row < (h + 1) * MEM_HD)
        p = _softmax_rows(_dot(mqb, jnp.where(sel, mk_t, 0.0)))
        o = _dot_nt(p, jnp.where(sel, mv_t, 0.0))
        out = o if out is None else out + o
    return out


def _mem_kv_kernel(x_ref, w_ref, k_ref, v_ref):
    xb = x_ref[0].astype(BF16)
    for layer in range(DEPTH):
        kv_t = _dot_nt(w_ref[layer], xb)
        k_ref[layer, 0] = kv_t[:MEM_W]
        v_ref[layer, 0] = kv_t[MEM_W:]


def _mem_kv(mem, w_t_bf):
    b = mem.shape[0]
    out = jax.ShapeDtypeStruct((DEPTH, b, MEM_W, N_MEM), F32)
    return pl.pallas_call(
        _mem_kv_kernel,
        grid=(b,),
        in_specs=[pl.BlockSpec((1, N_MEM, D_MODEL), lambda i: (i, 0, 0)),
                  _const_spec((DEPTH, 2 * MEM_W, D_MODEL))],
        out_specs=[pl.BlockSpec((DEPTH, 1, MEM_W, N_MEM), lambda i: (0, i, 0, 0))] * 2,
        out_shape=[out, out],
        name="mem_kv",
        compiler_params=_params("arbitrary"),
    )(mem, w_t_bf)


def _ret_tables(tile):
    h = np.arange(RET_HEADS, dtype=np.float64)
    log_g = np.log(1.0 - np.exp2(-5.0 - h))
    idx = np.arange(tile, dtype=np.float64)
    dist = np.abs(idx[:, None] - idx[None, :])
    visible = (idx[None, :] // CHUNK) <= (idx[:, None] // CHUNK)
    decay = np.where(visible[None], np.exp(dist[None] * log_g[:, None, None]), 0.0)
    xi = np.exp((idx + 1.0)[None, :] * log_g[:, None])[:, :, None]
    zeta = np.exp((tile - 1.0 - idx)[None, :] * log_g[:, None])[:, :, None]
    g_tile = np.exp(tile * log_g)
    f = lambda a: jnp.asarray(a, F32)
    return f(decay), f(xi), f(zeta), [float(g) for g in g_tile]


def _ret_kernel(x_ref, w_ref, cos_ref, sin_ref, r0_ref, mk_ref, mv_ref, gn_ref, decay_ref, xi_ref,
                zeta_ref, o_ref, m_ref, r_ref, state_ref, *, tile, g_tile):
    step = pl.program_id(1)

    @pl.when(step == 0)
    def _():
        state_ref[...] = r0_ref[...]

    group, t, _ = x_ref.shape
    proj = _dot(x_ref[...].reshape(group * t, D_MODEL), w_ref[0])
    cos = cos_ref[...]
    sin = sin_ref[...]

    def rope(a):
        return a * cos + pltpu.roll(a, RET_HD // 2, 1) * sin

    for g in range(group):
        own = slice(g * t, (g + 1) * t)
        for h in range(RET_HEADS):
            col = h * HEAD_W
            q = rope(proj[own, col:col + HEAD_W])
            k = rope(proj[own, MIX_W + col:MIX_W + col + HEAD_W]) * RET_HD ** -0.5
            v = proj[own, 2 * MIX_W + col:2 * MIX_W + col + HEAD_W]
            gate = proj[own, 3 * MIX_W + col:3 * MIX_W + col + HEAD_W]
            outs = []
            r = state_ref[g, h]
            for c in range(t // tile):
                rows = slice(c * tile, (c + 1) * tile)
                qc, kc, vc = q[rows], k[rows], v[rows]
                inner = _dot_nt(qc, kc) * decay_ref[h]
                outs.append(_dot(inner, vc) + _dot(qc, r) * xi_ref[h])
                r = g_tile[h] * r + _dot_tn(kc * zeta_ref[h], vc)
            state_ref[g, h] = r
            o = outs[0] if len(outs) == 1 else jnp.concatenate(outs, axis=0)
            mu = jnp.mean(o, axis=-1, keepdims=True)
            d = o - mu
            var = jnp.mean(d * d, axis=-1, keepdims=True)
            o = d * lax.rsqrt(var + LN_EPS) * gn_ref[:, col:col + HEAD_W]
            o = o * (gate / (1.0 + jnp.exp(-gate)))
            o_ref[g, :, col:col + HEAD_W] = o.astype(o_ref.dtype)

        m = _memory_attention(proj[own, 4 * MIX_W:], mk_ref[0, g], mv_ref[0, g])
        m_ref[g] = m.astype(m_ref.dtype)

    @pl.when(step == pl.num_programs(1) - 1)
    def _():
        r_ref[...] = state_ref[...]


def _ret_mixer(x, pos, r0, mk_t, mv_t, layer, w_bf, w_layer, gn_g, *, t_step, tile, group):
    b, s, _ = x.shape
    half = RET_HD // 2
    lane_freq = jnp.arange(HEAD_W, dtype=jnp.int32) % half
    inv_freq = jnp.exp(-math.log(RET_THETA) * lane_freq.astype(F32) * 2.0 / RET_HD)
    ang = pos.astype(F32)[:, None] * inv_freq[None, :]
    cos = jnp.cos(ang)
    sin = jnp.where(jnp.arange(HEAD_W) < half, -jnp.sin(ang), jnp.sin(ang))
    decay, xi, zeta, g_tile = _ret_tables(tile)
    cols = w_bf.shape[-1]
    kern = functools.partial(_ret_kernel, tile=tile, g_tile=g_tile)
    mem_spec = pl.BlockSpec((1, group, MEM_W, N_MEM), lambda i, j: (layer, i, 0, 0))
    state_spec = pl.BlockSpec((group, RET_HEADS, RET_HD, RET_HD), lambda i, j: (i, 0, 0, 0))
    return pl.pallas_call(
        kern,
        grid=(b // group, s // t_step),
        in_specs=[
            pl.BlockSpec((group, t_step, D_MODEL), lambda i, j: (i, j, 0)),
            _layer_spec((D_MODEL, cols), w_layer),
            pl.BlockSpec((t_step, HEAD_W), lambda i, j: (j, 0)),
            pl.BlockSpec((t_step, HEAD_W), lambda i, j: (j, 0)),
            state_spec,
            mem_spec, mem_spec,
            _const_spec((1, MIX_W)),
            _const_spec((RET_HEADS, tile, tile)),
            _const_spec((RET_HEADS, tile, 1)),
            _const_spec((RET_HEADS, tile, 1)),
        ],
        out_specs=[
            pl.BlockSpec((group, t_step, MIX_W), lambda i, j: (i, j, 0)),
            pl.BlockSpec((group, t_step, MEM_W), lambda i, j: (i, j, 0)),
            state_spec,
        ],
        out_shape=[
            jax.ShapeDtypeStruct((b, s, MIX_W), BF16),
            jax.ShapeDtypeStruct((b, s, MEM_W), BF16),
            jax.ShapeDtypeStruct((b, RET_HEADS, RET_HD, RET_HD), F32),
        ],
        scratch_shapes=[pltpu.VMEM((group, RET_HEADS, RET_HD, RET_HD), F32)],
        name="ret_mixer",
        compiler_params=_params("arbitrary", "arbitrary"),
    )(x, w_bf, cos, sin, r0, mk_t, mv_t, gn_g.reshape(1, MIX_W), decay, xi, zeta)


def _diff_proj_kernel(x_ref, w_ref, c_ref, sa_ref, sb_ref, mk_ref, mv_ref, q_ref, k_ref, v_ref, m_ref):
    group, t, _ = x_ref.shape
    proj = _dot(x_ref[...].reshape(group * t, D_MODEL), w_ref[0])
    c = c_ref[...]
    sa = sa_ref[...]
    sb = sb_ref[...]

    def rope(a):
        return (a * c + pltpu.roll(a, ROT_DIM // 2, 1) * sa
                + pltpu.roll(a, HEAD_W - ROT_DIM // 2, 1) * sb)

    for g in range(group):
        own = slice(g * t, (g + 1) * t)
        for h in range(DIFF_HEADS):
            col = h * HEAD_W
            q = rope(proj[own, col:col + HEAD_W]) * Q_SCALE
            q_ref[g, h] = q.astype(q_ref.dtype)
            k_ref[g, h] = rope(proj[own, MIX_W + col:MIX_W + col + HEAD_W])
            v_ref[g, h] = proj[own, 2 * MIX_W + col:2 * MIX_W + col + HEAD_W]
        m = _memory_attention(proj[own, 3 * MIX_W:], mk_ref[0, g], mv_ref[0, g])
        m_ref[g] = m.astype(m_ref.dtype)


def _diff_project(x, pos, mk_t, mv_t, layer, w_bf, w_layer, *, t_step, group):
    b, s, _ = x.shape
    half = ROT_DIM // 2
    lane = jnp.arange(HEAD_W, dtype=jnp.int32) % DIFF_HD
    inv_freq = jnp.exp(-math.log(ROPE_THETA) * (lane % half).astype(F32) * 2.0 / ROT_DIM)
    ang = pos.astype(F32)[:, None] * inv_freq[None, :]
    cos, sin = jnp.cos(ang), jnp.sin(ang)
    tables = [jnp.where(lane < ROT_DIM, cos, 1.0),
              jnp.where((lane >= half) & (lane < ROT_DIM), sin, 0.0),
              jnp.where(lane < half, -sin, 0.0)]
    cols = w_bf.shape[-1]
    tab_spec = pl.BlockSpec((t_step, HEAD_W), lambda i, j: (j, 0))
    tok_spec = lambda w: pl.BlockSpec((group, t_step, w), lambda i, j: (i, j, 0))
    head_spec = pl.BlockSpec((group, DIFF_HEADS, t_step, HEAD_W), lambda i, j: (i, 0, j, 0))
    mem_spec = pl.BlockSpec((1, group, MEM_W, N_MEM), lambda i, j: (layer, i, 0, 0))
    heads = lambda dt: jax.ShapeDtypeStruct((b, DIFF_HEADS, s, HEAD_W), dt)
    return pl.pallas_call(
        _diff_proj_kernel,
        grid=(b // group, s // t_step),
        in_specs=[tok_spec(D_MODEL), _layer_spec((D_MODEL, cols), w_layer), tab_spec, tab_spec,
                  tab_spec, mem_spec, mem_spec],
        out_specs=[head_spec, head_spec, head_spec, tok_spec(MEM_W)],
        out_shape=[
            heads(BF16),
            heads(F32),
            heads(F32),
            jax.ShapeDtypeStruct((b, s, MEM_W), BF16),
        ],
        name="diff_proj",
        compiler_params=_params("arbitrary", "arbitrary"),
    )(x, w_bf, *tables, mk_t, mv_t)


def _diff_lambda(lq1_ref, lk1_ref, lq2_ref, lk2_ref, lam_init):
    a = jnp.sum(lq1_ref[...] * lk1_ref[...], axis=-1, keepdims=True)
    b = jnp.sum(lq2_ref[...] * lk2_ref[...], axis=-1, keepdims=True)
    return jnp.exp(a) - jnp.exp(b) + lam_init


def _diff_finish(o, g_ref, lam_init):
    ms = jnp.mean(o * o, axis=-1, keepdims=True)
    return o * lax.rsqrt(ms + LN_EPS) * g_ref[...] * (1.0 - lam_init)


def _split_heads(k):
    lane = lax.broadcasted_iota(jnp.int32, k.shape, 1)
    first = lane < DIFF_HD
    return jnp.where(first, k, 0.0).astype(BF16), jnp.where(first, 0.0, k).astype(BF16)


def _causal_tiles(n_k):
    return [(j, int(j == g), int(j == 0), int(j == g), g) for g in range(n_k) for j in range(g + 1)]


def _diff_attn_prompt_kernel(tab_ref, q_ref, k_ref, v_ref, lq1_ref, lk1_ref, lq2_ref, lk2_ref, g_ref,
                             o_ref, kx_ref, vt_ref, qt_ref, mask_ref, s_ref, p_ref, acc_ref, *,
                             tq, tk, heads, lam_init):
    seq = q_ref.shape[2]
    n_q, n_k = seq // tq, seq // tk
    q_per_k = tk // tq
    tiles = _causal_tiles(n_k)
    n_tiles = len(tiles)
    chunks_q, chunks_k = tq // CHUNK, tk // CHUNK
    streams = [(hd, qs, mp) for hd in range(heads) for qs in range(q_per_k) for mp in range(2)]

    lane = lax.broadcasted_iota(jnp.int32, (seq, HEAD_W), 1)
    row_chunk = (lax.broadcasted_iota(jnp.int32, (seq, HEAD_W), 0) // CHUNK) % chunks_k
    chunk_one_hot = jnp.where(lane == row_chunk, 1.0, 0.0).astype(BF16)
    first = lane < DIFF_HD
    ones_row = jnp.where(lax.broadcasted_iota(jnp.int32, (SUM_ROWS, tk), 0) == 0, 1.0, 0.0).astype(BF16)
    for hd in range(heads):
        k = k_ref[0, hd]
        kx_ref[hd, 0, :, :HEAD_W] = jnp.where(first, k, 0.0).astype(BF16)
        kx_ref[hd, 1, :, :HEAD_W] = jnp.where(first, 0.0, k).astype(BF16)
        kx_ref[hd, 0, :, HEAD_W:] = chunk_one_hot
        kx_ref[hd, 1, :, HEAD_W:] = chunk_one_hot
        for t in range(n_k):
            vt_ref[hd, t, :HEAD_W, :] = v_ref[0, hd, t * tk:(t + 1) * tk, :].T.astype(BF16)
            vt_ref[hd, t, HEAD_W:, :] = ones_row
        for qi in range(n_q):
            qt_ref[hd, qi] = q_ref[0, hd, qi * tq:(qi + 1) * tq, :].astype(F32).T.astype(BF16)
    bias_row = lax.broadcasted_iota(jnp.int32, (HEAD_W, tq), 0)
    bias_col = lax.broadcasted_iota(jnp.int32, (HEAD_W, tq), 1) // CHUNK
    mask_ref[0] = jnp.zeros((HEAD_W, tq), BF16)
    for qs in range(q_per_k):
        hidden = (bias_row < chunks_k) & (bias_row > bias_col + chunks_q * qs)
        mask_ref[1 + qs] = jnp.where(hidden, NEG_INF, 0.0).astype(BF16)
    lam = _diff_lambda(lq1_ref, lk1_ref, lq2_ref, lk2_ref, lam_init)

    def tile_of(i):
        if isinstance(i, int):
            j, diag, is_first, is_last, grp = tiles[i]
            return j, diag, bool(is_first), bool(is_last), grp
        j, diag, is_first, is_last, grp = (tab_ref[i, f] for f in range(5))
        return j, diag, is_first == 1, is_last == 1, grp

    def stage_a(i, slot):
        j, diag, _, _, grp = tile_of(i)
        base = j * tk if isinstance(j, int) else pl.multiple_of(j * tk, tk)
        tops = []
        for hd, qs, mp in streams:
            q_op = jnp.concatenate([qt_ref[hd, grp * q_per_k + qs], mask_ref[diag * (1 + qs)]], axis=0)
            s = _dot(kx_ref[hd, mp, pl.ds(base, tk), :], q_op)
            s_ref[hd, slot, qs, mp] = s
            tops.append(jnp.max(s, axis=0, keepdims=True))
        return tuple(tops)

    def stage_b(i, slot, m, top):
        _, _, is_first, _, _ = tile_of(i)
        out = []
        for n, (hd, qs, mp) in enumerate(streams):
            m_prev = jnp.where(is_first, NEG_INF, m[n])
            m_new = jnp.maximum(m_prev, top[n])
            scale = jnp.exp2(m_prev - m_new)
            p_ref[hd, slot, qs, mp] = jnp.exp2(s_ref[hd, slot, qs, mp] - m_new).astype(BF16)
            out.append((m_new, scale))
        return tuple(zip(*out))

    def stage_c(i, slot, scale):
        j = tile_of(i)[0]
        for n, (hd, qs, mp) in enumerate(streams):
            acc_ref[hd, qs, mp] = (scale[n] * acc_ref[hd, qs, mp]
                                   + _dot(vt_ref[hd, j], p_ref[hd, slot, qs, mp]))

    def finish_if_last(i):
        _, _, _, is_last, grp = tile_of(i)

        def finish():
            for hd in range(heads):
                for qs in range(q_per_k):
                    w1 = 1.0 / acc_ref[hd, qs, 0, HEAD_W:HEAD_W + 1, :]
                    w2 = lam / acc_ref[hd, qs, 1, HEAD_W:HEAD_W + 1, :]
                    o_t = acc_ref[hd, qs, 0, :HEAD_W, :] * w1 - acc_ref[hd, qs, 1, :HEAD_W, :] * w2
                    ms = jnp.mean(o_t * o_t, axis=0, keepdims=True)
                    o = (o_t * (lax.rsqrt(ms + LN_EPS) * (1.0 - lam_init))).T * g_ref[...]
                    start = (grp * q_per_k + qs) * tq
                    rows = pl.ds(start if isinstance(start, int) else pl.multiple_of(start, tq), tq)
                    o_ref[0, rows, hd * HEAD_W:(hd + 1) * HEAD_W] = o.astype(o_ref.dtype)

        if isinstance(is_last, bool):
            if is_last:
                finish()
        else:
            pl.when(is_last)(finish)

    def iteration(i, slot, carry, a=True, b=True, c=True):
        m, top, scale = carry
        new_scale = scale
        if b:
            m, new_scale = stage_b(i - 1, 1 - slot, m, top)
        if a:
            top = stage_a(i, slot)
        if c:
            stage_c(i - 2, slot, scale)
            finish_if_last(i - 2)
        return m, top, new_scale

    def pair(n, carry):
        i = 2 + 2 * n
        return iteration(i + 1, 1, iteration(i, 0, carry))

    zero = (jnp.zeros((1, tq), F32),) * len(streams)
    acc_ref[...] = jnp.zeros(acc_ref.shape, F32)
    carry = (zero, zero, zero)
    carry = iteration(0, 0, carry, b=False, c=False)
    carry = iteration(1, 1, carry, c=False)
    carry = lax.fori_loop(0, (n_tiles - 2) // 2, pair, carry)
    carry = iteration(n_tiles, 0, carry, a=False)
    iteration(n_tiles + 1, 1, carry, a=False, b=False)


def _diff_attn_prompt(q, k, v, lam_vecs, subln_g, lam_init, *, tq, tk, heads):
    b, _, s, _ = q.shape
    q_per_k = tk // tq
    rows = _causal_tiles(s // tk)
    assert len(rows) % 2 == 0
    table = jnp.asarray(np.array(rows, np.int32))
    kern = functools.partial(_diff_attn_prompt_kernel, tq=tq, tk=tk, heads=heads, lam_init=lam_init)
    const = lambda shape: pl.BlockSpec(shape, lambda i, h, tab: (0,) * len(shape))
    head_spec = pl.BlockSpec((1, heads, s, HEAD_W), lambda i, h, tab: (i, h, 0, 0))
    return pl.pallas_call(
        kern,
        grid_spec=pltpu.PrefetchScalarGridSpec(
            num_scalar_prefetch=1,
            grid=(b, DIFF_HEADS // heads),
            in_specs=[head_spec, head_spec, head_spec] + [const((1, DIFF_HD))] * 4 + [const((1, HEAD_W))],
            out_specs=pl.BlockSpec((1, s, heads * HEAD_W), lambda i, h, tab: (i, 0, h)),
            scratch_shapes=[
                pltpu.VMEM((heads, 2, s, 2 * HEAD_W), BF16),
                pltpu.VMEM((heads, s // tk, HEAD_W + SUM_ROWS, tk), BF16),
                pltpu.VMEM((heads, s // tq, HEAD_W, tq), BF16),
                pltpu.VMEM((1 + tk // tq, HEAD_W, tq), BF16),
                pltpu.VMEM((heads, 2, q_per_k, 2, tk, tq), F32),
                pltpu.VMEM((heads, 2, q_per_k, 2, tk, tq), BF16),
                pltpu.VMEM((heads, q_per_k, 2, HEAD_W + SUM_ROWS, tq), F32),
            ]),
        out_shape=jax.ShapeDtypeStruct((b, s, MIX_W), BF16),
        name="diff_attn_prompt",
        compiler_params=_params("arbitrary", "arbitrary"),
    )(table, q, k, v, *lam_vecs, subln_g.reshape(1, HEAD_W))


def _diff_attn_sample_kernel(q_ref, kc_ref, vc_ref, kn_ref, vn_ref, lq1_ref, lk1_ref, lq2_ref, lk2_ref,
                             g_ref, o_ref, *, lam_init):
    lam = _diff_lambda(lq1_ref, lk1_ref, lq2_ref, lk2_ref, lam_init)
    for h in range(q_ref.shape[1]):
        q = q_ref[0, h]
        s = q.shape[0]
        q1, q2 = _split_heads(q)
        qq = jnp.concatenate([q1, q2], axis=0)
        sc = _dot_nt(qq, kc_ref[0, h])
        sn = _dot_nt(qq, kn_ref[0, h])
        m = jnp.maximum(jnp.max(sc, axis=-1, keepdims=True), jnp.max(sn, axis=-1, keepdims=True))
        pc = jnp.exp2(sc - m)
        pn = jnp.exp2(sn - m)
        l = jnp.sum(pc, axis=-1, keepdims=True) + jnp.sum(pn, axis=-1, keepdims=True)
        o2 = (_dot(pc, vc_ref[0, h]) + _dot(pn, vn_ref[0, h])) / l
        o = o2[:s] - lam * o2[s:]
        o_ref[0, :, h * HEAD_W:(h + 1) * HEAD_W] = _diff_finish(o, g_ref, lam_init).astype(o_ref.dtype)


def _diff_attn_sample(q, k_cache, v_cache, k_new, v_new, lam_vecs, subln_g, lam_init):
    b, heads, s, _ = q.shape
    past = k_cache.shape[2]
    kern = functools.partial(_diff_attn_sample_kernel, lam_init=lam_init)
    vec_spec = _const_spec((1, DIFF_HD))
    new_spec = pl.BlockSpec((1, heads, s, HEAD_W), lambda i: (i, 0, 0, 0))
    old_spec = pl.BlockSpec((1, heads, past, HEAD_W), lambda i: (i, 0, 0, 0))
    return pl.pallas_call(
        kern,
        grid=(b,),
        in_specs=[new_spec, old_spec, old_spec, new_spec, new_spec, vec_spec, vec_spec, vec_spec,
                  vec_spec, _const_spec((1, HEAD_W))],
        out_specs=pl.BlockSpec((1, s, heads * HEAD_W), lambda i: (i, 0, 0)),
        out_shape=jax.ShapeDtypeStruct((b, s, MIX_W), BF16),
        name="diff_attn_sample",
        compiler_params=_params("arbitrary"),
    )(q, k_cache, v_cache, k_new, v_new, *lam_vecs, subln_g.reshape(1, HEAD_W))


def _post_kernel(x_ref, o_ref, m_ref, wo_ref, g1_ref, b1_ref, wg_ref, wu_ref, wd_ref, g2_ref, b2_ref,
                 y_ref, h_ref, *, ff_chunk, row_groups):
    tm = x_ref.shape[0]
    groups = [slice(r * tm // row_groups, (r + 1) * tm // row_groups) for r in range(row_groups)]
    x1, x1b = [], []
    for rows in groups:
        mixed = (_dot(o_ref[rows, :], wo_ref[0, :MIX_W, :])
                 + _dot(m_ref[rows, :], wo_ref[0, MIX_W:, :]))
        x1.append(_layer_norm_rows(ALPHA * x_ref[rows, :] + mixed, g1_ref[0], b1_ref[0]))
        x1b.append(x1[-1].astype(BF16))
    for c in range(D_FF // ff_chunk):
        cols = slice(c * ff_chunk, (c + 1) * ff_chunk)
        for r, rows in enumerate(groups):
            gate = _dot(x1b[r], wg_ref[0, :, cols])
            up = _dot(x1b[r], wu_ref[0, :, cols])
            h_ref[rows, cols] = (gate / (1.0 + jnp.exp(-gate)) * up).astype(BF16)
    for r, rows in enumerate(groups):
        ff = _dot(h_ref[rows, :], wd_ref[0])
        y_ref[rows, :] = _layer_norm_rows(ALPHA * x1[r] + ff, g2_ref[0], b2_ref[0])


def _post(x, o, m, layer, wo, g1, b1, wg, wu, wd, g2, b2, *, tm, ff_chunk, row_groups):
    n = x.shape[0]
    row = lambda w: pl.BlockSpec((tm, w), lambda i: (i, 0))
    vec = _layer_spec((1, D_MODEL), layer)
    kern = functools.partial(_post_kernel, ff_chunk=ff_chunk, row_groups=row_groups)
    r1 = lambda a: a.reshape(DEPTH, 1, D_MODEL)
    return pl.pallas_call(
        kern,
        grid=(n // tm,),
        in_specs=[row(D_MODEL), row(MIX_W), row(MEM_W), _layer_spec((D_MODEL, D_MODEL), layer), vec, vec,
                  _layer_spec((D_MODEL, D_FF), layer), _layer_spec((D_MODEL, D_FF), layer),
                  _layer_spec((D_FF, D_MODEL), layer), vec, vec],
        out_specs=row(D_MODEL),
        out_shape=jax.ShapeDtypeStruct((n, D_MODEL), F32),
        scratch_shapes=[pltpu.VMEM((tm, D_FF), BF16)],
        name="post_mixer",
        compiler_params=_params("arbitrary"),
    )(x, o, m, wo, r1(g1), r1(b1), wg, wu, wd, r1(g2), r1(b2))


def kernel(x_prompt, x_sample, mem_prompt, cache_ret_state, cache_diff_k, cache_diff_v, cache_mem_k,
           cache_mem_v, ret_w_in, ret_gn_g, diff_w_in, diff_lambda_q1, diff_lambda_k1, diff_lambda_q2,
           diff_lambda_k2, diff_subln_g, w_mem_kv, w_o, ln1_g, ln1_b, w_gate, w_up, w_down, ln2_g, ln2_b):
    bp, sp, _ = x_prompt.shape
    bs, ss, _ = x_sample.shape
    pos_p = jnp.arange(sp)
    pos_s = PAST_LEN + jnp.arange(ss)
    bf = lambda a: a.astype(BF16)

    mem_k_p, mem_v_p = _mem_kv(mem_prompt, bf(w_mem_kv.transpose(0, 2, 1)))
    mem_t = lambda a: a.transpose(0, 1, 3, 4, 2).reshape(DEPTH, bs, MEM_W, N_MEM)
    mem_k_s, mem_v_s = mem_t(cache_mem_k), mem_t(cache_mem_v)

    ret_w, diff_w = bf(ret_w_in), bf(diff_w_in)
    wo, wg, wu, wd = bf(w_o), bf(w_gate), bf(w_up), bf(w_down)

    xp, xs = x_prompt, x_sample
    ret_p, ret_s, dkp, dvp, dks, dvs = [], [], [], [], [], []
    to_seq_major = lambda a: a.transpose(0, 2, 1, 3)
    for i in range(DEPTH):
        j = i // 2
        if i % 2 == 0:
            r0 = jnp.zeros((bp, RET_HEADS, RET_HD, RET_HD), F32)
            op, mp, rp = _ret_mixer(xp, pos_p, r0, mem_k_p, mem_v_p, i, ret_w, j, ret_gn_g[j],
                                    t_step=512, tile=256, group=1)
            os_, ms, rs = _ret_mixer(xs, pos_s, cache_ret_state[j], mem_k_s, mem_v_s, i, ret_w, j,
                                     ret_gn_g[j], t_step=ss, tile=ss, group=SAMPLE_GROUP)
            ret_p.append(rp)
            ret_s.append(rs)
        else:
            lam_init = _lambda_init(i)
            lam_vecs = [a[j].reshape(1, DIFF_HD) for a in
                        (diff_lambda_q1, diff_lambda_k1, diff_lambda_q2, diff_lambda_k2)]
            q, k, v, mp = _diff_project(xp, pos_p, mem_k_p, mem_v_p, i, diff_w, j, t_step=512, group=1)
            op = _diff_attn_prompt(q, k, v, lam_vecs, diff_subln_g[j], lam_init, tq=256, tk=512, heads=2)
            dkp.append(to_seq_major(k))
            dvp.append(to_seq_major(v))
            q, k, v, ms = _diff_project(xs, pos_s, mem_k_s, mem_v_s, i, diff_w, j, t_step=ss,
                                        group=SAMPLE_GROUP)
            os_ = _diff_attn_sample(q, cache_diff_k[j].transpose(0, 2, 1, 3),
                                    cache_diff_v[j].transpose(0, 2, 1, 3), k, v, lam_vecs,
                                    diff_subln_g[j], lam_init)
            dks.append(to_seq_major(k))
            dvs.append(to_seq_major(v))
        post = functools.partial(_post, layer=i, wo=wo, g1=ln1_g, b1=ln1_b, wg=wg, wu=wu, wd=wd,
                                 g2=ln2_g, b2=ln2_b, tm=1024, ff_chunk=256, row_groups=4)
        xp = post(xp.reshape(bp * sp, D_MODEL), op.reshape(bp * sp, MIX_W),
                  mp.reshape(bp * sp, MEM_W)).reshape(bp, sp, D_MODEL)
        xs = post(xs.reshape(bs * ss, D_MODEL), os_.reshape(bs * ss, MIX_W),
                  ms.reshape(bs * ss, MEM_W)).reshape(bs, ss, D_MODEL)

    mem_out = lambda a: a.reshape(DEPTH, bp, MEM_HEADS, MEM_HD, N_MEM).transpose(0, 1, 4, 2, 3)
    return (xp, xs, jnp.stack(ret_p), jnp.stack(ret_s), jnp.stack(dkp), jnp.stack(dvp),
            jnp.stack(dks), jnp.stack(dvs), mem_out(mem_k_p), mem_out(mem_v_p))
```

```python
import functools
import math

import jax
import jax.numpy as jnp
import numpy as np
from jax import lax
from jax.experimental import pallas as pl
from jax.experimental.pallas import tpu as pltpu

D_MODEL = 1024
DEPTH = 2
PAST_LEN = 1024
CHUNK = 64
N_MEM = 256
MEM_HEADS = 4
MEM_HD = 64
MEM_W = MEM_HEADS * MEM_HD
MIX_W = D_MODEL - MEM_W
RET_HEADS = 6
RET_HD = MIX_W // RET_HEADS
RET_THETA = 10000.0
DIFF_HEADS = 6
DIFF_HD = MIX_W // (2 * DIFF_HEADS)
ROPE_THETA = 500000.0
ROT_DIM = DIFF_HD // 4
D_FF = -(-8 * D_MODEL // (3 * 256)) * 256
ALPHA = (2 * DEPTH) ** 0.25
LN_EPS = 1e-5
NEG_INF = -1e30

Q_SCALE = DIFF_HD ** -0.5 * math.log2(math.e)
HEAD_W = 128
SUM_ROWS = 16

VMEM_LIMIT = 56 * 1024 * 1024
MIXER_ROWS = 512
RET_TILE = 4 * CHUNK
SAMPLE_GROUP = 4
ATTN_Q_TILE = 256
ATTN_K_TILE = 512
ATTN_HEADS = 2
POST_ROWS = 1024
POST_ROW_GROUPS = 4
FF_CHUNK = 256

F32 = jnp.float32
BF16 = jnp.bfloat16


def _lambda_init(layer_idx):
    return 0.8 - 0.6 * math.exp(-0.3 * layer_idx)


def _dot(a, b):
    return jnp.dot(a.astype(BF16), b.astype(BF16), preferred_element_type=F32)


def _dot_nt(a, b):
    return lax.dot_general(a.astype(BF16), b.astype(BF16), (((1,), (1,)), ((), ())),
                           preferred_element_type=F32)


def _dot_tn(a, b):
    return lax.dot_general(a.astype(BF16), b.astype(BF16), (((0,), (0,)), ((), ())),
                           preferred_element_type=F32)


def _fixed_spec(shape, index):
    return pl.BlockSpec(shape, lambda *_: index, pipeline_mode=pl.Buffered(1))


def _const_spec(shape):
    return _fixed_spec(shape, (0,) * len(shape))


def _layer_spec(shape, layer):
    return _fixed_spec((1,) + shape, (layer,) + (0,) * len(shape))


def _params(*semantics):
    return pltpu.CompilerParams(dimension_semantics=semantics, vmem_limit_bytes=VMEM_LIMIT)


def _layer_norm_rows(y, g, b):
    mu = jnp.mean(y, axis=-1, keepdims=True)
    d = y - mu
    var = jnp.mean(d * d, axis=-1, keepdims=True)
    return d * lax.rsqrt(var + LN_EPS) * g + b


def _memory_attention(mq, mk_t, mv_t):
    row = lax.broadcasted_iota(jnp.int32, mk_t.shape, 0)
    mqb = (mq * (MEM_HD ** -0.5 * math.log2(math.e))).astype(BF16)
    out = None
    for h in range(MEM_HEADS):
        sel = (row >= h * MEM_HD) & (row < (h + 1) * MEM_HD)
        s = _dot(mqb, jnp.where(sel, mk_t, 0.0))
        p = jnp.exp2(s - jnp.max(s, axis=-1, keepdims=True))
        o = _dot_nt(p, jnp.where(sel, mv_t, 0.0)) * (1.0 / jnp.sum(p, axis=-1, keepdims=True))
        out = o if out is None else out + o
    return out


def _mem_kv_kernel(x_ref, w_ref, k_ref, v_ref):
    xb = x_ref[0].astype(BF16)
    for layer in range(DEPTH):
        kv_t = _dot_nt(w_ref[layer], xb)
        k_ref[layer, 0] = kv_t[:MEM_W]
        v_ref[layer, 0] = kv_t[MEM_W:]


def _mem_kv(mem, w_t_bf):
    b = mem.shape[0]
    out = jax.ShapeDtypeStruct((DEPTH, b, MEM_W, N_MEM), F32)
    return pl.pallas_call(
        _mem_kv_kernel,
        grid=(b,),
        in_specs=[pl.BlockSpec((1, N_MEM, D_MODEL), lambda i: (i, 0, 0)),
                  _const_spec((DEPTH, 2 * MEM_W, D_MODEL))],
        out_specs=[pl.BlockSpec((DEPTH, 1, MEM_W, N_MEM), lambda i: (0, i, 0, 0))] * 2,
        out_shape=[out, out],
        name="mem_kv",
        compiler_params=_params("arbitrary"),
    )(mem, w_t_bf)


def _ret_tables(tile):
    h = np.arange(RET_HEADS, dtype=np.float64)
    log_g = np.log(1.0 - np.exp2(-5.0 - h))
    idx = np.arange(tile, dtype=np.float64)
    dist = np.abs(idx[:, None] - idx[None, :])
    visible = (idx[None, :] // CHUNK) <= (idx[:, None] // CHUNK)
    decay = np.where(visible[None], np.exp(dist[None] * log_g[:, None, None]), 0.0)
    xi = np.exp((idx + 1.0)[None, :] * log_g[:, None])[:, :, None]
    zeta = np.exp((tile - 1.0 - idx)[None, :] * log_g[:, None])[:, :, None]
    g_tile = np.exp(tile * log_g)
    f = lambda a: jnp.asarray(a, F32)
    return f(decay), f(xi), f(zeta), [float(g) for g in g_tile]


def _ret_kernel(x_ref, w_ref, cos_ref, sin_ref, r0_ref, mk_ref, mv_ref, gn_ref, decay_ref, xi_ref,
                zeta_ref, o_ref, m_ref, r_ref, state_ref, *, tile, g_tile):
    step = pl.program_id(1)

    @pl.when(step == 0)
    def _():
        state_ref[...] = r0_ref[...]

    group, t, _ = x_ref.shape
    proj = _dot(x_ref[...].reshape(group * t, D_MODEL), w_ref[0])
    cos = cos_ref[...]
    sin = sin_ref[...]

    def rope(a):
        return a * cos + pltpu.roll(a, RET_HD // 2, 1) * sin

    for g in range(group):
        own = slice(g * t, (g + 1) * t)
        for h in range(RET_HEADS):
            col = h * HEAD_W
            q = rope(proj[own, col:col + HEAD_W])
            k = rope(proj[own, MIX_W + col:MIX_W + col + HEAD_W]) * RET_HD ** -0.5
            v = proj[own, 2 * MIX_W + col:2 * MIX_W + col + HEAD_W]
            gate = proj[own, 3 * MIX_W + col:3 * MIX_W + col + HEAD_W]
            outs = []
            r = state_ref[g, h]
            for c in range(t // tile):
                rows = slice(c * tile, (c + 1) * tile)
                qc, kc, vc = q[rows], k[rows], v[rows]
                inner = _dot_nt(qc, kc) * decay_ref[h]
                outs.append(_dot(inner, vc) + _dot(qc, r) * xi_ref[h])
                r = g_tile[h] * r + _dot_tn(kc * zeta_ref[h], vc)
            state_ref[g, h] = r
            o = outs[0] if len(outs) == 1 else jnp.concatenate(outs, axis=0)
            mu = jnp.mean(o, axis=-1, keepdims=True)
            d = o - mu
            var = jnp.mean(d * d, axis=-1, keepdims=True)
            o = d * lax.rsqrt(var + LN_EPS) * gn_ref[:, col:col + HEAD_W]
            o = o * (gate / (1.0 + jnp.exp(-gate)))
            o_ref[g, :, col:col + HEAD_W] = o.astype(o_ref.dtype)

        m = _memory_attention(proj[own, 4 * MIX_W:], mk_ref[0, g], mv_ref[0, g])
        m_ref[g] = m.astype(m_ref.dtype)

    @pl.when(step == pl.num_programs(1) - 1)
    def _():
        r_ref[...] = state_ref[...]


def _ret_mixer(x, pos, r0, mk_t, mv_t, layer, w_bf, w_layer, gn_g, *, t_step, tile, group):
    b, s, _ = x.shape
    half = RET_HD // 2
    lane_freq = jnp.arange(HEAD_W, dtype=jnp.int32) % half
    inv_freq = jnp.exp(-math.log(RET_THETA) * lane_freq.astype(F32) * 2.0 / RET_HD)
    ang = pos.astype(F32)[:, None] * inv_freq[None, :]
    cos = jnp.cos(ang)
    sin = jnp.where(jnp.arange(HEAD_W) < half, -jnp.sin(ang), jnp.sin(ang))
    decay, xi, zeta, g_tile = _ret_tables(tile)
    cols = w_bf.shape[-1]
    kern = functools.partial(_ret_kernel, tile=tile, g_tile=g_tile)
    mem_spec = pl.BlockSpec((1, group, MEM_W, N_MEM), lambda i, j: (layer, i, 0, 0))
    state_spec = pl.BlockSpec((group, RET_HEADS, RET_HD, RET_HD), lambda i, j: (i, 0, 0, 0))
    return pl.pallas_call(
        kern,
        grid=(b // group, s // t_step),
        in_specs=[
            pl.BlockSpec((group, t_step, D_MODEL), lambda i, j: (i, j, 0)),
            _layer_spec((D_MODEL, cols), w_layer),
            pl.BlockSpec((t_step, HEAD_W), lambda i, j: (j, 0)),
            pl.BlockSpec((t_step, HEAD_W), lambda i, j: (j, 0)),
            state_spec,
            mem_spec, mem_spec,
            _const_spec((1, MIX_W)),
            _const_spec((RET_HEADS, tile, tile)),
            _const_spec((RET_HEADS, tile, 1)),
            _const_spec((RET_HEADS, tile, 1)),
        ],
        out_specs=[
            pl.BlockSpec((group, t_step, MIX_W), lambda i, j: (i, j, 0)),
            pl.BlockSpec((group, t_step, MEM_W), lambda i, j: (i, j, 0)),
            state_spec,
        ],
        out_shape=[
            jax.ShapeDtypeStruct((b, s, MIX_W), BF16),
            jax.ShapeDtypeStruct((b, s, MEM_W), BF16),
            jax.ShapeDtypeStruct((b, RET_HEADS, RET_HD, RET_HD), F32),
        ],
        scratch_shapes=[pltpu.VMEM((group, RET_HEADS, RET_HD, RET_HD), F32)],
        name="ret_mixer",
        compiler_params=_params("arbitrary", "arbitrary"),
    )(x, w_bf, cos, sin, r0, mk_t, mv_t, gn_g.reshape(1, MIX_W), decay, xi, zeta)


def _diff_proj_kernel(x_ref, w_ref, c_ref, sa_ref, sb_ref, mk_ref, mv_ref, q_ref, k_ref, v_ref, m_ref):
    group, t, _ = x_ref.shape
    proj = _dot(x_ref[...].reshape(group * t, D_MODEL), w_ref[0])
    c = c_ref[...]
    sa = sa_ref[...]
    sb = sb_ref[...]

    def rope(a):
        return (a * c + pltpu.roll(a, ROT_DIM // 2, 1) * sa
                + pltpu.roll(a, HEAD_W - ROT_DIM // 2, 1) * sb)

    for g in range(group):
        own = slice(g * t, (g + 1) * t)
        for h in range(DIFF_HEADS):
            col = h * HEAD_W
            q = rope(proj[own, col:col + HEAD_W]) * Q_SCALE
            q_ref[g, h] = q.astype(q_ref.dtype)
            k_ref[g, h] = rope(proj[own, MIX_W + col:MIX_W + col + HEAD_W])
            v_ref[g, h] = proj[own, 2 * MIX_W + col:2 * MIX_W + col + HEAD_W]
        m = _memory_attention(proj[own, 3 * MIX_W:], mk_ref[0, g], mv_ref[0, g])
        m_ref[g] = m.astype(m_ref.dtype)


def _diff_project(x, pos, mk_t, mv_t, layer, w_bf, w_layer, *, t_step, group):
    b, s, _ = x.shape
    half = ROT_DIM // 2
    lane = jnp.arange(HEAD_W, dtype=jnp.int32) % DIFF_HD
    inv_freq = jnp.exp(-math.log(ROPE_THETA) * (lane % half).astype(F32) * 2.0 / ROT_DIM)
    ang = pos.astype(F32)[:, None] * inv_freq[None, :]
    cos, sin = jnp.cos(ang), jnp.sin(ang)
    tables = [jnp.where(lane < ROT_DIM, cos, 1.0),
              jnp.where((lane >= half) & (lane < ROT_DIM), sin, 0.0),
              jnp.where(lane < half, -sin, 0.0)]
    cols = w_bf.shape[-1]
    tab_spec = pl.BlockSpec((t_step, HEAD_W), lambda i, j: (j, 0))
    tok_spec = lambda w: pl.BlockSpec((group, t_step, w), lambda i, j: (i, j, 0))
    head_spec = pl.BlockSpec((group, DIFF_HEADS, t_step, HEAD_W), lambda i, j: (i, 0, j, 0))
    mem_spec = pl.BlockSpec((1, group, MEM_W, N_MEM), lambda i, j: (layer, i, 0, 0))
    heads = lambda dt: jax.ShapeDtypeStruct((b, DIFF_HEADS, s, HEAD_W), dt)
    return pl.pallas_call(
        _diff_proj_kernel,
        grid=(b // group, s // t_step),
        in_specs=[tok_spec(D_MODEL), _layer_spec((D_MODEL, cols), w_layer), tab_spec, tab_spec,
                  tab_spec, mem_spec, mem_spec],
        out_specs=[head_spec, head_spec, head_spec, tok_spec(MEM_W)],
        out_shape=[
            heads(BF16),
            heads(F32),
            heads(F32),
            jax.ShapeDtypeStruct((b, s, MEM_W), BF16),
        ],
        name="diff_proj",
        compiler_params=_params("arbitrary", "arbitrary"),
    )(x, w_bf, *tables, mk_t, mv_t)


def _diff_lambda(lq1_ref, lk1_ref, lq2_ref, lk2_ref, lam_init):
    a = jnp.sum(lq1_ref[...] * lk1_ref[...], axis=-1, keepdims=True)
    b = jnp.sum(lq2_ref[...] * lk2_ref[...], axis=-1, keepdims=True)
    return jnp.exp(a) - jnp.exp(b) + lam_init


def _diff_finish(o, g_ref, lam_init):
    ms = jnp.mean(o * o, axis=-1, keepdims=True)
    return o * lax.rsqrt(ms + LN_EPS) * g_ref[...] * (1.0 - lam_init)


def _split_heads(k):
    lane = lax.broadcasted_iota(jnp.int32, k.shape, 1)
    first = lane < DIFF_HD
    return jnp.where(first, k, 0.0).astype(BF16), jnp.where(first, 0.0, k).astype(BF16)


def _causal_tiles(n_k):
    return [(j, int(j == g), int(j == 0), int(j == g), g) for g in range(n_k) for j in range(g + 1)]


def _diff_attn_prompt_kernel(tab_ref, q_ref, k_ref, v_ref, lq1_ref, lk1_ref, lq2_ref, lk2_ref, g_ref,
                             o_ref, kx_ref, vt_ref, qt_ref, mask_ref, s_ref, p_ref, acc_ref, *,
                             tq, tk, heads, lam_init):
    seq = q_ref.shape[2]
    n_q, n_k = seq // tq, seq // tk
    q_per_k = tk // tq
    tiles = _causal_tiles(n_k)
    n_tiles = len(tiles)
    chunks_q, chunks_k = tq // CHUNK, tk // CHUNK
    streams = [(hd, qs, mp) for hd in range(heads) for qs in range(q_per_k) for mp in range(2)]

    lane = lax.broadcasted_iota(jnp.int32, (seq, HEAD_W), 1)
    row_chunk = (lax.broadcasted_iota(jnp.int32, (seq, HEAD_W), 0) // CHUNK) % chunks_k
    chunk_one_hot = jnp.where(lane == row_chunk, 1.0, 0.0).astype(BF16)
    first = lane < DIFF_HD
    ones_row = jnp.where(lax.broadcasted_iota(jnp.int32, (SUM_ROWS, tk), 0) == 0, 1.0, 0.0).astype(BF16)
    for hd in range(heads):
        k = k_ref[0, hd]
        kx_ref[hd, 0, :, :HEAD_W] = jnp.where(first, k, 0.0).astype(BF16)
        kx_ref[hd, 1, :, :HEAD_W] = jnp.where(first, 0.0, k).astype(BF16)
        kx_ref[hd, 0, :, HEAD_W:] = chunk_one_hot
        kx_ref[hd, 1, :, HEAD_W:] = chunk_one_hot
        for t in range(n_k):
            vt_ref[hd, t, :HEAD_W, :] = v_ref[0, hd, t * tk:(t + 1) * tk, :].T.astype(BF16)
            vt_ref[hd, t, HEAD_W:, :] = ones_row
        for qi in range(n_q):
            qt_ref[hd, qi] = q_ref[0, hd, qi * tq:(qi + 1) * tq, :].astype(F32).T.astype(BF16)
    bias_row = lax.broadcasted_iota(jnp.int32, (HEAD_W, tq), 0)
    bias_col = lax.broadcasted_iota(jnp.int32, (HEAD_W, tq), 1) // CHUNK
    mask_ref[0] = jnp.zeros((HEAD_W, tq), BF16)
    for qs in range(q_per_k):
        hidden = (bias_row < chunks_k) & (bias_row > bias_col + chunks_q * qs)
        mask_ref[1 + qs] = jnp.where(hidden, NEG_INF, 0.0).astype(BF16)
    lam = _diff_lambda(lq1_ref, lk1_ref, lq2_ref, lk2_ref, lam_init)

    def tile_of(i):
        if isinstance(i, int):
            j, diag, is_first, is_last, grp = tiles[i]
            return j, diag, bool(is_first), bool(is_last), grp
        j, diag, is_first, is_last, grp = (tab_ref[i, f] for f in range(5))
        return j, diag, is_first == 1, is_last == 1, grp

    def stage_a(i, slot):
        j, diag, _, _, grp = tile_of(i)
        base = j * tk if isinstance(j, int) else pl.multiple_of(j * tk, tk)
        tops = []
        for hd, qs, mp in streams:
            q_op = jnp.concatenate([qt_ref[hd, grp * q_per_k + qs], mask_ref[diag * (1 + qs)]], axis=0)
            s = _dot(kx_ref[hd, mp, pl.ds(base, tk), :], q_op)
            s_ref[hd, slot, qs, mp] = s
            tops.append(jnp.max(s, axis=0, keepdims=True))
        return tuple(tops)

    def stage_b(i, slot, m, top):
        _, _, is_first, _, _ = tile_of(i)
        out = []
        for n, (hd, qs, mp) in enumerate(streams):
            m_prev = jnp.where(is_first, NEG_INF, m[n])
            m_new = jnp.maximum(m_prev, top[n])
            scale = jnp.exp2(m_prev - m_new)
            p_ref[hd, slot, qs, mp] = jnp.exp2(s_ref[hd, slot, qs, mp] - m_new).astype(BF16)
            out.append((m_new, scale))
        return tuple(zip(*out))

    def stage_c(i, slot, scale):
        j = tile_of(i)[0]
        for n, (hd, qs, mp) in enumerate(streams):
            acc_ref[hd, qs, mp] = (scale[n] * acc_ref[hd, qs, mp]
                                   + _dot(vt_ref[hd, j], p_ref[hd, slot, qs, mp]))

    def finish_if_last(i):
        _, _, _, is_last, grp = tile_of(i)

        def finish():
            for hd in range(heads):
                for qs in range(q_per_k):
                    w1 = 1.0 / acc_ref[hd, qs, 0, HEAD_W:HEAD_W + 1, :]
                    w2 = lam / acc_ref[hd, qs, 1, HEAD_W:HEAD_W + 1, :]
                    o_t = acc_ref[hd, qs, 0, :HEAD_W, :] * w1 - acc_ref[hd, qs, 1, :HEAD_W, :] * w2
                    ms = jnp.mean(o_t * o_t, axis=0, keepdims=True)
                    o = (o_t * (lax.rsqrt(ms + LN_EPS) * (1.0 - lam_init))).T * g_ref[...]
                    start = (grp * q_per_k + qs) * tq
                    rows = pl.ds(start if isinstance(start, int) else pl.multiple_of(start, tq), tq)
                    o_ref[0, rows, hd * HEAD_W:(hd + 1) * HEAD_W] = o.astype(o_ref.dtype)

        if isinstance(is_last, bool):
            if is_last:
                finish()
        else:
            pl.when(is_last)(finish)

    def iteration(i, slot, carry, a=True, b=True, c=True):
        m, top, scale = carry
        new_scale = scale
        if b:
            m, new_scale = stage_b(i - 1, 1 - slot, m, top)
        if a:
            top = stage_a(i, slot)
        if c:
            stage_c(i - 2, slot, scale)
            finish_if_last(i - 2)
        return m, top, new_scale

    def pair(n, carry):
        i = 2 + 2 * n
        return iteration(i + 1, 1, iteration(i, 0, carry))

    zero = (jnp.zeros((1, tq), F32),) * len(streams)
    acc_ref[...] = jnp.zeros(acc_ref.shape, F32)
    carry = (zero, zero, zero)
    carry = iteration(0, 0, carry, b=False, c=False)
    carry = iteration(1, 1, carry, c=False)
    carry = lax.fori_loop(0, (n_tiles - 2) // 2, pair, carry)
    carry = iteration(n_tiles, 0, carry, a=False)
    iteration(n_tiles + 1, 1, carry, a=False, b=False)


def _diff_attn_prompt(q, k, v, lam_vecs, subln_g, lam_init, *, tq, tk, heads):
    b, _, s, _ = q.shape
    q_per_k = tk // tq
    rows = _causal_tiles(s // tk)
    assert len(rows) % 2 == 0
    table = jnp.asarray(np.array(rows, np.int32))
    kern = functools.partial(_diff_attn_prompt_kernel, tq=tq, tk=tk, heads=heads, lam_init=lam_init)
    const = lambda shape: pl.BlockSpec(shape, lambda i, h, tab: (0,) * len(shape))
    head_spec = pl.BlockSpec((1, heads, s, HEAD_W), lambda i, h, tab: (i, h, 0, 0))
    return pl.pallas_call(
        kern,
        grid_spec=pltpu.PrefetchScalarGridSpec(
            num_scalar_prefetch=1,
            grid=(b, DIFF_HEADS // heads),
            in_specs=[head_spec, head_spec, head_spec] + [const((1, DIFF_HD))] * 4 + [const((1, HEAD_W))],
            out_specs=pl.BlockSpec((1, s, heads * HEAD_W), lambda i, h, tab: (i, 0, h)),
            scratch_shapes=[
                pltpu.VMEM((heads, 2, s, 2 * HEAD_W), BF16),
                pltpu.VMEM((heads, s // tk, HEAD_W + SUM_ROWS, tk), BF16),
                pltpu.VMEM((heads, s // tq, HEAD_W, tq), BF16),
                pltpu.VMEM((1 + tk // tq, HEAD_W, tq), BF16),
                pltpu.VMEM((heads, 2, q_per_k, 2, tk, tq), F32),
                pltpu.VMEM((heads, 2, q_per_k, 2, tk, tq), BF16),
                pltpu.VMEM((heads, q_per_k, 2, HEAD_W + SUM_ROWS, tq), F32),
            ]),
        out_shape=jax.ShapeDtypeStruct((b, s, MIX_W), BF16),
        name="diff_attn_prompt",
        compiler_params=_params("arbitrary", "arbitrary"),
    )(table, q, k, v, *lam_vecs, subln_g.reshape(1, HEAD_W))


def _diff_attn_sample_kernel(q_ref, kc_ref, vc_ref, kn_ref, vn_ref, lq1_ref, lk1_ref, lq2_ref, lk2_ref,
                             g_ref, o_ref, *, lam_init):
    lam = _diff_lambda(lq1_ref, lk1_ref, lq2_ref, lk2_ref, lam_init)
    for h in range(q_ref.shape[1]):
        q = q_ref[0, h]
        s = q.shape[0]
        q1, q2 = _split_heads(q)
        qq = jnp.concatenate([q1, q2], axis=0)
        sc = _dot_nt(qq, kc_ref[0, h])
        sn = _dot_nt(qq, kn_ref[0, h])
        m = jnp.maximum(jnp.max(sc, axis=-1, keepdims=True), jnp.max(sn, axis=-1, keepdims=True))
        pc = jnp.exp2(sc - m)
        pn = jnp.exp2(sn - m)
        l = jnp.sum(pc, axis=-1, keepdims=True) + jnp.sum(pn, axis=-1, keepdims=True)
        o2 = (_dot(pc, vc_ref[0, h]) + _dot(pn, vn_ref[0, h])) / l
        o = o2[:s] - lam * o2[s:]
        o_ref[0, :, h * HEAD_W:(h + 1) * HEAD_W] = _diff_finish(o, g_ref, lam_init).astype(o_ref.dtype)


def _diff_attn_sample(q, k_cache, v_cache, k_new, v_new, lam_vecs, subln_g, lam_init):
    b, heads, s, _ = q.shape
    past = k_cache.shape[2]
    kern = functools.partial(_diff_attn_sample_kernel, lam_init=lam_init)
    vec_spec = _const_spec((1, DIFF_HD))
    new_spec = pl.BlockSpec((1, heads, s, HEAD_W), lambda i: (i, 0, 0, 0))
    old_spec = pl.BlockSpec((1, heads, past, HEAD_W), lambda i: (i, 0, 0, 0))
    return pl.pallas_call(
        kern,
        grid=(b,),
        in_specs=[new_spec, old_spec, old_spec, new_spec, new_spec, vec_spec, vec_spec, vec_spec,
                  vec_spec, _const_spec((1, HEAD_W))],
        out_specs=pl.BlockSpec((1, s, heads * HEAD_W), lambda i: (i, 0, 0)),
        out_shape=jax.ShapeDtypeStruct((b, s, MIX_W), BF16),
        name="diff_attn_sample",
        compiler_params=_params("arbitrary"),
    )(q, k_cache, v_cache, k_new, v_new, *lam_vecs, subln_g.reshape(1, HEAD_W))


def _post_kernel(x_ref, o_ref, m_ref, wo_ref, g1_ref, b1_ref, wg_ref, wu_ref, wd_ref, g2_ref, b2_ref,
                 y_ref, h_ref, *, ff_chunk, row_groups):
    tm = x_ref.shape[0]
    groups = [slice(r * tm // row_groups, (r + 1) * tm // row_groups) for r in range(row_groups)]
    x1, x1b = [], []
    for rows in groups:
        mixed = (_dot(o_ref[rows, :], wo_ref[0, :MIX_W, :])
                 + _dot(m_ref[rows, :], wo_ref[0, MIX_W:, :]))
        x1.append(_layer_norm_rows(ALPHA * x_ref[rows, :] + mixed, g1_ref[0], b1_ref[0]))
        x1b.append(x1[-1].astype(BF16))
    for c in range(D_FF // ff_chunk):
        cols = slice(c * ff_chunk, (c + 1) * ff_chunk)
        for r, rows in enumerate(groups):
            gate = _dot(x1b[r], wg_ref[0, :, cols])
            up = _dot(x1b[r], wu_ref[0, :, cols])
            h_ref[rows, cols] = (gate / (1.0 + jnp.exp(-gate)) * up).astype(BF16)
    for r, rows in enumerate(groups):
        ff = _dot(h_ref[rows, :], wd_ref[0])
        y_ref[rows, :] = _layer_norm_rows(ALPHA * x1[r] + ff, g2_ref[0], b2_ref[0])


def _post(x, o, m, layer, wo, g1, b1, wg, wu, wd, g2, b2, *, tm, ff_chunk, row_groups):
    n = x.shape[0]
    row = lambda w: pl.BlockSpec((tm, w), lambda i: (i, 0))
    vec = _layer_spec((1, D_MODEL), layer)
    kern = functools.partial(_post_kernel, ff_chunk=ff_chunk, row_groups=row_groups)
    r1 = lambda a: a.reshape(DEPTH, 1, D_MODEL)
    return pl.pallas_call(
        kern,
        grid=(n // tm,),
        in_specs=[row(D_MODEL), row(MIX_W), row(MEM_W), _layer_spec((D_MODEL, D_MODEL), layer), vec, vec,
                  _layer_spec((D_MODEL, D_FF), layer), _layer_spec((D_MODEL, D_FF), layer),
                  _layer_spec((D_FF, D_MODEL), layer), vec, vec],
        out_specs=row(D_MODEL),
        out_shape=jax.ShapeDtypeStruct((n, D_MODEL), F32),
        scratch_shapes=[pltpu.VMEM((tm, D_FF), BF16)],
        name="post_mixer",
        compiler_params=_params("arbitrary"),
    )(x, o, m, wo, r1(g1), r1(b1), wg, wu, wd, r1(g2), r1(b2))


def kernel(x_prompt, x_sample, mem_prompt, cache_ret_state, cache_diff_k, cache_diff_v, cache_mem_k,
           cache_mem_v, ret_w_in, ret_gn_g, diff_w_in, diff_lambda_q1, diff_lambda_k1, diff_lambda_q2,
           diff_lambda_k2, diff_subln_g, w_mem_kv, w_o, ln1_g, ln1_b, w_gate, w_up, w_down, ln2_g, ln2_b):
    bp, sp, _ = x_prompt.shape
    bs, ss, _ = x_sample.shape
    assert sp % MIXER_ROWS == 0 and MIXER_ROWS % RET_TILE == 0 and sp % ATTN_K_TILE == 0
    assert ss == CHUNK and bs % SAMPLE_GROUP == 0 and DIFF_HEADS % ATTN_HEADS == 0
    assert (bp * sp) % POST_ROWS == 0 and (bs * ss) % POST_ROWS == 0
    pos_p = jnp.arange(sp)
    pos_s = PAST_LEN + jnp.arange(ss)
    bf = lambda a: a.astype(BF16)

    mem_k_p, mem_v_p = _mem_kv(mem_prompt, bf(w_mem_kv.transpose(0, 2, 1)))
    mem_t = lambda a: a.transpose(0, 1, 3, 4, 2).reshape(DEPTH, bs, MEM_W, N_MEM)
    mem_k_s, mem_v_s = mem_t(cache_mem_k), mem_t(cache_mem_v)

    ret_w, diff_w = bf(ret_w_in), bf(diff_w_in)
    wo, wg, wu, wd = bf(w_o), bf(w_gate), bf(w_up), bf(w_down)

    xp, xs = x_prompt, x_sample
    ret_p, ret_s, dkp, dvp, dks, dvs = [], [], [], [], [], []
    to_seq_major = lambda a: a.transpose(0, 2, 1, 3)
    for i in range(DEPTH):
        j = i // 2
        if i % 2 == 0:
            r0 = jnp.zeros((bp, RET_HEADS, RET_HD, RET_HD), F32)
            op, mp, rp = _ret_mixer(xp, pos_p, r0, mem_k_p, mem_v_p, i, ret_w, j, ret_gn_g[j],
                                    t_step=MIXER_ROWS, tile=RET_TILE, group=1)
            os_, ms, rs = _ret_mixer(xs, pos_s, cache_ret_state[j], mem_k_s, mem_v_s, i, ret_w, j,
                                     ret_gn_g[j], t_step=ss, tile=ss, group=SAMPLE_GROUP)
            ret_p.append(rp)
            ret_s.append(rs)
        else:
            lam_init = _lambda_init(i)
            lam_vecs = [a[j].reshape(1, DIFF_HD) for a in
                        (diff_lambda_q1, diff_lambda_k1, diff_lambda_q2, diff_lambda_k2)]
            q, k, v, mp = _diff_project(xp, pos_p, mem_k_p, mem_v_p, i, diff_w, j, t_step=MIXER_ROWS,
                                        group=1)
            op = _diff_attn_prompt(q, k, v, lam_vecs, diff_subln_g[j], lam_init, tq=ATTN_Q_TILE,
                                   tk=ATTN_K_TILE, heads=ATTN_HEADS)
            dkp.append(to_seq_major(k))
            dvp.append(to_seq_major(v))
            q, k, v, ms = _diff_project(xs, pos_s, mem_k_s, mem_v_s, i, diff_w, j, t_step=ss,
                                        group=SAMPLE_GROUP)
            os_ = _diff_attn_sample(q, cache_diff_k[j].transpose(0, 2, 1, 3),
                                    cache_diff_v[j].transpose(0, 2, 1, 3), k, v, lam_vecs,
                                    diff_subln_g[j], lam_init)
            dks.append(to_seq_major(k))
            dvs.append(to_seq_major(v))
        post = functools.partial(_post, layer=i, wo=wo, g1=ln1_g, b1=ln1_b, wg=wg, wu=wu, wd=wd,
                                 g2=ln2_g, b2=ln2_b, tm=POST_ROWS, ff_chunk=FF_CHUNK,
                                 row_groups=POST_ROW_GROUPS)
        xp = post(xp.reshape(bp * sp, D_MODEL), op.reshape(bp * sp, MIX_W),
                  mp.reshape(bp * sp, MEM_W)).reshape(bp, sp, D_MODEL)
        xs = post(xs.reshape(bs * ss, D_MODEL), os_.reshape(bs * ss, MIX_W),
                  ms.reshape(bs * ss, MEM_W)).reshape(bs, ss, D_MODEL)

    mem_out = lambda a: a.reshape(DEPTH, bp, MEM_HEADS, MEM_HD, N_MEM).transpose(0, 1, 4, 2, 3)
    return (xp, xs, jnp.stack(ret_p), jnp.stack(ret_s), jnp.stack(dkp), jnp.stack(dvp),
            jnp.stack(dks), jnp.stack(dvs), mem_out(mem_k_p), mem_out(mem_v_p))
```

```python
import functools
import math

import jax
import jax.numpy as jnp
import numpy as np
from jax import lax
from jax.experimental import pallas as pl
from jax.experimental.pallas import tpu as pltpu

D_MODEL = 1024
DEPTH = 2
PAST_LEN = 1024
CHUNK = 64
N_MEM = 256
MEM_HEADS = 4
MEM_HD = 64
MEM_W = MEM_HEADS * MEM_HD
MIX_W = D_MODEL - MEM_W
RET_HEADS = 6
RET_HD = MIX_W // RET_HEADS
RET_THETA = 10000.0
DIFF_HEADS = 6
DIFF_HD = MIX_W // (2 * DIFF_HEADS)
ROPE_THETA = 500000.0
ROT_DIM = DIFF_HD // 4
D_FF = -(-8 * D_MODEL // (3 * 256)) * 256
ALPHA = (2 * DEPTH) ** 0.25
LN_EPS = 1e-5
NEG_INF = -1e30

Q_SCALE = DIFF_HD ** -0.5 * math.log2(math.e)
HEAD_W = 128
SUM_ROWS = 16

VMEM_LIMIT = 56 * 1024 * 1024
MIXER_ROWS = 512
RET_TILE = 4 * CHUNK
SAMPLE_GROUP = 4
ATTN_Q_TILE = 256
ATTN_K_TILE = 512
ATTN_HEADS = 2
POST_ROWS = 1024
POST_ROW_GROUPS = 4
FF_CHUNK = 256

F32 = jnp.float32
BF16 = jnp.bfloat16


def _lambda_init(layer_idx):
    return 0.8 - 0.6 * math.exp(-0.3 * layer_idx)


def _dot(a, b):
    return jnp.dot(a.astype(BF16), b.astype(BF16), preferred_element_type=F32)


def _dot_nt(a, b):
    return lax.dot_general(a.astype(BF16), b.astype(BF16), (((1,), (1,)), ((), ())),
                           preferred_element_type=F32)


def _dot_tn(a, b):
    return lax.dot_general(a.astype(BF16), b.astype(BF16), (((0,), (0,)), ((), ())),
                           preferred_element_type=F32)


def _fixed_spec(shape, index):
    return pl.BlockSpec(shape, lambda *_: index, pipeline_mode=pl.Buffered(1))


def _const_spec(shape):
    return _fixed_spec(shape, (0,) * len(shape))


def _layer_spec(shape, layer):
    return _fixed_spec((1,) + shape, (layer,) + (0,) * len(shape))


def _params(*semantics):
    return pltpu.CompilerParams(dimension_semantics=semantics, vmem_limit_bytes=VMEM_LIMIT)


def _layer_norm_rows(y, g, b):
    mu = jnp.mean(y, axis=-1, keepdims=True)
    d = y - mu
    var = jnp.mean(d * d, axis=-1, keepdims=True)
    return d * lax.rsqrt(var + LN_EPS) * g + b


def _memory_attention(mq, mk_t, mv_t):
    mqb = _memory_query(mq)
    out = _memory_head(mqb, mk_t, mv_t, 0)
    for h in range(1, MEM_HEADS):
        out = out + _memory_head(mqb, mk_t, mv_t, h)
    return out


def _memory_query(mq):
    return (mq * (MEM_HD ** -0.5 * math.log2(math.e))).astype(BF16)


def _memory_head(mqb, mk_t, mv_t, h):
    row = lax.broadcasted_iota(jnp.int32, mk_t.shape, 0)
    sel = (row >= h * MEM_HD) & (row < (h + 1) * MEM_HD)
    s = _dot(mqb, jnp.where(sel, mk_t, 0.0))
    p = jnp.exp2(s - jnp.max(s, axis=-1, keepdims=True))
    return _dot_nt(p, jnp.where(sel, mv_t, 0.0)) * (1.0 / jnp.sum(p, axis=-1, keepdims=True))


def _mem_kv_kernel(x_ref, w_ref, k_ref, v_ref):
    xb = x_ref[0].astype(BF16)
    for layer in range(DEPTH):
        kv_t = _dot_nt(w_ref[layer], xb)
        k_ref[layer, 0] = kv_t[:MEM_W]
        v_ref[layer, 0] = kv_t[MEM_W:]


def _mem_kv(mem, w_t_bf):
    b = mem.shape[0]
    out = jax.ShapeDtypeStruct((DEPTH, b, MEM_W, N_MEM), F32)
    return pl.pallas_call(
        _mem_kv_kernel,
        grid=(b,),
        in_specs=[pl.BlockSpec((1, N_MEM, D_MODEL), lambda i: (i, 0, 0)),
                  _const_spec((DEPTH, 2 * MEM_W, D_MODEL))],
        out_specs=[pl.BlockSpec((DEPTH, 1, MEM_W, N_MEM), lambda i: (0, i, 0, 0))] * 2,
        out_shape=[out, out],
        name="mem_kv",
        compiler_params=_params("arbitrary"),
    )(mem, w_t_bf)


def _ret_tables(tile):
    h = np.arange(RET_HEADS, dtype=np.float64)
    log_g = np.log(1.0 - np.exp2(-5.0 - h))
    idx = np.arange(tile, dtype=np.float64)
    dist = np.abs(idx[:, None] - idx[None, :])
    visible = (idx[None, :] // CHUNK) <= (idx[:, None] // CHUNK)
    decay = np.where(visible[None], np.exp(dist[None] * log_g[:, None, None]), 0.0)
    xi = np.exp((idx + 1.0)[None, :] * log_g[:, None])[:, :, None]
    zeta = np.exp((tile - 1.0 - idx)[None, :] * log_g[:, None])[:, :, None]
    g_tile = np.exp(tile * log_g)
    f = lambda a: jnp.asarray(a, F32)
    return f(decay), f(xi), f(zeta), [float(g) for g in g_tile]


def _ret_kernel(x_ref, w_ref, cos_ref, sin_ref, r0_ref, mk_ref, mv_ref, gn_ref, decay_ref, xi_ref,
                zeta_ref, o_ref, m_ref, r_ref, state_ref, *, tile, g_tile):
    step = pl.program_id(1)

    @pl.when(step == 0)
    def _():
        state_ref[...] = r0_ref[...]

    group, t, _ = x_ref.shape
    proj = _dot(x_ref[...].reshape(group * t, D_MODEL), w_ref[0])
    cos = cos_ref[...]
    sin = sin_ref[...]

    def rope(a):
        return a * cos + pltpu.roll(a, RET_HD // 2, 1) * sin

    for g in range(group):
        own = slice(g * t, (g + 1) * t)
        for h in range(RET_HEADS):
            col = h * HEAD_W
            q = rope(proj[own, col:col + HEAD_W])
            k = rope(proj[own, MIX_W + col:MIX_W + col + HEAD_W]) * RET_HD ** -0.5
            v = proj[own, 2 * MIX_W + col:2 * MIX_W + col + HEAD_W]
            gate = proj[own, 3 * MIX_W + col:3 * MIX_W + col + HEAD_W]
            outs = []
            r = state_ref[g, h]
            for c in range(t // tile):
                rows = slice(c * tile, (c + 1) * tile)
                qc, kc, vc = q[rows], k[rows], v[rows]
                inner = _dot_nt(qc, kc) * decay_ref[h]
                outs.append(_dot(inner, vc) + _dot(qc, r) * xi_ref[h])
                r = g_tile[h] * r + _dot_tn(kc * zeta_ref[h], vc)
            state_ref[g, h] = r
            o = outs[0] if len(outs) == 1 else jnp.concatenate(outs, axis=0)
            mu = jnp.mean(o, axis=-1, keepdims=True)
            d = o - mu
            var = jnp.mean(d * d, axis=-1, keepdims=True)
            o = d * lax.rsqrt(var + LN_EPS) * gn_ref[:, col:col + HEAD_W]
            o = o * (gate / (1.0 + jnp.exp(-gate)))
            o_ref[g, :, col:col + HEAD_W] = o.astype(o_ref.dtype)

        m = _memory_attention(proj[own, 4 * MIX_W:], mk_ref[0, g], mv_ref[0, g])
        m_ref[g] = m.astype(m_ref.dtype)

    @pl.when(step == pl.num_programs(1) - 1)
    def _():
        r_ref[...] = state_ref[...]


def _ret_mixer(x, pos, r0, mk_t, mv_t, layer, w_bf, w_layer, gn_g, *, t_step, tile, group):
    b, s, _ = x.shape
    half = RET_HD // 2
    lane_freq = jnp.arange(HEAD_W, dtype=jnp.int32) % half
    inv_freq = jnp.exp(-math.log(RET_THETA) * lane_freq.astype(F32) * 2.0 / RET_HD)
    ang = pos.astype(F32)[:, None] * inv_freq[None, :]
    cos = jnp.cos(ang)
    sin = jnp.where(jnp.arange(HEAD_W) < half, -jnp.sin(ang), jnp.sin(ang))
    decay, xi, zeta, g_tile = _ret_tables(tile)
    cols = w_bf.shape[-1]
    kern = functools.partial(_ret_kernel, tile=tile, g_tile=g_tile)
    mem_spec = pl.BlockSpec((1, group, MEM_W, N_MEM), lambda i, j: (layer, i, 0, 0))
    state_spec = pl.BlockSpec((group, RET_HEADS, RET_HD, RET_HD), lambda i, j: (i, 0, 0, 0))
    return pl.pallas_call(
        kern,
        grid=(b // group, s // t_step),
        in_specs=[
            pl.BlockSpec((group, t_step, D_MODEL), lambda i, j: (i, j, 0)),
            _layer_spec((D_MODEL, cols), w_layer),
            pl.BlockSpec((t_step, HEAD_W), lambda i, j: (j, 0)),
            pl.BlockSpec((t_step, HEAD_W), lambda i, j: (j, 0)),
            state_spec,
            mem_spec, mem_spec,
            _const_spec((1, MIX_W)),
            _const_spec((RET_HEADS, tile, tile)),
            _const_spec((RET_HEADS, tile, 1)),
            _const_spec((RET_HEADS, tile, 1)),
        ],
        out_specs=[
            pl.BlockSpec((group, t_step, MIX_W), lambda i, j: (i, j, 0)),
            pl.BlockSpec((group, t_step, MEM_W), lambda i, j: (i, j, 0)),
            state_spec,
        ],
        out_shape=[
            jax.ShapeDtypeStruct((b, s, MIX_W), BF16),
            jax.ShapeDtypeStruct((b, s, MEM_W), BF16),
            jax.ShapeDtypeStruct((b, RET_HEADS, RET_HD, RET_HD), F32),
        ],
        scratch_shapes=[pltpu.VMEM((group, RET_HEADS, RET_HD, RET_HD), F32)],
        name="ret_mixer",
        compiler_params=_params("arbitrary", "arbitrary"),
    )(x, w_bf, cos, sin, r0, mk_t, mv_t, gn_g.reshape(1, MIX_W), decay, xi, zeta)


def _diff_proj_kernel(x_ref, w_ref, c_ref, sa_ref, sb_ref, mk_ref, mv_ref, q_ref, k_ref, v_ref, m_ref):
    group, t, _ = x_ref.shape
    proj = _dot(x_ref[...].reshape(group * t, D_MODEL), w_ref[0])
    c = c_ref[...]
    sa = sa_ref[...]
    sb = sb_ref[...]

    def rope(a):
        return (a * c + pltpu.roll(a, ROT_DIM // 2, 1) * sa
                + pltpu.roll(a, HEAD_W - ROT_DIM // 2, 1) * sb)

    for g in range(group):
        own = slice(g * t, (g + 1) * t)
        for h in range(DIFF_HEADS):
            col = h * HEAD_W
            q = rope(proj[own, col:col + HEAD_W]) * Q_SCALE
            q_ref[g, h] = q.astype(q_ref.dtype)
            k_ref[g, h] = rope(proj[own, MIX_W + col:MIX_W + col + HEAD_W])
            v_ref[g, h] = proj[own, 2 * MIX_W + col:2 * MIX_W + col + HEAD_W]
        m = _memory_attention(proj[own, 3 * MIX_W:], mk_ref[0, g], mv_ref[0, g])
        m_ref[g] = m.astype(m_ref.dtype)


def _diff_project(x, pos, mk_t, mv_t, layer, w_bf, w_layer, *, t_step, group):
    b, s, _ = x.shape
    half = ROT_DIM // 2
    lane = jnp.arange(HEAD_W, dtype=jnp.int32) % DIFF_HD
    inv_freq = jnp.exp(-math.log(ROPE_THETA) * (lane % half).astype(F32) * 2.0 / ROT_DIM)
    ang = pos.astype(F32)[:, None] * inv_freq[None, :]
    cos, sin = jnp.cos(ang), jnp.sin(ang)
    tables = [jnp.where(lane < ROT_DIM, cos, 1.0),
              jnp.where((lane >= half) & (lane < ROT_DIM), sin, 0.0),
              jnp.where(lane < half, -sin, 0.0)]
    cols = w_bf.shape[-1]
    tab_spec = pl.BlockSpec((t_step, HEAD_W), lambda i, j: (j, 0))
    tok_spec = lambda w: pl.BlockSpec((group, t_step, w), lambda i, j: (i, j, 0))
    head_spec = pl.BlockSpec((group, DIFF_HEADS, t_step, HEAD_W), lambda i, j: (i, 0, j, 0))
    mem_spec = pl.BlockSpec((1, group, MEM_W, N_MEM), lambda i, j: (layer, i, 0, 0))
    heads = lambda dt: jax.ShapeDtypeStruct((b, DIFF_HEADS, s, HEAD_W), dt)
    return pl.pallas_call(
        _diff_proj_kernel,
        grid=(b // group, s // t_step),
        in_specs=[tok_spec(D_MODEL), _layer_spec((D_MODEL, cols), w_layer), tab_spec, tab_spec,
                  tab_spec, mem_spec, mem_spec],
        out_specs=[head_spec, head_spec, head_spec, tok_spec(MEM_W)],
        out_shape=[
            heads(BF16),
            heads(F32),
            heads(F32),
            jax.ShapeDtypeStruct((b, s, MEM_W), BF16),
        ],
        name="diff_proj",
        compiler_params=_params("arbitrary", "arbitrary"),
    )(x, w_bf, *tables, mk_t, mv_t)


def _diff_lambda(lq1_ref, lk1_ref, lq2_ref, lk2_ref, lam_init):
    a = jnp.sum(lq1_ref[...] * lk1_ref[...], axis=-1, keepdims=True)
    b = jnp.sum(lq2_ref[...] * lk2_ref[...], axis=-1, keepdims=True)
    return jnp.exp(a) - jnp.exp(b) + lam_init


def _diff_finish(o, g_ref, lam_init):
    ms = jnp.mean(o * o, axis=-1, keepdims=True)
    return o * lax.rsqrt(ms + LN_EPS) * g_ref[...] * (1.0 - lam_init)


def _split_heads(k):
    lane = lax.broadcasted_iota(jnp.int32, k.shape, 1)
    first = lane < DIFF_HD
    return jnp.where(first, k, 0.0).astype(BF16), jnp.where(first, 0.0, k).astype(BF16)


def _causal_tiles(n_k):
    return [(j, int(j == g), int(j == 0), int(j == g), g) for g in range(n_k) for j in range(g + 1)]


def _diff_attn_prompt_kernel(tab_ref, q_ref, k_ref, v_ref, lq1_ref, lk1_ref, lq2_ref, lk2_ref, g_ref,
                             o_ref, kx_ref, vt_ref, qt_ref, mask_ref, s_ref, p_ref, acc_ref, *,
                             tq, tk, heads, lam_init):
    seq = q_ref.shape[2]
    n_q, n_k = seq // tq, seq // tk
    q_per_k = tk // tq
    tiles = _causal_tiles(n_k)
    n_tiles = len(tiles)
    chunks_q, chunks_k = tq // CHUNK, tk // CHUNK
    streams = [(hd, qs, mp) for hd in range(heads) for qs in range(q_per_k) for mp in range(2)]

    @pl.when((pl.program_id(0) == 0) & (pl.program_id(1) == 0))
    def _():
        lane = lax.broadcasted_iota(jnp.int32, (seq, HEAD_W), 1)
        row_chunk = (lax.broadcasted_iota(jnp.int32, (seq, HEAD_W), 0) // CHUNK) % chunks_k
        chunk_one_hot = jnp.where(lane == row_chunk, 1.0, 0.0).astype(BF16)
        ones_row = jnp.where(lax.broadcasted_iota(jnp.int32, (SUM_ROWS, tk), 0) == 0, 1.0, 0.0).astype(BF16)
        for hd in range(heads):
            kx_ref[hd, 0, :, HEAD_W:] = chunk_one_hot
            kx_ref[hd, 1, :, HEAD_W:] = chunk_one_hot
            for t in range(n_k):
                vt_ref[hd, t, HEAD_W:, :] = ones_row
        bias_row = lax.broadcasted_iota(jnp.int32, (HEAD_W, tq), 0)
        bias_col = lax.broadcasted_iota(jnp.int32, (HEAD_W, tq), 1) // CHUNK
        mask_ref[0] = jnp.zeros((HEAD_W, tq), BF16)
        for qs in range(q_per_k):
            hidden = (bias_row < chunks_k) & (bias_row > bias_col + chunks_q * qs)
            mask_ref[1 + qs] = jnp.where(hidden, NEG_INF, 0.0).astype(BF16)
        acc_ref[...] = jnp.zeros(acc_ref.shape, F32)

    first = lax.broadcasted_iota(jnp.int32, (seq, HEAD_W), 1) < DIFF_HD
    for hd in range(heads):
        k = k_ref[0, hd]
        kx_ref[hd, 0, :, :HEAD_W] = jnp.where(first, k, 0.0).astype(BF16)
        kx_ref[hd, 1, :, :HEAD_W] = jnp.where(first, 0.0, k).astype(BF16)
        for t in range(n_k):
            vt_ref[hd, t, :HEAD_W, :] = v_ref[0, hd, t * tk:(t + 1) * tk, :].T.astype(BF16)
        for qi in range(n_q):
            qt_ref[hd, qi] = q_ref[0, hd, qi * tq:(qi + 1) * tq, :].astype(F32).T.astype(BF16)
    lam = _diff_lambda(lq1_ref, lk1_ref, lq2_ref, lk2_ref, lam_init)

    def tile_of(i):
        if isinstance(i, int):
            j, diag, is_first, is_last, grp = tiles[i]
            return j, diag, bool(is_first), bool(is_last), grp
        j, diag, is_first, is_last, grp = (tab_ref[i, f] for f in range(5))
        return j, diag, is_first == 1, is_last == 1, grp

    def stage_a(i, slot):
        j, diag, _, _, grp = tile_of(i)
        base = j * tk if isinstance(j, int) else pl.multiple_of(j * tk, tk)
        tops = []
        for hd, qs, mp in streams:
            q_op = jnp.concatenate([qt_ref[hd, grp * q_per_k + qs], mask_ref[diag * (1 + qs)]], axis=0)
            s = _dot(kx_ref[hd, mp, pl.ds(base, tk), :], q_op)
            s_ref[hd, slot, qs, mp] = s
            tops.append(jnp.max(s, axis=0, keepdims=True))
        return tuple(tops)

    def stage_b(i, slot, m, top):
        _, _, is_first, _, _ = tile_of(i)
        out = []
        for n, (hd, qs, mp) in enumerate(streams):
            m_prev = jnp.where(is_first, NEG_INF, m[n])
            m_new = jnp.maximum(m_prev, top[n])
            scale = jnp.exp2(m_prev - m_new)
            p_ref[hd, slot, qs, mp] = jnp.exp2(s_ref[hd, slot, qs, mp] - m_new).astype(BF16)
            out.append((m_new, scale))
        return tuple(zip(*out))

    def stage_c(i, slot, scale):
        j = tile_of(i)[0]
        for n, (hd, qs, mp) in enumerate(streams):
            acc_ref[hd, qs, mp] = (scale[n] * acc_ref[hd, qs, mp]
                                   + _dot(vt_ref[hd, j], p_ref[hd, slot, qs, mp]))

    def finish_if_last(i):
        _, _, _, is_last, grp = tile_of(i)

        def finish():
            for hd in range(heads):
                for qs in range(q_per_k):
                    w1 = 1.0 / acc_ref[hd, qs, 0, HEAD_W:HEAD_W + 1, :]
                    w2 = lam / acc_ref[hd, qs, 1, HEAD_W:HEAD_W + 1, :]
                    o_t = acc_ref[hd, qs, 0, :HEAD_W, :] * w1 - acc_ref[hd, qs, 1, :HEAD_W, :] * w2
                    ms = jnp.mean(o_t * o_t, axis=0, keepdims=True)
                    o = (o_t * (lax.rsqrt(ms + LN_EPS) * (1.0 - lam_init))).T * g_ref[...]
                    start = (grp * q_per_k + qs) * tq
                    rows = pl.ds(start if isinstance(start, int) else pl.multiple_of(start, tq), tq)
                    o_ref[0, rows, hd * HEAD_W:(hd + 1) * HEAD_W] = o.astype(o_ref.dtype)

        if isinstance(is_last, bool):
            if is_last:
                finish()
        else:
            pl.when(is_last)(finish)

    def iteration(i, slot, carry, a=True, b=True, c=True):
        m, top, scale = carry
        new_scale = scale
        if b:
            m, new_scale = stage_b(i - 1, 1 - slot, m, top)
        if a:
            top = stage_a(i, slot)
        if c:
            stage_c(i - 2, slot, scale)
            finish_if_last(i - 2)
        return m, top, new_scale

    def pair(n, carry):
        i = 2 + 2 * n
        return iteration(i + 1, 1, iteration(i, 0, carry))

    zero = (jnp.zeros((1, tq), F32),) * len(streams)
    carry = (zero, zero, zero)
    carry = iteration(0, 0, carry, b=False, c=False)
    carry = iteration(1, 1, carry, c=False)
    carry = lax.fori_loop(0, (n_tiles - 2) // 2, pair, carry)
    carry = iteration(n_tiles, 0, carry, a=False)
    iteration(n_tiles + 1, 1, carry, a=False, b=False)


def _diff_attn_prompt(q, k, v, lam_vecs, subln_g, lam_init, *, tq, tk, heads):
    b, _, s, _ = q.shape
    q_per_k = tk // tq
    rows = _causal_tiles(s // tk)
    assert len(rows) % 2 == 0
    table = jnp.asarray(np.array(rows, np.int32))
    kern = functools.partial(_diff_attn_prompt_kernel, tq=tq, tk=tk, heads=heads, lam_init=lam_init)
    const = lambda shape: pl.BlockSpec(shape, lambda i, h, tab: (0,) * len(shape))
    head_spec = pl.BlockSpec((1, heads, s, HEAD_W), lambda i, h, tab: (i, h, 0, 0))
    return pl.pallas_call(
        kern,
        grid_spec=pltpu.PrefetchScalarGridSpec(
            num_scalar_prefetch=1,
            grid=(b, DIFF_HEADS // heads),
            in_specs=[head_spec, head_spec, head_spec] + [const((1, DIFF_HD))] * 4 + [const((1, HEAD_W))],
            out_specs=pl.BlockSpec((1, s, heads * HEAD_W), lambda i, h, tab: (i, 0, h)),
            scratch_shapes=[
                pltpu.VMEM((heads, 2, s, 2 * HEAD_W), BF16),
                pltpu.VMEM((heads, s // tk, HEAD_W + SUM_ROWS, tk), BF16),
                pltpu.VMEM((heads, s // tq, HEAD_W, tq), BF16),
                pltpu.VMEM((1 + tk // tq, HEAD_W, tq), BF16),
                pltpu.VMEM((heads, 2, q_per_k, 2, tk, tq), F32),
                pltpu.VMEM((heads, 2, q_per_k, 2, tk, tq), BF16),
                pltpu.VMEM((heads, q_per_k, 2, HEAD_W + SUM_ROWS, tq), F32),
            ]),
        out_shape=jax.ShapeDtypeStruct((b, s, MIX_W), BF16),
        name="diff_attn_prompt",
        compiler_params=_params("arbitrary", "arbitrary"),
    )(table, q, k, v, *lam_vecs, subln_g.reshape(1, HEAD_W))


def _diff_attn_sample_kernel(q_ref, kc_ref, vc_ref, kn_ref, vn_ref, lq1_ref, lk1_ref, lq2_ref, lk2_ref,
                             g_ref, o_ref, *, lam_init):
    lam = _diff_lambda(lq1_ref, lk1_ref, lq2_ref, lk2_ref, lam_init)
    for h in range(q_ref.shape[1]):
        q = q_ref[0, h]
        s = q.shape[0]
        q1, q2 = _split_heads(q)
        qq = jnp.concatenate([q1, q2], axis=0)
        sc = _dot_nt(qq, kc_ref[0, h])
        sn = _dot_nt(qq, kn_ref[0, h])
        m = jnp.maximum(jnp.max(sc, axis=-1, keepdims=True), jnp.max(sn, axis=-1, keepdims=True))
        pc = jnp.exp2(sc - m)
        pn = jnp.exp2(sn - m)
        l = jnp.sum(pc, axis=-1, keepdims=True) + jnp.sum(pn, axis=-1, keepdims=True)
        o2 = (_dot(pc, vc_ref[0, h]) + _dot(pn, vn_ref[0, h])) / l
        o = o2[:s] - lam * o2[s:]
        o_ref[0, :, h * HEAD_W:(h + 1) * HEAD_W] = _diff_finish(o, g_ref, lam_init).astype(o_ref.dtype)


def _diff_attn_sample(q, k_cache, v_cache, k_new, v_new, lam_vecs, subln_g, lam_init):
    b, heads, s, _ = q.shape
    past = k_cache.shape[2]
    kern = functools.partial(_diff_attn_sample_kernel, lam_init=lam_init)
    vec_spec = _const_spec((1, DIFF_HD))
    new_spec = pl.BlockSpec((1, heads, s, HEAD_W), lambda i: (i, 0, 0, 0))
    old_spec = pl.BlockSpec((1, heads, past, HEAD_W), lambda i: (i, 0, 0, 0))
    return pl.pallas_call(
        kern,
        grid=(b,),
        in_specs=[new_spec, old_spec, old_spec, new_spec, new_spec, vec_spec, vec_spec, vec_spec,
                  vec_spec, _const_spec((1, HEAD_W))],
        out_specs=pl.BlockSpec((1, s, heads * HEAD_W), lambda i: (i, 0, 0)),
        out_shape=jax.ShapeDtypeStruct((b, s, MIX_W), BF16),
        name="diff_attn_sample",
        compiler_params=_params("arbitrary"),
    )(q, k_cache, v_cache, k_new, v_new, *lam_vecs, subln_g.reshape(1, HEAD_W))


def _post_kernel(x_ref, o_ref, m_ref, wo_ref, g1_ref, b1_ref, wg_ref, wu_ref, wd_ref, g2_ref, b2_ref,
                 y_ref, h_ref, *, ff_chunk, row_groups):
    tm = x_ref.shape[0]
    groups = [slice(r * tm // row_groups, (r + 1) * tm // row_groups) for r in range(row_groups)]
    x1, x1b = [], []
    for rows in groups:
        mixed = (_dot(o_ref[rows, :], wo_ref[0, :MIX_W, :])
                 + _dot(m_ref[rows, :], wo_ref[0, MIX_W:, :]))
        x1.append(_layer_norm_rows(ALPHA * x_ref[rows, :] + mixed, g1_ref[0], b1_ref[0]))
        x1b.append(x1[-1].astype(BF16))
    for c in range(D_FF // ff_chunk):
        cols = slice(c * ff_chunk, (c + 1) * ff_chunk)
        for r, rows in enumerate(groups):
            gate = _dot(x1b[r], wg_ref[0, :, cols])
            up = _dot(x1b[r], wu_ref[0, :, cols])
            h_ref[rows, cols] = (gate / (1.0 + jnp.exp(-gate)) * up).astype(BF16)
    for r, rows in enumerate(groups):
        ff = _dot(h_ref[rows, :], wd_ref[0])
        y_ref[rows, :] = _layer_norm_rows(ALPHA * x1[r] + ff, g2_ref[0], b2_ref[0])


def _post(x, o, m, layer, wo, g1, b1, wg, wu, wd, g2, b2, *, tm, ff_chunk, row_groups):
    n = x.shape[0]
    row = lambda w: pl.BlockSpec((tm, w), lambda i: (i, 0))
    vec = _layer_spec((1, D_MODEL), layer)
    kern = functools.partial(_post_kernel, ff_chunk=ff_chunk, row_groups=row_groups)
    r1 = lambda a: a.reshape(DEPTH, 1, D_MODEL)
    return pl.pallas_call(
        kern,
        grid=(n // tm,),
        in_specs=[row(D_MODEL), row(MIX_W), row(MEM_W), _layer_spec((D_MODEL, D_MODEL), layer), vec, vec,
                  _layer_spec((D_MODEL, D_FF), layer), _layer_spec((D_MODEL, D_FF), layer),
                  _layer_spec((D_FF, D_MODEL), layer), vec, vec],
        out_specs=row(D_MODEL),
        out_shape=jax.ShapeDtypeStruct((n, D_MODEL), F32),
        scratch_shapes=[pltpu.VMEM((tm, D_FF), BF16)],
        name="post_mixer",
        compiler_params=_params("arbitrary"),
    )(x, o, m, wo, r1(g1), r1(b1), wg, wu, wd, r1(g2), r1(b2))


def kernel(x_prompt, x_sample, mem_prompt, cache_ret_state, cache_diff_k, cache_diff_v, cache_mem_k,
           cache_mem_v, ret_w_in, ret_gn_g, diff_w_in, diff_lambda_q1, diff_lambda_k1, diff_lambda_q2,
           diff_lambda_k2, diff_subln_g, w_mem_kv, w_o, ln1_g, ln1_b, w_gate, w_up, w_down, ln2_g, ln2_b):
    bp, sp, _ = x_prompt.shape
    bs, ss, _ = x_sample.shape
    assert sp % MIXER_ROWS == 0 and MIXER_ROWS % RET_TILE == 0 and sp % ATTN_K_TILE == 0
    assert ss == CHUNK and bs % SAMPLE_GROUP == 0 and DIFF_HEADS % ATTN_HEADS == 0
    assert (bp * sp) % POST_ROWS == 0 and (bs * ss) % POST_ROWS == 0
    pos_p = jnp.arange(sp)
    pos_s = PAST_LEN + jnp.arange(ss)
    bf = lambda a: a.astype(BF16)

    mem_k_p, mem_v_p = _mem_kv(mem_prompt, bf(w_mem_kv.transpose(0, 2, 1)))
    mem_t = lambda a: a.transpose(0, 1, 3, 4, 2).reshape(DEPTH, bs, MEM_W, N_MEM)
    mem_k_s, mem_v_s = mem_t(cache_mem_k), mem_t(cache_mem_v)

    ret_w, diff_w = bf(ret_w_in), bf(diff_w_in)
    wo, wg, wu, wd = bf(w_o), bf(w_gate), bf(w_up), bf(w_down)

    xp, xs = x_prompt, x_sample
    ret_p, ret_s, dkp, dvp, dks, dvs = [], [], [], [], [], []
    to_seq_major = lambda a: a.transpose(0, 2, 1, 3)
    for i in range(DEPTH):
        j = i // 2
        if i % 2 == 0:
            r0 = jnp.zeros((bp, RET_HEADS, RET_HD, RET_HD), F32)
            op, mp, rp = _ret_mixer(xp, pos_p, r0, mem_k_p, mem_v_p, i, ret_w, j, ret_gn_g[j],
                                    t_step=MIXER_ROWS, tile=RET_TILE, group=1)
            os_, ms, rs = _ret_mixer(xs, pos_s, cache_ret_state[j], mem_k_s, mem_v_s, i, ret_w, j,
                                     ret_gn_g[j], t_step=ss, tile=ss, group=SAMPLE_GROUP)
            ret_p.append(rp)
            ret_s.append(rs)
        else:
            lam_init = _lambda_init(i)
            lam_vecs = [a[j].reshape(1, DIFF_HD) for a in
                        (diff_lambda_q1, diff_lambda_k1, diff_lambda_q2, diff_lambda_k2)]
            q, k, v, mp = _diff_project(xp, pos_p, mem_k_p, mem_v_p, i, diff_w, j, t_step=MIXER_ROWS,
                                        group=1)
            op = _diff_attn_prompt(q, k, v, lam_vecs, diff_subln_g[j], lam_init, tq=ATTN_Q_TILE,
                                   tk=ATTN_K_TILE, heads=ATTN_HEADS)
            dkp.append(to_seq_major(k))
            dvp.append(to_seq_major(v))
            q, k, v, ms = _diff_project(xs, pos_s, mem_k_s, mem_v_s, i, diff_w, j, t_step=ss,
                                        group=SAMPLE_GROUP)
            os_ = _diff_attn_sample(q, cache_diff_k[j].transpose(0, 2, 1, 3),
                                    cache_diff_v[j].transpose(0, 2, 1, 3), k, v, lam_vecs,
                                    diff_subln_g[j], lam_init)
            dks.append(to_seq_major(k))
            dvs.append(to_seq_major(v))
        post = functools.partial(_post, layer=i, wo=wo, g1=ln1_g, b1=ln1_b, wg=wg, wu=wu, wd=wd,
                                 g2=ln2_g, b2=ln2_b, tm=POST_ROWS, ff_chunk=FF_CHUNK,
                                 row_groups=POST_ROW_GROUPS)
        xp = post(xp.reshape(bp * sp, D_MODEL), op.reshape(bp * sp, MIX_W),
                  mp.reshape(bp * sp, MEM_W)).reshape(bp, sp, D_MODEL)
        xs = post(xs.reshape(bs * ss, D_MODEL), os_.reshape(bs * ss, MIX_W),
                  ms.reshape(bs * ss, MEM_W)).reshape(bs, ss, D_MODEL)

    mem_out = lambda a: a.reshape(DEPTH, bp, MEM_HEADS, MEM_HD, N_MEM).transpose(0, 1, 4, 2, 3)
    return (xp, xs, jnp.stack(ret_p), jnp.stack(ret_s), jnp.stack(dkp), jnp.stack(dvp),
            jnp.stack(dks), jnp.stack(dvs), mem_out(mem_k_p), mem_out(mem_v_p))
```

```python
import functools
import math

import jax
import jax.numpy as jnp
import numpy as np
from jax import lax
from jax.experimental import pallas as pl
from jax.experimental.pallas import tpu as pltpu

D_MODEL = 1024
DEPTH = 2
PAST_LEN = 1024
CHUNK = 64
N_MEM = 256
MEM_HEADS = 4
MEM_HD = 64
MEM_W = MEM_HEADS * MEM_HD
MIX_W = D_MODEL - MEM_W
RET_HEADS = 6
RET_HD = MIX_W // RET_HEADS
RET_THETA = 10000.0
DIFF_HEADS = 6
DIFF_HD = MIX_W // (2 * DIFF_HEADS)
ROPE_THETA = 500000.0
ROT_DIM = DIFF_HD // 4
D_FF = -(-8 * D_MODEL // (3 * 256)) * 256
ALPHA = (2 * DEPTH) ** 0.25
LN_EPS = 1e-5
NEG_INF = -1e30

Q_SCALE = DIFF_HD ** -0.5 * math.log2(math.e)
HEAD_W = 128
SUM_ROWS = 16

VMEM_LIMIT = 56 * 1024 * 1024
MIXER_ROWS = 1024
RET_TILE = 4 * CHUNK
SAMPLE_GROUP = 4
ATTN_Q_TILE = 256
ATTN_K_TILE = 512
ATTN_HEADS = 2
POST_ROWS = 1024
POST_ROW_GROUPS = 4
FF_CHUNK = 256

F32 = jnp.float32
BF16 = jnp.bfloat16


def _lambda_init(layer_idx):
    return 0.8 - 0.6 * math.exp(-0.3 * layer_idx)


def _dot(a, b):
    return jnp.dot(a.astype(BF16), b.astype(BF16), preferred_element_type=F32)


def _dot_nt(a, b):
    return lax.dot_general(a.astype(BF16), b.astype(BF16), (((1,), (1,)), ((), ())),
                           preferred_element_type=F32)


def _dot_tn(a, b):
    return lax.dot_general(a.astype(BF16), b.astype(BF16), (((0,), (0,)), ((), ())),
                           preferred_element_type=F32)


def _fixed_spec(shape, index):
    return pl.BlockSpec(shape, lambda *_: index, pipeline_mode=pl.Buffered(1))


def _const_spec(shape):
    return _fixed_spec(shape, (0,) * len(shape))


def _layer_spec(shape, layer):
    return _fixed_spec((1,) + shape, (layer,) + (0,) * len(shape))


def _params(*semantics):
    return pltpu.CompilerParams(dimension_semantics=semantics, vmem_limit_bytes=VMEM_LIMIT)


def _layer_norm_rows(y, g, b):
    mu = jnp.mean(y, axis=-1, keepdims=True)
    d = y - mu
    var = jnp.mean(d * d, axis=-1, keepdims=True)
    return d * lax.rsqrt(var + LN_EPS) * g + b


def _memory_attention(mq, mk_t, mv_t):
    mqb = _memory_query(mq)
    out = _memory_head(mqb, mk_t, mv_t, 0)
    for h in range(1, MEM_HEADS):
        out = out + _memory_head(mqb, mk_t, mv_t, h)
    return out


def _memory_query(mq):
    return (mq * (MEM_HD ** -0.5 * math.log2(math.e))).astype(BF16)


def _memory_head(mqb, mk_t, mv_t, h):
    row = lax.broadcasted_iota(jnp.int32, mk_t.shape, 0)
    sel = (row >= h * MEM_HD) & (row < (h + 1) * MEM_HD)
    s = _dot(mqb, jnp.where(sel, mk_t, 0.0))
    p = jnp.exp2(s - jnp.max(s, axis=-1, keepdims=True))
    return _dot_nt(p, jnp.where(sel, mv_t, 0.0)) * (1.0 / jnp.sum(p, axis=-1, keepdims=True))


def _mem_kv_kernel(x_ref, w_ref, k_ref, v_ref):
    xb = x_ref[0].astype(BF16)
    for layer in range(DEPTH):
        kv_t = _dot_nt(w_ref[layer], xb)
        k_ref[layer, 0] = kv_t[:MEM_W]
        v_ref[layer, 0] = kv_t[MEM_W:]


def _mem_kv(mem, w_t_bf):
    b = mem.shape[0]
    out = jax.ShapeDtypeStruct((DEPTH, b, MEM_W, N_MEM), F32)
    return pl.pallas_call(
        _mem_kv_kernel,
        grid=(b,),
        in_specs=[pl.BlockSpec((1, N_MEM, D_MODEL), lambda i: (i, 0, 0)),
                  _const_spec((DEPTH, 2 * MEM_W, D_MODEL))],
        out_specs=[pl.BlockSpec((DEPTH, 1, MEM_W, N_MEM), lambda i: (0, i, 0, 0))] * 2,
        out_shape=[out, out],
        name="mem_kv",
        compiler_params=_params("arbitrary"),
    )(mem, w_t_bf)


def _ret_tables(tile):
    h = np.arange(RET_HEADS, dtype=np.float64)
    log_g = np.log(1.0 - np.exp2(-5.0 - h))
    idx = np.arange(tile, dtype=np.float64)
    dist = np.abs(idx[:, None] - idx[None, :])
    visible = (idx[None, :] // CHUNK) <= (idx[:, None] // CHUNK)
    decay = np.where(visible[None], np.exp(dist[None] * log_g[:, None, None]), 0.0)
    xi = np.exp((idx + 1.0)[None, :] * log_g[:, None])[:, :, None]
    zeta = np.exp((tile - 1.0 - idx)[None, :] * log_g[:, None])[:, :, None]
    g_tile = np.exp(tile * log_g)
    f = lambda a: jnp.asarray(a, F32)
    return f(decay), f(xi), f(zeta), [float(g) for g in g_tile]


def _ret_kernel(x_ref, w_ref, cos_ref, sin_ref, r0_ref, mk_ref, mv_ref, gn_ref, decay_ref, xi_ref,
                zeta_ref, o_ref, m_ref, r_ref, state_ref, *, tile, g_tile):
    step = pl.program_id(1)

    @pl.when(step == 0)
    def _():
        state_ref[...] = r0_ref[...]

    group, t, _ = x_ref.shape
    proj = _dot(x_ref[...].reshape(group * t, D_MODEL), w_ref[0])
    cos = cos_ref[...]
    sin = sin_ref[...]

    def rope(a):
        return a * cos + pltpu.roll(a, RET_HD // 2, 1) * sin

    for g in range(group):
        own = slice(g * t, (g + 1) * t)
        for h in range(RET_HEADS):
            col = h * HEAD_W
            q = rope(proj[own, col:col + HEAD_W])
            k = rope(proj[own, MIX_W + col:MIX_W + col + HEAD_W]) * RET_HD ** -0.5
            v = proj[own, 2 * MIX_W + col:2 * MIX_W + col + HEAD_W]
            gate = proj[own, 3 * MIX_W + col:3 * MIX_W + col + HEAD_W]
            outs = []
            r = state_ref[g, h]
            for c in range(t // tile):
                rows = slice(c * tile, (c + 1) * tile)
                qc, kc, vc = q[rows], k[rows], v[rows]
                inner = _dot_nt(qc, kc) * decay_ref[h]
                outs.append(_dot(inner, vc) + _dot(qc, r) * xi_ref[h])
                r = g_tile[h] * r + _dot_tn(kc * zeta_ref[h], vc)
            state_ref[g, h] = r
            o = outs[0] if len(outs) == 1 else jnp.concatenate(outs, axis=0)
            mu = jnp.mean(o, axis=-1, keepdims=True)
            d = o - mu
            var = jnp.mean(d * d, axis=-1, keepdims=True)
            o = d * lax.rsqrt(var + LN_EPS) * gn_ref[:, col:col + HEAD_W]
            o = o * (gate / (1.0 + jnp.exp(-gate)))
            o_ref[g, :, col:col + HEAD_W] = o.astype(o_ref.dtype)

        m = _memory_attention(proj[own, 4 * MIX_W:], mk_ref[0, g], mv_ref[0, g])
        m_ref[g] = m.astype(m_ref.dtype)

    @pl.when(step == pl.num_programs(1) - 1)
    def _():
        r_ref[...] = state_ref[...]


def _ret_mixer(x, pos, r0, mk_t, mv_t, layer, w_bf, w_layer, gn_g, *, t_step, tile, group):
    b, s, _ = x.shape
    half = RET_HD // 2
    lane_freq = jnp.arange(HEAD_W, dtype=jnp.int32) % half
    inv_freq = jnp.exp(-math.log(RET_THETA) * lane_freq.astype(F32) * 2.0 / RET_HD)
    ang = pos.astype(F32)[:, None] * inv_freq[None, :]
    cos = jnp.cos(ang)
    sin = jnp.where(jnp.arange(HEAD_W) < half, -jnp.sin(ang), jnp.sin(ang))
    decay, xi, zeta, g_tile = _ret_tables(tile)
    cols = w_bf.shape[-1]
    kern = functools.partial(_ret_kernel, tile=tile, g_tile=g_tile)
    mem_spec = pl.BlockSpec((1, group, MEM_W, N_MEM), lambda i, j: (layer, i, 0, 0))
    state_spec = pl.BlockSpec((group, RET_HEADS, RET_HD, RET_HD), lambda i, j: (i, 0, 0, 0))
    return pl.pallas_call(
        kern,
        grid=(b // group, s // t_step),
        in_specs=[
            pl.BlockSpec((group, t_step, D_MODEL), lambda i, j: (i, j, 0)),
            _layer_spec((D_MODEL, cols), w_layer),
            pl.BlockSpec((t_step, HEAD_W), lambda i, j: (j, 0)),
            pl.BlockSpec((t_step, HEAD_W), lambda i, j: (j, 0)),
            state_spec,
            mem_spec, mem_spec,
            _const_spec((1, MIX_W)),
            _const_spec((RET_HEADS, tile, tile)),
            _const_spec((RET_HEADS, tile, 1)),
            _const_spec((RET_HEADS, tile, 1)),
        ],
        out_specs=[
            pl.BlockSpec((group, t_step, MIX_W), lambda i, j: (i, j, 0)),
            pl.BlockSpec((group, t_step, MEM_W), lambda i, j: (i, j, 0)),
            state_spec,
        ],
        out_shape=[
            jax.ShapeDtypeStruct((b, s, MIX_W), BF16),
            jax.ShapeDtypeStruct((b, s, MEM_W), BF16),
            jax.ShapeDtypeStruct((b, RET_HEADS, RET_HD, RET_HD), F32),
        ],
        scratch_shapes=[pltpu.VMEM((group, RET_HEADS, RET_HD, RET_HD), F32)],
        name="ret_mixer",
        compiler_params=_params("arbitrary", "arbitrary"),
    )(x, w_bf, cos, sin, r0, mk_t, mv_t, gn_g.reshape(1, MIX_W), decay, xi, zeta)


def _diff_proj_kernel(x_ref, w_ref, c_ref, sa_ref, sb_ref, mk_ref, mv_ref, q_ref, k_ref, v_ref, m_ref):
    group, t, _ = x_ref.shape
    proj = _dot(x_ref[...].reshape(group * t, D_MODEL), w_ref[0])
    c = c_ref[...]
    sa = sa_ref[...]
    sb = sb_ref[...]

    def rope(a):
        return (a * c + pltpu.roll(a, ROT_DIM // 2, 1) * sa
                + pltpu.roll(a, HEAD_W - ROT_DIM // 2, 1) * sb)

    for g in range(group):
        own = slice(g * t, (g + 1) * t)
        for h in range(DIFF_HEADS):
            col = h * HEAD_W
            q = rope(proj[own, col:col + HEAD_W]) * Q_SCALE
            q_ref[g, h] = q.astype(q_ref.dtype)
            k_ref[g, h] = rope(proj[own, MIX_W + col:MIX_W + col + HEAD_W])
            v_ref[g, h] = proj[own, 2 * MIX_W + col:2 * MIX_W + col + HEAD_W]
        m = _memory_attention(proj[own, 3 * MIX_W:], mk_ref[0, g], mv_ref[0, g])
        m_ref[g] = m.astype(m_ref.dtype)


def _diff_project(x, pos, mk_t, mv_t, layer, w_bf, w_layer, *, t_step, group):
    b, s, _ = x.shape
    half = ROT_DIM // 2
    lane = jnp.arange(HEAD_W, dtype=jnp.int32) % DIFF_HD
    inv_freq = jnp.exp(-math.log(ROPE_THETA) * (lane % half).astype(F32) * 2.0 / ROT_DIM)
    ang = pos.astype(F32)[:, None] * inv_freq[None, :]
    cos, sin = jnp.cos(ang), jnp.sin(ang)
    tables = [jnp.where(lane < ROT_DIM, cos, 1.0),
              jnp.where((lane >= half) & (lane < ROT_DIM), sin, 0.0),
              jnp.where(lane < half, -sin, 0.0)]
    cols = w_bf.shape[-1]
    tab_spec = pl.BlockSpec((t_step, HEAD_W), lambda i, j: (j, 0))
    tok_spec = lambda w: pl.BlockSpec((group, t_step, w), lambda i, j: (i, j, 0))
    head_spec = pl.BlockSpec((group, DIFF_HEADS, t_step, HEAD_W), lambda i, j: (i, 0, j, 0))
    mem_spec = pl.BlockSpec((1, group, MEM_W, N_MEM), lambda i, j: (layer, i, 0, 0))
    heads = lambda dt: jax.ShapeDtypeStruct((b, DIFF_HEADS, s, HEAD_W), dt)
    return pl.pallas_call(
        _diff_proj_kernel,
        grid=(b // group, s // t_step),
        in_specs=[tok_spec(D_MODEL), _layer_spec((D_MODEL, cols), w_layer), tab_spec, tab_spec,
                  tab_spec, mem_spec, mem_spec],
        out_specs=[head_spec, head_spec, head_spec, tok_spec(MEM_W)],
        out_shape=[
            heads(BF16),
            heads(F32),
            heads(F32),
            jax.ShapeDtypeStruct((b, s, MEM_W), BF16),
        ],
        name="diff_proj",
        compiler_params=_params("arbitrary", "arbitrary"),
    )(x, w_bf, *tables, mk_t, mv_t)


def _diff_lambda(lq1_ref, lk1_ref, lq2_ref, lk2_ref, lam_init):
    a = jnp.sum(lq1_ref[...] * lk1_ref[...], axis=-1, keepdims=True)
    b = jnp.sum(lq2_ref[...] * lk2_ref[...], axis=-1, keepdims=True)
    return jnp.exp(a) - jnp.exp(b) + lam_init


def _diff_finish(o, g_ref, lam_init):
    ms = jnp.mean(o * o, axis=-1, keepdims=True)
    return o * lax.rsqrt(ms + LN_EPS) * g_ref[...] * (1.0 - lam_init)


def _split_heads(k):
    lane = lax.broadcasted_iota(jnp.int32, k.shape, 1)
    first = lane < DIFF_HD
    return jnp.where(first, k, 0.0).astype(BF16), jnp.where(first, 0.0, k).astype(BF16)


def _causal_tiles(n_k):
    return [(j, int(j == g), int(j == 0), int(j == g), g) for g in range(n_k) for j in range(g + 1)]


def _diff_attn_prompt_kernel(tab_ref, q_ref, k_ref, v_ref, lq1_ref, lk1_ref, lq2_ref, lk2_ref, g_ref,
                             o_ref, kx_ref, vt_ref, qt_ref, mask_ref, s_ref, p_ref, acc_ref, *,
                             tq, tk, heads, lam_init):
    seq = q_ref.shape[2]
    n_q, n_k = seq // tq, seq // tk
    q_per_k = tk // tq
    tiles = _causal_tiles(n_k)
    n_tiles = len(tiles)
    chunks_q, chunks_k = tq // CHUNK, tk // CHUNK
    streams = [(hd, qs, mp) for hd in range(heads) for qs in range(q_per_k) for mp in range(2)]

    @pl.when((pl.program_id(0) == 0) & (pl.program_id(1) == 0))
    def _():
        lane = lax.broadcasted_iota(jnp.int32, (seq, HEAD_W), 1)
        row_chunk = (lax.broadcasted_iota(jnp.int32, (seq, HEAD_W), 0) // CHUNK) % chunks_k
        chunk_one_hot = jnp.where(lane == row_chunk, 1.0, 0.0).astype(BF16)
        ones_row = jnp.where(lax.broadcasted_iota(jnp.int32, (SUM_ROWS, tk), 0) == 0, 1.0, 0.0).astype(BF16)
        for hd in range(heads):
            kx_ref[hd, 0, :, HEAD_W:] = chunk_one_hot
            kx_ref[hd, 1, :, HEAD_W:] = chunk_one_hot
            for t in range(n_k):
                vt_ref[hd, t, HEAD_W:, :] = ones_row
        bias_row = lax.broadcasted_iota(jnp.int32, (HEAD_W, tq), 0)
        bias_col = lax.broadcasted_iota(jnp.int32, (HEAD_W, tq), 1) // CHUNK
        mask_ref[0] = jnp.zeros((HEAD_W, tq), BF16)
        for qs in range(q_per_k):
            hidden = (bias_row < chunks_k) & (bias_row > bias_col + chunks_q * qs)
            mask_ref[1 + qs] = jnp.where(hidden, NEG_INF, 0.0).astype(BF16)
        acc_ref[...] = jnp.zeros(acc_ref.shape, F32)

    first = lax.broadcasted_iota(jnp.int32, (seq, HEAD_W), 1) < DIFF_HD
    for hd in range(heads):
        k = k_ref[0, hd]
        kx_ref[hd, 0, :, :HEAD_W] = jnp.where(first, k, 0.0).astype(BF16)
        kx_ref[hd, 1, :, :HEAD_W] = jnp.where(first, 0.0, k).astype(BF16)
        for t in range(n_k):
            vt_ref[hd, t, :HEAD_W, :] = v_ref[0, hd, t * tk:(t + 1) * tk, :].T.astype(BF16)
        for qi in range(n_q):
            qt_ref[hd, qi] = q_ref[0, hd, qi * tq:(qi + 1) * tq, :].astype(F32).T.astype(BF16)
    lam = _diff_lambda(lq1_ref, lk1_ref, lq2_ref, lk2_ref, lam_init)

    def tile_of(i):
        if isinstance(i, int):
            j, diag, is_first, is_last, grp = tiles[i]
            return j, diag, bool(is_first), bool(is_last), grp
        j, diag, is_first, is_last, grp = (tab_ref[i, f] for f in range(5))
        return j, diag, is_first == 1, is_last == 1, grp

    def stage_a(i, slot):
        j, diag, _, _, grp = tile_of(i)
        base = j * tk if isinstance(j, int) else pl.multiple_of(j * tk, tk)
        tops = []
        for hd, qs, mp in streams:
            q_op = jnp.concatenate([qt_ref[hd, grp * q_per_k + qs], mask_ref[diag * (1 + qs)]], axis=0)
            s = _dot(kx_ref[hd, mp, pl.ds(base, tk), :], q_op)
            s_ref[hd, slot, qs, mp] = s
            tops.append(jnp.max(s, axis=0, keepdims=True))
        return tuple(tops)

    def stage_b(i, slot, m, top):
        _, _, is_first, _, _ = tile_of(i)
        out = []
        for n, (hd, qs, mp) in enumerate(streams):
            m_prev = jnp.where(is_first, NEG_INF, m[n])
            m_new = jnp.maximum(m_prev, top[n])
            scale = jnp.exp2(m_prev - m_new)
            p_ref[hd, slot, qs, mp] = jnp.exp2(s_ref[hd, slot, qs, mp] - m_new).astype(BF16)
            out.append((m_new, scale))
        return tuple(zip(*out))

    def stage_c(i, slot, scale):
        j = tile_of(i)[0]
        for n, (hd, qs, mp) in enumerate(streams):
            acc_ref[hd, qs, mp] = (scale[n] * acc_ref[hd, qs, mp]
                                   + _dot(vt_ref[hd, j], p_ref[hd, slot, qs, mp]))

    def finish_if_last(i):
        _, _, _, is_last, grp = tile_of(i)

        def finish():
            for hd in range(heads):
                for qs in range(q_per_k):
                    w1 = 1.0 / acc_ref[hd, qs, 0, HEAD_W:HEAD_W + 1, :]
                    w2 = lam / acc_ref[hd, qs, 1, HEAD_W:HEAD_W + 1, :]
                    o_t = acc_ref[hd, qs, 0, :HEAD_W, :] * w1 - acc_ref[hd, qs, 1, :HEAD_W, :] * w2
                    ms = jnp.mean(o_t * o_t, axis=0, keepdims=True)
                    o = (o_t * (lax.rsqrt(ms + LN_EPS) * (1.0 - lam_init))).T * g_ref[...]
                    start = (grp * q_per_k + qs) * tq
                    rows = pl.ds(start if isinstance(start, int) else pl.multiple_of(start, tq), tq)
                    o_ref[0, rows, hd * HEAD_W:(hd + 1) * HEAD_W] = o.astype(o_ref.dtype)

        if isinstance(is_last, bool):
            if is_last:
                finish()
        else:
            pl.when(is_last)(finish)

    def iteration(i, slot, carry, a=True, b=True, c=True):
        m, top, scale = carry
        new_scale = scale
        if b:
            m, new_scale = stage_b(i - 1, 1 - slot, m, top)
        if a:
            top = stage_a(i, slot)
        if c:
            stage_c(i - 2, slot, scale)
            finish_if_last(i - 2)
        return m, top, new_scale

    def pair(n, carry):
        i = 2 + 2 * n
        return iteration(i + 1, 1, iteration(i, 0, carry))

    zero = (jnp.zeros((1, tq), F32),) * len(streams)
    carry = (zero, zero, zero)
    carry = iteration(0, 0, carry, b=False, c=False)
    carry = iteration(1, 1, carry, c=False)
    carry = lax.fori_loop(0, (n_tiles - 2) // 2, pair, carry)
    carry = iteration(n_tiles, 0, carry, a=False)
    iteration(n_tiles + 1, 1, carry, a=False, b=False)


def _diff_attn_prompt(q, k, v, lam_vecs, subln_g, lam_init, *, tq, tk, heads):
    b, _, s, _ = q.shape
    q_per_k = tk // tq
    rows = _causal_tiles(s // tk)
    assert len(rows) % 2 == 0
    table = jnp.asarray(np.array(rows, np.int32))
    kern = functools.partial(_diff_attn_prompt_kernel, tq=tq, tk=tk, heads=heads, lam_init=lam_init)
    const = lambda shape: pl.BlockSpec(shape, lambda i, h, tab: (0,) * len(shape))
    head_spec = pl.BlockSpec((1, heads, s, HEAD_W), lambda i, h, tab: (i, h, 0, 0))
    return pl.pallas_call(
        kern,
        grid_spec=pltpu.PrefetchScalarGridSpec(
            num_scalar_prefetch=1,
            grid=(b, DIFF_HEADS // heads),
            in_specs=[head_spec, head_spec, head_spec] + [const((1, DIFF_HD))] * 4 + [const((1, HEAD_W))],
            out_specs=pl.BlockSpec((1, s, heads * HEAD_W), lambda i, h, tab: (i, 0, h)),
            scratch_shapes=[
                pltpu.VMEM((heads, 2, s, 2 * HEAD_W), BF16),
                pltpu.VMEM((heads, s // tk, HEAD_W + SUM_ROWS, tk), BF16),
                pltpu.VMEM((heads, s // tq, HEAD_W, tq), BF16),
                pltpu.VMEM((1 + tk // tq, HEAD_W, tq), BF16),
                pltpu.VMEM((heads, 2, q_per_k, 2, tk, tq), F32),
                pltpu.VMEM((heads, 2, q_per_k, 2, tk, tq), BF16),
                pltpu.VMEM((heads, q_per_k, 2, HEAD_W + SUM_ROWS, tq), F32),
            ]),
        out_shape=jax.ShapeDtypeStruct((b, s, MIX_W), BF16),
        name="diff_attn_prompt",
        compiler_params=_params("arbitrary", "arbitrary"),
    )(table, q, k, v, *lam_vecs, subln_g.reshape(1, HEAD_W))


def _diff_attn_sample_kernel(q_ref, kc_ref, vc_ref, kn_ref, vn_ref, lq1_ref, lk1_ref, lq2_ref, lk2_ref,
                             g_ref, o_ref, *, lam_init):
    lam = _diff_lambda(lq1_ref, lk1_ref, lq2_ref, lk2_ref, lam_init)
    for h in range(q_ref.shape[1]):
        q = q_ref[0, h]
        s = q.shape[0]
        q1, q2 = _split_heads(q)
        qq = jnp.concatenate([q1, q2], axis=0)
        sc = _dot_nt(qq, kc_ref[0, h])
        sn = _dot_nt(qq, kn_ref[0, h])
        m = jnp.maximum(jnp.max(sc, axis=-1, keepdims=True), jnp.max(sn, axis=-1, keepdims=True))
        pc = jnp.exp2(sc - m)
        pn = jnp.exp2(sn - m)
        l = jnp.sum(pc, axis=-1, keepdims=True) + jnp.sum(pn, axis=-1, keepdims=True)
        o2 = (_dot(pc, vc_ref[0, h]) + _dot(pn, vn_ref[0, h])) / l
        o = o2[:s] - lam * o2[s:]
        o_ref[0, :, h * HEAD_W:(h + 1) * HEAD_W] = _diff_finish(o, g_ref, lam_init).astype(o_ref.dtype)


def _diff_attn_sample(q, k_cache, v_cache, k_new, v_new, lam_vecs, subln_g, lam_init):
    b, heads, s, _ = q.shape
    past = k_cache.shape[2]
    kern = functools.partial(_diff_attn_sample_kernel, lam_init=lam_init)
    vec_spec = _const_spec((1, DIFF_HD))
    new_spec = pl.BlockSpec((1, heads, s, HEAD_W), lambda i: (i, 0, 0, 0))
    old_spec = pl.BlockSpec((1, heads, past, HEAD_W), lambda i: (i, 0, 0, 0))
    return pl.pallas_call(
        kern,
        grid=(b,),
        in_specs=[new_spec, old_spec, old_spec, new_spec, new_spec, vec_spec, vec_spec, vec_spec,
                  vec_spec, _const_spec((1, HEAD_W))],
        out_specs=pl.BlockSpec((1, s, heads * HEAD_W), lambda i: (i, 0, 0)),
        out_shape=jax.ShapeDtypeStruct((b, s, MIX_W), BF16),
        name="diff_attn_sample",
        compiler_params=_params("arbitrary"),
    )(q, k_cache, v_cache, k_new, v_new, *lam_vecs, subln_g.reshape(1, HEAD_W))


def _post_kernel(x_ref, o_ref, m_ref, wo_ref, g1_ref, b1_ref, wg_ref, wu_ref, wd_ref, g2_ref, b2_ref,
                 y_ref, h_ref, *, ff_chunk, row_groups):
    tm = x_ref.shape[0]
    groups = [slice(r * tm // row_groups, (r + 1) * tm // row_groups) for r in range(row_groups)]
    x1, x1b = [], []
    for rows in groups:
        mixed = (_dot(o_ref[rows, :], wo_ref[0, :MIX_W, :])
                 + _dot(m_ref[rows, :], wo_ref[0, MIX_W:, :]))
        x1.append(_layer_norm_rows(ALPHA * x_ref[rows, :] + mixed, g1_ref[0], b1_ref[0]))
        x1b.append(x1[-1].astype(BF16))
    for c in range(D_FF // ff_chunk):
        cols = slice(c * ff_chunk, (c + 1) * ff_chunk)
        for r, rows in enumerate(groups):
            gate = _dot(x1b[r], wg_ref[0, :, cols])
            up = _dot(x1b[r], wu_ref[0, :, cols])
            h_ref[rows, cols] = (gate / (1.0 + jnp.exp(-gate)) * up).astype(BF16)
    for r, rows in enumerate(groups):
        ff = _dot(h_ref[rows, :], wd_ref[0])
        y_ref[rows, :] = _layer_norm_rows(ALPHA * x1[r] + ff, g2_ref[0], b2_ref[0])


def _post(x, o, m, layer, wo, g1, b1, wg, wu, wd, g2, b2, *, tm, ff_chunk, row_groups):
    n = x.shape[0]
    row = lambda w: pl.BlockSpec((tm, w), lambda i: (i, 0))
    vec = _layer_spec((1, D_MODEL), layer)
    kern = functools.partial(_post_kernel, ff_chunk=ff_chunk, row_groups=row_groups)
    r1 = lambda a: a.reshape(DEPTH, 1, D_MODEL)
    return pl.pallas_call(
        kern,
        grid=(n // tm,),
        in_specs=[row(D_MODEL), row(MIX_W), row(MEM_W), _layer_spec((D_MODEL, D_MODEL), layer), vec, vec,
                  _layer_spec((D_MODEL, D_FF), layer), _layer_spec((D_MODEL, D_FF), layer),
                  _layer_spec((D_FF, D_MODEL), layer), vec, vec],
        out_specs=row(D_MODEL),
        out_shape=jax.ShapeDtypeStruct((n, D_MODEL), F32),
        scratch_shapes=[pltpu.VMEM((tm, D_FF), BF16)],
        name="post_mixer",
        compiler_params=_params("arbitrary"),
    )(x, o, m, wo, r1(g1), r1(b1), wg, wu, wd, r1(g2), r1(b2))


def kernel(x_prompt, x_sample, mem_prompt, cache_ret_state, cache_diff_k, cache_diff_v, cache_mem_k,
           cache_mem_v, ret_w_in, ret_gn_g, diff_w_in, diff_lambda_q1, diff_lambda_k1, diff_lambda_q2,
           diff_lambda_k2, diff_subln_g, w_mem_kv, w_o, ln1_g, ln1_b, w_gate, w_up, w_down, ln2_g, ln2_b):
    bp, sp, _ = x_prompt.shape
    bs, ss, _ = x_sample.shape
    assert sp % MIXER_ROWS == 0 and MIXER_ROWS % RET_TILE == 0 and sp % ATTN_K_TILE == 0
    assert ss == CHUNK and bs % SAMPLE_GROUP == 0 and DIFF_HEADS % ATTN_HEADS == 0
    assert (bp * sp) % POST_ROWS == 0 and (bs * ss) % POST_ROWS == 0
    pos_p = jnp.arange(sp)
    pos_s = PAST_LEN + jnp.arange(ss)
    bf = lambda a: a.astype(BF16)

    mem_k_p, mem_v_p = _mem_kv(mem_prompt, bf(w_mem_kv.transpose(0, 2, 1)))
    mem_t = lambda a: a.transpose(0, 1, 3, 4, 2).reshape(DEPTH, bs, MEM_W, N_MEM)
    mem_k_s, mem_v_s = mem_t(cache_mem_k), mem_t(cache_mem_v)

    ret_w, diff_w = bf(ret_w_in), bf(diff_w_in)
    wo, wg, wu, wd = bf(w_o), bf(w_gate), bf(w_up), bf(w_down)

    xp, xs = x_prompt, x_sample
    ret_p, ret_s, dkp, dvp, dks, dvs = [], [], [], [], [], []
    to_seq_major = lambda a: a.transpose(0, 2, 1, 3)
    for i in range(DEPTH):
        j = i // 2
        if i % 2 == 0:
            r0 = jnp.zeros((bp, RET_HEADS, RET_HD, RET_HD), F32)
            op, mp, rp = _ret_mixer(xp, pos_p, r0, mem_k_p, mem_v_p, i, ret_w, j, ret_gn_g[j],
                                    t_step=MIXER_ROWS, tile=RET_TILE, group=1)
            os_, ms, rs = _ret_mixer(xs, pos_s, cache_ret_state[j], mem_k_s, mem_v_s, i, ret_w, j,
                                     ret_gn_g[j], t_step=ss, tile=ss, group=SAMPLE_GROUP)
            ret_p.append(rp)
            ret_s.append(rs)
        else:
            lam_init = _lambda_init(i)
            lam_vecs = [a[j].reshape(1, DIFF_HD) for a in
                        (diff_lambda_q1, diff_lambda_k1, diff_lambda_q2, diff_lambda_k2)]
            q, k, v, mp = _diff_project(xp, pos_p, mem_k_p, mem_v_p, i, diff_w, j, t_step=MIXER_ROWS,
                                        group=1)
            op = _diff_attn_prompt(q, k, v, lam_vecs, diff_subln_g[j], lam_init, tq=ATTN_Q_TILE,
                                   tk=ATTN_K_TILE, heads=ATTN_HEADS)
            dkp.append(to_seq_major(k))
            dvp.append(to_seq_major(v))
            q, k, v, ms = _diff_project(xs, pos_s, mem_k_s, mem_v_s, i, diff_w, j, t_step=ss,
                                        group=SAMPLE_GROUP)
            os_ = _diff_attn_sample(q, cache_diff_k[j].transpose(0, 2, 1, 3),
                                    cache_diff_v[j].transpose(0, 2, 1, 3), k, v, lam_vecs,
                                    diff_subln_g[j], lam_init)
            dks.append(to_seq_major(k))
            dvs.append(to_seq_major(v))
        post = functools.partial(_post, layer=i, wo=wo, g1=ln1_g, b1=ln1_b, wg=wg, wu=wu, wd=wd,
                                 g2=ln2_g, b2=ln2_b, tm=POST_ROWS, ff_chunk=FF_CHUNK,
                                 row_groups=POST_ROW_GROUPS)
        xp = post(xp.reshape(bp * sp, D_MODEL), op.reshape(bp * sp, MIX_W),
                  mp.reshape(bp * sp, MEM_W)).reshape(bp, sp, D_MODEL)
        xs = post(xs.reshape(bs * ss, D_MODEL), os_.reshape(bs * ss, MIX_W),
                  ms.reshape(bs * ss, MEM_W)).reshape(bs, ss, D_MODEL)

    mem_out = lambda a: a.reshape(DEPTH, bp, MEM_HEADS, MEM_HD, N_MEM).transpose(0, 1, 4, 2, 3)
    return (xp, xs, jnp.stack(ret_p), jnp.stack(ret_s), jnp.stack(dkp), jnp.stack(dvp),
            jnp.stack(dks), jnp.stack(dvs), mem_out(mem_k_p), mem_out(mem_v_p))
```

```python
import functools
import math

import jax
import jax.numpy as jnp
import numpy as np
from jax import lax
from jax.experimental import pallas as pl
from jax.experimental.pallas import tpu as pltpu

D_MODEL = 1024
DEPTH = 2
PAST_LEN = 1024
CHUNK = 64
N_MEM = 256
MEM_HEADS = 4
MEM_HD = 64
MEM_W = MEM_HEADS * MEM_HD
MIX_W = D_MODEL - MEM_W
RET_HEADS = 6
RET_HD = MIX_W // RET_HEADS
RET_THETA = 10000.0
DIFF_HEADS = 6
DIFF_HD = MIX_W // (2 * DIFF_HEADS)
ROPE_THETA = 500000.0
ROT_DIM = DIFF_HD // 4
D_FF = -(-8 * D_MODEL // (3 * 256)) * 256
ALPHA = (2 * DEPTH) ** 0.25
LN_EPS = 1e-5
NEG_INF = -1e30

Q_SCALE = DIFF_HD ** -0.5 * math.log2(math.e)
HEAD_W = 128
SUM_ROWS = 16

VMEM_LIMIT = 56 * 1024 * 1024
MIXER_ROWS = 1024
RET_TILE = 4 * CHUNK
SAMPLE_GROUP = 4
ATTN_Q_TILE = 256
ATTN_K_TILE = 512
ATTN_HEADS = 2
POST_ROWS = 1024
POST_ROW_GROUPS = 4
FF_CHUNK = 256

F32 = jnp.float32
BF16 = jnp.bfloat16


def _lambda_init(layer_idx):
    return 0.8 - 0.6 * math.exp(-0.3 * layer_idx)


def _dot(a, b):
    return jnp.dot(a.astype(BF16), b.astype(BF16), preferred_element_type=F32)


def _dot_nt(a, b):
    return lax.dot_general(a.astype(BF16), b.astype(BF16), (((1,), (1,)), ((), ())),
                           preferred_element_type=F32)


def _dot_tn(a, b):
    return lax.dot_general(a.astype(BF16), b.astype(BF16), (((0,), (0,)), ((), ())),
                           preferred_element_type=F32)


def _fixed_spec(shape, index):
    return pl.BlockSpec(shape, lambda *_: index, pipeline_mode=pl.Buffered(1))


def _const_spec(shape):
    return _fixed_spec(shape, (0,) * len(shape))


def _layer_spec(shape, layer):
    return _fixed_spec((1,) + shape, (layer,) + (0,) * len(shape))


def _params(*semantics):
    return pltpu.CompilerParams(dimension_semantics=semantics, vmem_limit_bytes=VMEM_LIMIT)


def _layer_norm_rows(y, g, b):
    mu = jnp.mean(y, axis=-1, keepdims=True)
    d = y - mu
    var = jnp.mean(d * d, axis=-1, keepdims=True)
    return d * lax.rsqrt(var + LN_EPS) * g + b


def _memory_attention(mq, mk_t, mv_t):
    mqb = _memory_query(mq)
    out = _memory_head(mqb, mk_t, mv_t, 0)
    for h in range(1, MEM_HEADS):
        out = out + _memory_head(mqb, mk_t, mv_t, h)
    return out


def _memory_query(mq):
    return (mq * (MEM_HD ** -0.5 * math.log2(math.e))).astype(BF16)


def _memory_head(mqb, mk_t, mv_t, h):
    row = lax.broadcasted_iota(jnp.int32, mk_t.shape, 0)
    sel = (row >= h * MEM_HD) & (row < (h + 1) * MEM_HD)
    s = _dot(mqb, jnp.where(sel, mk_t, 0.0))
    p = jnp.exp2(s - jnp.max(s, axis=-1, keepdims=True))
    return _dot_nt(p, jnp.where(sel, mv_t, 0.0)) * (1.0 / jnp.sum(p, axis=-1, keepdims=True))


def _mem_kv_kernel(x_ref, w_ref, k_ref, v_ref):
    xb = x_ref[0].astype(BF16)
    for layer in range(DEPTH):
        kv_t = _dot_nt(w_ref[layer], xb)
        k_ref[layer, 0] = kv_t[:MEM_W]
        v_ref[layer, 0] = kv_t[MEM_W:]


def _mem_kv(mem, w_t_bf):
    b = mem.shape[0]
    out = jax.ShapeDtypeStruct((DEPTH, b, MEM_W, N_MEM), F32)
    return pl.pallas_call(
        _mem_kv_kernel,
        grid=(b,),
        in_specs=[pl.BlockSpec((1, N_MEM, D_MODEL), lambda i: (i, 0, 0)),
                  _const_spec((DEPTH, 2 * MEM_W, D_MODEL))],
        out_specs=[pl.BlockSpec((DEPTH, 1, MEM_W, N_MEM), lambda i: (0, i, 0, 0))] * 2,
        out_shape=[out, out],
        name="mem_kv",
        compiler_params=_params("arbitrary"),
    )(mem, w_t_bf)


def _ret_tables(tile):
    h = np.arange(RET_HEADS, dtype=np.float64)
    log_g = np.log(1.0 - np.exp2(-5.0 - h))
    idx = np.arange(tile, dtype=np.float64)
    dist = np.abs(idx[:, None] - idx[None, :])
    visible = (idx[None, :] // CHUNK) <= (idx[:, None] // CHUNK)
    decay = np.where(visible[None], np.exp(dist[None] * log_g[:, None, None]), 0.0)
    xi = np.exp((idx + 1.0)[None, :] * log_g[:, None])[:, :, None]
    zeta = np.exp((tile - 1.0 - idx)[None, :] * log_g[:, None])[:, :, None]
    g_tile = np.exp(tile * log_g)
    f = lambda a: jnp.asarray(a, F32)
    return f(decay), f(xi), f(zeta), [float(g) for g in g_tile]


def _ret_kernel(x_ref, w_ref, cos_ref, sin_ref, r0_ref, mk_ref, mv_ref, gn_ref, decay_ref, xi_ref,
                zeta_ref, o_ref, m_ref, r_ref, state_ref, *, tile, g_tile):
    step = pl.program_id(1)

    @pl.when(step == 0)
    def _():
        state_ref[...] = r0_ref[...]

    group, t, _ = x_ref.shape
    proj = _dot(x_ref[...].reshape(group * t, D_MODEL), w_ref[0])
    cos = cos_ref[...]
    sin = sin_ref[...]

    def rope(a):
        return a * cos + pltpu.roll(a, RET_HD // 2, 1) * sin

    for g in range(group):
        own = slice(g * t, (g + 1) * t)
        for h in range(RET_HEADS):
            col = h * HEAD_W
            q = rope(proj[own, col:col + HEAD_W])
            k = rope(proj[own, MIX_W + col:MIX_W + col + HEAD_W]) * RET_HD ** -0.5
            v = proj[own, 2 * MIX_W + col:2 * MIX_W + col + HEAD_W]
            gate = proj[own, 3 * MIX_W + col:3 * MIX_W + col + HEAD_W]
            outs = []
            r = state_ref[g, h]
            for c in range(t // tile):
                rows = slice(c * tile, (c + 1) * tile)
                qc, kc, vc = q[rows], k[rows], v[rows]
                inner = _dot_nt(qc, kc) * decay_ref[h]
                outs.append(_dot(inner, vc) + _dot(qc, r) * xi_ref[h])
                r = g_tile[h] * r + _dot_tn(kc * zeta_ref[h], vc)
            state_ref[g, h] = r
            o = outs[0] if len(outs) == 1 else jnp.concatenate(outs, axis=0)
            mu = jnp.mean(o, axis=-1, keepdims=True)
            d = o - mu
            var = jnp.mean(d * d, axis=-1, keepdims=True)
            o = d * lax.rsqrt(var + LN_EPS) * gn_ref[:, col:col + HEAD_W]
            o = o * (gate / (1.0 + jnp.exp(-gate)))
            o_ref[g, :, col:col + HEAD_W] = o.astype(o_ref.dtype)

        m = _memory_attention(proj[own, 4 * MIX_W:], mk_ref[0, g], mv_ref[0, g])
        m_ref[g] = m.astype(m_ref.dtype)

    @pl.when(step == pl.num_programs(1) - 1)
    def _():
        r_ref[...] = state_ref[...]


def _ret_mixer(x, pos, r0, mk_t, mv_t, layer, w_bf, w_layer, gn_g, *, t_step, tile, group):
    b, s, _ = x.shape
    half = RET_HD // 2
    lane_freq = jnp.arange(HEAD_W, dtype=jnp.int32) % half
    inv_freq = jnp.exp(-math.log(RET_THETA) * lane_freq.astype(F32) * 2.0 / RET_HD)
    ang = pos.astype(F32)[:, None] * inv_freq[None, :]
    cos = jnp.cos(ang)
    sin = jnp.where(jnp.arange(HEAD_W) < half, -jnp.sin(ang), jnp.sin(ang))
    decay, xi, zeta, g_tile = _ret_tables(tile)
    cols = w_bf.shape[-1]
    kern = functools.partial(_ret_kernel, tile=tile, g_tile=g_tile)
    mem_spec = pl.BlockSpec((1, group, MEM_W, N_MEM), lambda i, j: (layer, i, 0, 0))
    state_spec = pl.BlockSpec((group, RET_HEADS, RET_HD, RET_HD), lambda i, j: (i, 0, 0, 0))
    return pl.pallas_call(
        kern,
        grid=(b // group, s // t_step),
        in_specs=[
            pl.BlockSpec((group, t_step, D_MODEL), lambda i, j: (i, j, 0)),
            _layer_spec((D_MODEL, cols), w_layer),
            pl.BlockSpec((t_step, HEAD_W), lambda i, j: (j, 0)),
            pl.BlockSpec((t_step, HEAD_W), lambda i, j: (j, 0)),
            state_spec,
            mem_spec, mem_spec,
            _const_spec((1, MIX_W)),
            _const_spec((RET_HEADS, tile, tile)),
            _const_spec((RET_HEADS, tile, 1)),
            _const_spec((RET_HEADS, tile, 1)),
        ],
        out_specs=[
            pl.BlockSpec((group, t_step, MIX_W), lambda i, j: (i, j, 0)),
            pl.BlockSpec((group, t_step, MEM_W), lambda i, j: (i, j, 0)),
            state_spec,
        ],
        out_shape=[
            jax.ShapeDtypeStruct((b, s, MIX_W), BF16),
            jax.ShapeDtypeStruct((b, s, MEM_W), BF16),
            jax.ShapeDtypeStruct((b, RET_HEADS, RET_HD, RET_HD), F32),
        ],
        scratch_shapes=[pltpu.VMEM((group, RET_HEADS, RET_HD, RET_HD), F32)],
        name="ret_mixer",
        compiler_params=_params("arbitrary", "arbitrary"),
    )(x, w_bf, cos, sin, r0, mk_t, mv_t, gn_g.reshape(1, MIX_W), decay, xi, zeta)


def _diff_proj_kernel(x_ref, w_ref, c_ref, sa_ref, sb_ref, mk_ref, mv_ref, q_ref, k_ref, v_ref, m_ref):
    group, t, _ = x_ref.shape
    proj = _dot(x_ref[...].reshape(group * t, D_MODEL), w_ref[0])
    c = c_ref[...]
    sa = sa_ref[...]
    sb = sb_ref[...]

    def rope(a):
        return (a * c + pltpu.roll(a, ROT_DIM // 2, 1) * sa
                + pltpu.roll(a, HEAD_W - ROT_DIM // 2, 1) * sb)

    for g in range(group):
        own = slice(g * t, (g + 1) * t)
        for h in range(DIFF_HEADS):
            col = h * HEAD_W
            q = rope(proj[own, col:col + HEAD_W]) * Q_SCALE
            q_ref[g, h] = q.astype(q_ref.dtype)
            k_ref[g, h] = rope(proj[own, MIX_W + col:MIX_W + col + HEAD_W])
            v_ref[g, h] = proj[own, 2 * MIX_W + col:2 * MIX_W + col + HEAD_W]
        m = _memory_attention(proj[own, 3 * MIX_W:], mk_ref[0, g], mv_ref[0, g])
        m_ref[g] = m.astype(m_ref.dtype)


def _diff_project(x, pos, mk_t, mv_t, layer, w_bf, w_layer, *, t_step, group):
    b, s, _ = x.shape
    half = ROT_DIM // 2
    lane = jnp.arange(HEAD_W, dtype=jnp.int32) % DIFF_HD
    inv_freq = jnp.exp(-math.log(ROPE_THETA) * (lane % half).astype(F32) * 2.0 / ROT_DIM)
    ang = pos.astype(F32)[:, None] * inv_freq[None, :]
    cos, sin = jnp.cos(ang), jnp.sin(ang)
    tables = [jnp.where(lane < ROT_DIM, cos, 1.0),
              jnp.where((lane >= half) & (lane < ROT_DIM), sin, 0.0),
              jnp.where(lane < half, -sin, 0.0)]
    cols = w_bf.shape[-1]
    tab_spec = pl.BlockSpec((t_step, HEAD_W), lambda i, j: (j, 0))
    tok_spec = lambda w: pl.BlockSpec((group, t_step, w), lambda i, j: (i, j, 0))
    head_spec = pl.BlockSpec((group, DIFF_HEADS, t_step, HEAD_W), lambda i, j: (i, 0, j, 0))
    mem_spec = pl.BlockSpec((1, group, MEM_W, N_MEM), lambda i, j: (layer, i, 0, 0))
    heads = lambda dt: jax.ShapeDtypeStruct((b, DIFF_HEADS, s, HEAD_W), dt)
    return pl.pallas_call(
        _diff_proj_kernel,
        grid=(b // group, s // t_step),
        in_specs=[tok_spec(D_MODEL), _layer_spec((D_MODEL, cols), w_layer), tab_spec, tab_spec,
                  tab_spec, mem_spec, mem_spec],
        out_specs=[head_spec, head_spec, head_spec, tok_spec(MEM_W)],
        out_shape=[
            heads(BF16),
            heads(F32),
            heads(F32),
            jax.ShapeDtypeStruct((b, s, MEM_W), BF16),
        ],
        name="diff_proj",
        compiler_params=_params("arbitrary", "arbitrary"),
    )(x, w_bf, *tables, mk_t, mv_t)


def _diff_lambda(lq1_ref, lk1_ref, lq2_ref, lk2_ref, lam_init):
    a = jnp.sum(lq1_ref[...] * lk1_ref[...], axis=-1, keepdims=True)
    b = jnp.sum(lq2_ref[...] * lk2_ref[...], axis=-1, keepdims=True)
    return jnp.exp(a) - jnp.exp(b) + lam_init


def _diff_finish(o, g_ref, lam_init):
    ms = jnp.mean(o * o, axis=-1, keepdims=True)
    return o * lax.rsqrt(ms + LN_EPS) * g_ref[...] * (1.0 - lam_init)


def _split_heads(k):
    lane = lax.broadcasted_iota(jnp.int32, k.shape, 1)
    first = lane < DIFF_HD
    return jnp.where(first, k, 0.0).astype(BF16), jnp.where(first, 0.0, k).astype(BF16)


def _causal_tiles(n_k):
    return [(j, int(j == g), int(j == 0), int(j == g), g) for g in range(n_k) for j in range(g + 1)]


def _diff_attn_prompt_kernel(tab_ref, q_ref, k_ref, v_ref, lq1_ref, lk1_ref, lq2_ref, lk2_ref, g_ref,
                             o_ref, kx_ref, vt_ref, qt_ref, mask_ref, s_ref, p_ref, acc_ref, *,
                             tq, tk, heads, lam_init):
    seq = q_ref.shape[2]
    n_q, n_k = seq // tq, seq // tk
    q_per_k = tk // tq
    tiles = _causal_tiles(n_k)
    n_tiles = len(tiles)
    chunks_q, chunks_k = tq // CHUNK, tk // CHUNK
    streams = [(hd, qs, mp) for hd in range(heads) for qs in range(q_per_k) for mp in range(2)]

    @pl.when((pl.program_id(0) == 0) & (pl.program_id(1) == 0))
    def _():
        lane = lax.broadcasted_iota(jnp.int32, (seq, HEAD_W), 1)
        row_chunk = (lax.broadcasted_iota(jnp.int32, (seq, HEAD_W), 0) // CHUNK) % chunks_k
        chunk_one_hot = jnp.where(lane == row_chunk, 1.0, 0.0).astype(BF16)
        ones_row = jnp.where(lax.broadcasted_iota(jnp.int32, (SUM_ROWS, tk), 0) == 0, 1.0, 0.0).astype(BF16)
        for hd in range(heads):
            kx_ref[hd, 0, :, HEAD_W:] = chunk_one_hot
            kx_ref[hd, 1, :, HEAD_W:] = chunk_one_hot
            for t in range(n_k):
                vt_ref[hd, t, HEAD_W:, :] = ones_row
        bias_row = lax.broadcasted_iota(jnp.int32, (HEAD_W, tq), 0)
        bias_col = lax.broadcasted_iota(jnp.int32, (HEAD_W, tq), 1) // CHUNK
        mask_ref[0] = jnp.zeros((HEAD_W, tq), BF16)
        for qs in range(q_per_k):
            hidden = (bias_row < chunks_k) & (bias_row > bias_col + chunks_q * qs)
            mask_ref[1 + qs] = jnp.where(hidden, NEG_INF, 0.0).astype(BF16)
        acc_ref[...] = jnp.zeros(acc_ref.shape, F32)

    first = lax.broadcasted_iota(jnp.int32, (seq, HEAD_W), 1) < DIFF_HD
    for hd in range(heads):
        k = k_ref[0, hd]
        kx_ref[hd, 0, :, :HEAD_W] = jnp.where(first, k, 0.0).astype(BF16)
        kx_ref[hd, 1, :, :HEAD_W] = jnp.where(first, 0.0, k).astype(BF16)
        for t in range(n_k):
            vt_ref[hd, t, :HEAD_W, :] = v_ref[0, hd, t * tk:(t + 1) * tk, :].T.astype(BF16)
        for qi in range(n_q):
            qt_ref[hd, qi] = q_ref[0, hd, qi * tq:(qi + 1) * tq, :].astype(F32).T.astype(BF16)
    lam = _diff_lambda(lq1_ref, lk1_ref, lq2_ref, lk2_ref, lam_init)

    def tile_of(i):
        if isinstance(i, int):
            j, diag, is_first, is_last, grp = tiles[i]
            return j, diag, bool(is_first), bool(is_last), grp
        j, diag, is_first, is_last, grp = (tab_ref[i, f] for f in range(5))
        return j, diag, is_first == 1, is_last == 1, grp

    def stage_a(i, slot):
        j, diag, _, _, grp = tile_of(i)
        base = j * tk if isinstance(j, int) else pl.multiple_of(j * tk, tk)
        tops = []
        for hd, qs, mp in streams:
            q_op = jnp.concatenate([qt_ref[hd, grp * q_per_k + qs], mask_ref[diag * (1 + qs)]], axis=0)
            s = _dot(kx_ref[hd, mp, pl.ds(base, tk), :], q_op)
            s_ref[hd, slot, qs, mp, :tk, :] = s
            tops.append(jnp.max(s, axis=0, keepdims=True))
        return tuple(tops)

    def stage_b(i, slot, m, top):
        _, _, is_first, _, _ = tile_of(i)
        out = []
        for n, (hd, qs, mp) in enumerate(streams):
            m_prev = jnp.where(is_first, NEG_INF, m[n])
            m_new = jnp.maximum(m_prev, top[n])
            scale = jnp.exp2(m_prev - m_new)
            p_ref[hd, slot, qs, mp, :tk, :] = jnp.exp2(s_ref[hd, slot, qs, mp, :tk, :] - m_new).astype(BF16)
            out.append((m_new, scale))
        return tuple(zip(*out))

    def stage_c(i, slot, scale):
        j = tile_of(i)[0]
        for n, (hd, qs, mp) in enumerate(streams):
            acc_ref[hd, qs, mp] = (scale[n] * acc_ref[hd, qs, mp]
                                   + _dot(vt_ref[hd, j], p_ref[hd, slot, qs, mp, :tk, :]))

    def finish_if_last(i):
        _, _, _, is_last, grp = tile_of(i)

        def finish():
            for hd in range(heads):
                for qs in range(q_per_k):
                    w1 = 1.0 / acc_ref[hd, qs, 0, HEAD_W:HEAD_W + 1, :]
                    w2 = lam / acc_ref[hd, qs, 1, HEAD_W:HEAD_W + 1, :]
                    o_t = acc_ref[hd, qs, 0, :HEAD_W, :] * w1 - acc_ref[hd, qs, 1, :HEAD_W, :] * w2
                    ms = jnp.mean(o_t * o_t, axis=0, keepdims=True)
                    o = (o_t * (lax.rsqrt(ms + LN_EPS) * (1.0 - lam_init))).T * g_ref[...]
                    start = (grp * q_per_k + qs) * tq
                    rows = pl.ds(start if isinstance(start, int) else pl.multiple_of(start, tq), tq)
                    o_ref[0, rows, hd * HEAD_W:(hd + 1) * HEAD_W] = o.astype(o_ref.dtype)

        if isinstance(is_last, bool):
            if is_last:
                finish()
        else:
            pl.when(is_last)(finish)

    def iteration(i, slot, carry, a=True, b=True, c=True):
        m, top, scale = carry
        new_scale = scale
        if b:
            m, new_scale = stage_b(i - 1, 1 - slot, m, top)
        if a:
            top = stage_a(i, slot)
        if c:
            stage_c(i - 2, slot, scale)
            finish_if_last(i - 2)
        return m, top, new_scale

    def pair(n, carry):
        i = 2 + 2 * n
        return iteration(i + 1, 1, iteration(i, 0, carry))

    zero = (jnp.zeros((1, tq), F32),) * len(streams)
    carry = (zero, zero, zero)
    carry = iteration(0, 0, carry, b=False, c=False)
    carry = iteration(1, 1, carry, c=False)
    carry = lax.fori_loop(0, (n_tiles - 2) // 2, pair, carry)
    carry = iteration(n_tiles, 0, carry, a=False)
    iteration(n_tiles + 1, 1, carry, a=False, b=False)


def _diff_attn_prompt(q, k, v, lam_vecs, subln_g, lam_init, *, tq, tk, heads):
    b, _, s, _ = q.shape
    q_per_k = tk // tq
    rows = _causal_tiles(s // tk)
    assert len(rows) % 2 == 0
    table = jnp.asarray(np.array(rows, np.int32))
    kern = functools.partial(_diff_attn_prompt_kernel, tq=tq, tk=tk, heads=heads, lam_init=lam_init)
    const = lambda shape: pl.BlockSpec(shape, lambda i, h, tab: (0,) * len(shape))
    head_spec = pl.BlockSpec((1, heads, s, HEAD_W), lambda i, h, tab: (i, h, 0, 0))
    return pl.pallas_call(
        kern,
        grid_spec=pltpu.PrefetchScalarGridSpec(
            num_scalar_prefetch=1,
            grid=(b, DIFF_HEADS // heads),
            in_specs=[head_spec, head_spec, head_spec] + [const((1, DIFF_HD))] * 4 + [const((1, HEAD_W))],
            out_specs=pl.BlockSpec((1, s, heads * HEAD_W), lambda i, h, tab: (i, 0, h)),
            scratch_shapes=[
                pltpu.VMEM((heads, 2, s, 2 * HEAD_W), BF16),
                pltpu.VMEM((heads, s // tk, HEAD_W + SUM_ROWS, tk), BF16),
                pltpu.VMEM((heads, s // tq, HEAD_W, tq), BF16),
                pltpu.VMEM((1 + tk // tq, HEAD_W, tq), BF16),
                pltpu.VMEM((heads, 2, q_per_k, 2, tk + 8, tq), F32),
                pltpu.VMEM((heads, 2, q_per_k, 2, tk + 16, tq), BF16),
                pltpu.VMEM((heads, q_per_k, 2, HEAD_W + SUM_ROWS, tq), F32),
            ]),
        out_shape=jax.ShapeDtypeStruct((b, s, MIX_W), BF16),
        name="diff_attn_prompt",
        compiler_params=_params("arbitrary", "arbitrary"),
    )(table, q, k, v, *lam_vecs, subln_g.reshape(1, HEAD_W))


def _diff_attn_sample_kernel(q_ref, kc_ref, vc_ref, kn_ref, vn_ref, lq1_ref, lk1_ref, lq2_ref, lk2_ref,
                             g_ref, o_ref, *, lam_init):
    lam = _diff_lambda(lq1_ref, lk1_ref, lq2_ref, lk2_ref, lam_init)
    for h in range(q_ref.shape[1]):
        q = q_ref[0, h]
        s = q.shape[0]
        q1, q2 = _split_heads(q)
        qq = jnp.concatenate([q1, q2], axis=0)
        sc = _dot_nt(qq, kc_ref[0, h])
        sn = _dot_nt(qq, kn_ref[0, h])
        m = jnp.maximum(jnp.max(sc, axis=-1, keepdims=True), jnp.max(sn, axis=-1, keepdims=True))
        pc = jnp.exp2(sc - m)
        pn = jnp.exp2(sn - m)
        l = jnp.sum(pc, axis=-1, keepdims=True) + jnp.sum(pn, axis=-1, keepdims=True)
        o2 = (_dot(pc, vc_ref[0, h]) + _dot(pn, vn_ref[0, h])) / l
        o = o2[:s] - lam * o2[s:]
        o_ref[0, :, h * HEAD_W:(h + 1) * HEAD_W] = _diff_finish(o, g_ref, lam_init).astype(o_ref.dtype)


def _diff_attn_sample(q, k_cache, v_cache, k_new, v_new, lam_vecs, subln_g, lam_init):
    b, heads, s, _ = q.shape
    past = k_cache.shape[2]
    kern = functools.partial(_diff_attn_sample_kernel, lam_init=lam_init)
    vec_spec = _const_spec((1, DIFF_HD))
    new_spec = pl.BlockSpec((1, heads, s, HEAD_W), lambda i: (i, 0, 0, 0))
    old_spec = pl.BlockSpec((1, heads, past, HEAD_W), lambda i: (i, 0, 0, 0))
    return pl.pallas_call(
        kern,
        grid=(b,),
        in_specs=[new_spec, old_spec, old_spec, new_spec, new_spec, vec_spec, vec_spec, vec_spec,
                  vec_spec, _const_spec((1, HEAD_W))],
        out_specs=pl.BlockSpec((1, s, heads * HEAD_W), lambda i: (i, 0, 0)),
        out_shape=jax.ShapeDtypeStruct((b, s, MIX_W), BF16),
        name="diff_attn_sample",
        compiler_params=_params("arbitrary"),
    )(q, k_cache, v_cache, k_new, v_new, *lam_vecs, subln_g.reshape(1, HEAD_W))


def _post_kernel(x_ref, o_ref, m_ref, wo_ref, g1_ref, b1_ref, wg_ref, wu_ref, wd_ref, g2_ref, b2_ref,
                 y_ref, h_ref, *, ff_chunk, row_groups):
    tm = x_ref.shape[0]
    groups = [slice(r * tm // row_groups, (r + 1) * tm // row_groups) for r in range(row_groups)]
    x1, x1b = [], []
    for rows in groups:
        mixed = (_dot(o_ref[rows, :], wo_ref[0, :MIX_W, :])
                 + _dot(m_ref[rows, :], wo_ref[0, MIX_W:, :]))
        x1.append(_layer_norm_rows(ALPHA * x_ref[rows, :] + mixed, g1_ref[0], b1_ref[0]))
        x1b.append(x1[-1].astype(BF16))
    for c in range(D_FF // ff_chunk):
        cols = slice(c * ff_chunk, (c + 1) * ff_chunk)
        for r, rows in enumerate(groups):
            gate = _dot(x1b[r], wg_ref[0, :, cols])
            up = _dot(x1b[r], wu_ref[0, :, cols])
            h_ref[rows, cols] = (gate / (1.0 + jnp.exp(-gate)) * up).astype(BF16)
    for r, rows in enumerate(groups):
        ff = _dot(h_ref[rows, :], wd_ref[0])
        y_ref[rows, :] = _layer_norm_rows(ALPHA * x1[r] + ff, g2_ref[0], b2_ref[0])


def _post(x, o, m, layer, wo, g1, b1, wg, wu, wd, g2, b2, *, tm, ff_chunk, row_groups):
    n = x.shape[0]
    row = lambda w: pl.BlockSpec((tm, w), lambda i: (i, 0))
    vec = _layer_spec((1, D_MODEL), layer)
    kern = functools.partial(_post_kernel, ff_chunk=ff_chunk, row_groups=row_groups)
    r1 = lambda a: a.reshape(DEPTH, 1, D_MODEL)
    return pl.pallas_call(
        kern,
        grid=(n // tm,),
        in_specs=[row(D_MODEL), row(MIX_W), row(MEM_W), _layer_spec((D_MODEL, D_MODEL), layer), vec, vec,
                  _layer_spec((D_MODEL, D_FF), layer), _layer_spec((D_MODEL, D_FF), layer),
                  _layer_spec((D_FF, D_MODEL), layer), vec, vec],
        out_specs=row(D_MODEL),
        out_shape=jax.ShapeDtypeStruct((n, D_MODEL), F32),
        scratch_shapes=[pltpu.VMEM((tm, D_FF), BF16)],
        name="post_mixer",
        compiler_params=_params("arbitrary"),
    )(x, o, m, wo, r1(g1), r1(b1), wg, wu, wd, r1(g2), r1(b2))


def kernel(x_prompt, x_sample, mem_prompt, cache_ret_state, cache_diff_k, cache_diff_v, cache_mem_k,
           cache_mem_v, ret_w_in, ret_gn_g, diff_w_in, diff_lambda_q1, diff_lambda_k1, diff_lambda_q2,
           diff_lambda_k2, diff_subln_g, w_mem_kv, w_o, ln1_g, ln1_b, w_gate, w_up, w_down, ln2_g, ln2_b):
    bp, sp, _ = x_prompt.shape
    bs, ss, _ = x_sample.shape
    assert sp % MIXER_ROWS == 0 and MIXER_ROWS % RET_TILE == 0 and sp % ATTN_K_TILE == 0
    assert ss == CHUNK and bs % SAMPLE_GROUP == 0 and DIFF_HEADS % ATTN_HEADS == 0
    assert (bp * sp) % POST_ROWS == 0 and (bs * ss) % POST_ROWS == 0
    pos_p = jnp.arange(sp)
    pos_s = PAST_LEN + jnp.arange(ss)
    bf = lambda a: a.astype(BF16)

    mem_k_p, mem_v_p = _mem_kv(mem_prompt, bf(w_mem_kv.transpose(0, 2, 1)))
    mem_t = lambda a: a.transpose(0, 1, 3, 4, 2).reshape(DEPTH, bs, MEM_W, N_MEM)
    mem_k_s, mem_v_s = mem_t(cache_mem_k), mem_t(cache_mem_v)

    ret_w, diff_w = bf(ret_w_in), bf(diff_w_in)
    wo, wg, wu, wd = bf(w_o), bf(w_gate), bf(w_up), bf(w_down)

    xp, xs = x_prompt, x_sample
    ret_p, ret_s, dkp, dvp, dks, dvs = [], [], [], [], [], []
    to_seq_major = lambda a: a.transpose(0, 2, 1, 3)
    for i in range(DEPTH):
        j = i // 2
        if i % 2 == 0:
            r0 = jnp.zeros((bp, RET_HEADS, RET_HD, RET_HD), F32)
            op, mp, rp = _ret_mixer(xp, pos_p, r0, mem_k_p, mem_v_p, i, ret_w, j, ret_gn_g[j],
                                    t_step=MIXER_ROWS, tile=RET_TILE, group=1)
            os_, ms, rs = _ret_mixer(xs, pos_s, cache_ret_state[j], mem_k_s, mem_v_s, i, ret_w, j,
                                     ret_gn_g[j], t_step=ss, tile=ss, group=SAMPLE_GROUP)
            ret_p.append(rp)
            ret_s.append(rs)
        else:
            lam_init = _lambda_init(i)
            lam_vecs = [a[j].reshape(1, DIFF_HD) for a in
                        (diff_lambda_q1, diff_lambda_k1, diff_lambda_q2, diff_lambda_k2)]
            q, k, v, mp = _diff_project(xp, pos_p, mem_k_p, mem_v_p, i, diff_w, j, t_step=MIXER_ROWS,
                                        group=1)
            op = _diff_attn_prompt(q, k, v, lam_vecs, diff_subln_g[j], lam_init, tq=ATTN_Q_TILE,
                                   tk=ATTN_K_TILE, heads=ATTN_HEADS)
            dkp.append(to_seq_major(k))
            dvp.append(to_seq_major(v))
            q, k, v, ms = _diff_project(xs, pos_s, mem_k_s, mem_v_s, i, diff_w, j, t_step=ss,
                                        group=SAMPLE_GROUP)
            os_ = _diff_attn_sample(q, cache_diff_k[j].transpose(0, 2, 1, 3),
                                    cache_diff_v[j].transpose(0, 2, 1, 3), k, v, lam_vecs,
                                    diff_subln_g[j], lam_init)
            dks.append(to_seq_major(k))
            dvs.append(to_seq_major(v))
        post = functools.partial(_post, layer=i, wo=wo, g1=ln1_g, b1=ln1_b, wg=wg, wu=wu, wd=wd,
                                 g2=ln2_g, b2=ln2_b, tm=POST_ROWS, ff_chunk=FF_CHUNK,
                                 row_groups=POST_ROW_GROUPS)
        xp = post(xp.reshape(bp * sp, D_MODEL), op.reshape(bp * sp, MIX_W),
                  mp.reshape(bp * sp, MEM_W)).reshape(bp, sp, D_MODEL)
        xs = post(xs.reshape(bs * ss, D_MODEL), os_.reshape(bs * ss, MIX_W),
                  ms.reshape(bs * ss, MEM_W)).reshape(bs, ss, D_MODEL)

    mem_out = lambda a: a.reshape(DEPTH, bp, MEM_HEADS, MEM_HD, N_MEM).transpose(0, 1, 4, 2, 3)
    return (xp, xs, jnp.stack(ret_p), jnp.stack(ret_s), jnp.stack(dkp), jnp.stack(dvp),
            jnp.stack(dks), jnp.stack(dvs), mem_out(mem_k_p), mem_out(mem_v_p))
```

```python
import functools
import math

import jax
import jax.numpy as jnp
import numpy as np
from jax import lax
from jax.experimental import pallas as pl
from jax.experimental.pallas import tpu as pltpu

D_MODEL = 1024
DEPTH = 2
PAST_LEN = 1024
CHUNK = 64
N_MEM = 256
MEM_HEADS = 4
MEM_HD = 64
MEM_W = MEM_HEADS * MEM_HD
MIX_W = D_MODEL - MEM_W
RET_HEADS = 6
RET_HD = MIX_W // RET_HEADS
RET_THETA = 10000.0
DIFF_HEADS = 6
DIFF_HD = MIX_W // (2 * DIFF_HEADS)
ROPE_THETA = 500000.0
ROT_DIM = DIFF_HD // 4
D_FF = -(-8 * D_MODEL // (3 * 256)) * 256
ALPHA = (2 * DEPTH) ** 0.25
LN_EPS = 1e-5
NEG_INF = -1e30

Q_SCALE = DIFF_HD ** -0.5 * math.log2(math.e)
HEAD_W = 128
SUM_ROWS = 16

VMEM_LIMIT = 56 * 1024 * 1024
MIXER_ROWS = 1024
RET_TILE = 4 * CHUNK
SAMPLE_GROUP = 4
ATTN_Q_TILE = 256
ATTN_K_TILE = 512
ATTN_HEADS = 2
POST_ROWS = 1024
POST_ROW_GROUPS = 4
FF_CHUNK = 256

F32 = jnp.float32
BF16 = jnp.bfloat16


def _lambda_init(layer_idx):
    return 0.8 - 0.6 * math.exp(-0.3 * layer_idx)


def _dot(a, b):
    return jnp.dot(a.astype(BF16), b.astype(BF16), preferred_element_type=F32)


def _dot_nt(a, b):
    return lax.dot_general(a.astype(BF16), b.astype(BF16), (((1,), (1,)), ((), ())),
                           preferred_element_type=F32)


def _dot_tn(a, b):
    return lax.dot_general(a.astype(BF16), b.astype(BF16), (((0,), (0,)), ((), ())),
                           preferred_element_type=F32)


def _fixed_spec(shape, index):
    return pl.BlockSpec(shape, lambda *_: index, pipeline_mode=pl.Buffered(1))


def _const_spec(shape):
    return _fixed_spec(shape, (0,) * len(shape))


def _layer_spec(shape, layer):
    return _fixed_spec((1,) + shape, (layer,) + (0,) * len(shape))


def _params(*semantics):
    return pltpu.CompilerParams(dimension_semantics=semantics, vmem_limit_bytes=VMEM_LIMIT)


def _layer_norm_rows(y, g, b):
    mu = jnp.mean(y, axis=-1, keepdims=True)
    d = y - mu
    var = jnp.mean(d * d, axis=-1, keepdims=True)
    return d * lax.rsqrt(var + LN_EPS) * g + b


def _memory_attention(mq, mk_t, mv_t):
    mqb = _memory_query(mq)
    out = _memory_head(mqb, mk_t, mv_t, 0)
    for h in range(1, MEM_HEADS):
        out = out + _memory_head(mqb, mk_t, mv_t, h)
    return out


def _memory_query(mq):
    return (mq * (MEM_HD ** -0.5 * math.log2(math.e))).astype(BF16)


def _memory_head(mqb, mk_t, mv_t, h):
    row = lax.broadcasted_iota(jnp.int32, mk_t.shape, 0)
    sel = (row >= h * MEM_HD) & (row < (h + 1) * MEM_HD)
    s = _dot(mqb, jnp.where(sel, mk_t, 0.0))
    p = jnp.exp2(s - jnp.max(s, axis=-1, keepdims=True))
    return _dot_nt(p, jnp.where(sel, mv_t, 0.0)) * (1.0 / jnp.sum(p, axis=-1, keepdims=True))


def _mem_kv_kernel(x_ref, w_ref, k_ref, v_ref):
    xb = x_ref[0].astype(BF16)
    for layer in range(DEPTH):
        kv_t = _dot_nt(w_ref[layer], xb)
        k_ref[layer, 0] = kv_t[:MEM_W]
        v_ref[layer, 0] = kv_t[MEM_W:]


def _mem_kv(mem, w_t_bf):
    b = mem.shape[0]
    out = jax.ShapeDtypeStruct((DEPTH, b, MEM_W, N_MEM), F32)
    return pl.pallas_call(
        _mem_kv_kernel,
        grid=(b,),
        in_specs=[pl.BlockSpec((1, N_MEM, D_MODEL), lambda i: (i, 0, 0)),
                  _const_spec((DEPTH, 2 * MEM_W, D_MODEL))],
        out_specs=[pl.BlockSpec((DEPTH, 1, MEM_W, N_MEM), lambda i: (0, i, 0, 0))] * 2,
        out_shape=[out, out],
        name="mem_kv",
        compiler_params=_params("arbitrary"),
    )(mem, w_t_bf)


def _ret_tables(tile):
    h = np.arange(RET_HEADS, dtype=np.float64)
    log_g = np.log(1.0 - np.exp2(-5.0 - h))
    idx = np.arange(tile, dtype=np.float64)
    dist = np.abs(idx[:, None] - idx[None, :])
    visible = (idx[None, :] // CHUNK) <= (idx[:, None] // CHUNK)
    decay = np.where(visible[None], np.exp(dist[None] * log_g[:, None, None]), 0.0)
    xi = np.exp((idx + 1.0)[None, :] * log_g[:, None])[:, :, None]
    zeta = np.exp((tile - 1.0 - idx)[None, :] * log_g[:, None])[:, :, None]
    g_tile = np.exp(tile * log_g)
    f = lambda a: jnp.asarray(a, F32)
    return f(decay), f(xi), f(zeta), [float(g) for g in g_tile]


def _ret_kernel(x_ref, w_ref, cos_ref, sin_ref, r0_ref, mk_ref, mv_ref, gn_ref, decay_ref, xi_ref,
                zeta_ref, o_ref, m_ref, r_ref, state_ref, *, tile, g_tile):
    step = pl.program_id(1)

    @pl.when(step == 0)
    def _():
        state_ref[...] = r0_ref[...]

    group, t, _ = x_ref.shape
    proj = _dot(x_ref[...].reshape(group * t, D_MODEL), w_ref[0])
    cos = cos_ref[...]
    sin = sin_ref[...]

    def rope(a):
        return a * cos + pltpu.roll(a, RET_HD // 2, 1) * sin

    for g in range(group):
        own = slice(g * t, (g + 1) * t)
        for h in range(RET_HEADS):
            col = h * HEAD_W
            q = rope(proj[own, col:col + HEAD_W])
            k = rope(proj[own, MIX_W + col:MIX_W + col + HEAD_W]) * RET_HD ** -0.5
            v = proj[own, 2 * MIX_W + col:2 * MIX_W + col + HEAD_W]
            gate = proj[own, 3 * MIX_W + col:3 * MIX_W + col + HEAD_W]
            outs = []
            r = state_ref[g, h]
            for c in range(t // tile):
                rows = slice(c * tile, (c + 1) * tile)
                qc, kc, vc = q[rows], k[rows], v[rows]
                inner = _dot_nt(qc, kc) * decay_ref[h]
                outs.append(_dot(inner, vc) + _dot(qc, r) * xi_ref[h])
                r = g_tile[h] * r + _dot_tn(kc * zeta_ref[h], vc)
            state_ref[g, h] = r
            o = outs[0] if len(outs) == 1 else jnp.concatenate(outs, axis=0)
            mu = jnp.mean(o, axis=-1, keepdims=True)
            d = o - mu
            var = jnp.mean(d * d, axis=-1, keepdims=True)
            o = d * lax.rsqrt(var + LN_EPS) * gn_ref[:, col:col + HEAD_W]
            o = o * (gate / (1.0 + jnp.exp(-gate)))
            o_ref[g, :, col:col + HEAD_W] = o.astype(o_ref.dtype)

        m = _memory_attention(proj[own, 4 * MIX_W:], mk_ref[0, g], mv_ref[0, g])
        m_ref[g] = m.astype(m_ref.dtype)

    @pl.when(step == pl.num_programs(1) - 1)
    def _():
        r_ref[...] = state_ref[...]


def _ret_mixer(x, pos, r0, mk_t, mv_t, layer, w_bf, w_layer, gn_g, *, t_step, tile, group):
    b, s, _ = x.shape
    half = RET_HD // 2
    lane_freq = jnp.arange(HEAD_W, dtype=jnp.int32) % half
    inv_freq = jnp.exp(-math.log(RET_THETA) * lane_freq.astype(F32) * 2.0 / RET_HD)
    ang = pos.astype(F32)[:, None] * inv_freq[None, :]
    cos = jnp.cos(ang)
    sin = jnp.where(jnp.arange(HEAD_W) < half, -jnp.sin(ang), jnp.sin(ang))
    decay, xi, zeta, g_tile = _ret_tables(tile)
    cols = w_bf.shape[-1]
    kern = functools.partial(_ret_kernel, tile=tile, g_tile=g_tile)
    mem_spec = pl.BlockSpec((1, group, MEM_W, N_MEM), lambda i, j: (layer, i, 0, 0))
    state_spec = pl.BlockSpec((group, RET_HEADS, RET_HD, RET_HD), lambda i, j: (i, 0, 0, 0))
    return pl.pallas_call(
        kern,
        grid=(b // group, s // t_step),
        in_specs=[
            pl.BlockSpec((group, t_step, D_MODEL), lambda i, j: (i, j, 0)),
            _layer_spec((D_MODEL, cols), w_layer),
            pl.BlockSpec((t_step, HEAD_W), lambda i, j: (j, 0)),
            pl.BlockSpec((t_step, HEAD_W), lambda i, j: (j, 0)),
            state_spec,
            mem_spec, mem_spec,
            _const_spec((1, MIX_W)),
            _const_spec((RET_HEADS, tile, tile)),
            _const_spec((RET_HEADS, tile, 1)),
            _const_spec((RET_HEADS, tile, 1)),
        ],
        out_specs=[
            pl.BlockSpec((group, t_step, MIX_W), lambda i, j: (i, j, 0)),
            pl.BlockSpec((group, t_step, MEM_W), lambda i, j: (i, j, 0)),
            state_spec,
        ],
        out_shape=[
            jax.ShapeDtypeStruct((b, s, MIX_W), BF16),
            jax.ShapeDtypeStruct((b, s, MEM_W), BF16),
            jax.ShapeDtypeStruct((b, RET_HEADS, RET_HD, RET_HD), F32),
        ],
        scratch_shapes=[pltpu.VMEM((group, RET_HEADS, RET_HD, RET_HD), F32)],
        name="ret_mixer",
        compiler_params=_params("arbitrary", "arbitrary"),
    )(x, w_bf, cos, sin, r0, mk_t, mv_t, gn_g.reshape(1, MIX_W), decay, xi, zeta)


def _diff_proj_kernel(x_ref, w_ref, c_ref, sa_ref, sb_ref, mk_ref, mv_ref, q_ref, k_ref, v_ref, m_ref):
    group, t, _ = x_ref.shape
    proj = _dot(x_ref[...].reshape(group * t, D_MODEL), w_ref[0])
    c = c_ref[...]
    sa = sa_ref[...]
    sb = sb_ref[...]

    def rope(a):
        return (a * c + pltpu.roll(a, ROT_DIM // 2, 1) * sa
                + pltpu.roll(a, HEAD_W - ROT_DIM // 2, 1) * sb)

    for g in range(group):
        own = slice(g * t, (g + 1) * t)
        for h in range(DIFF_HEADS):
            col = h * HEAD_W
            q = rope(proj[own, col:col + HEAD_W]) * Q_SCALE
            q_ref[g, h] = q.astype(q_ref.dtype)
            k_ref[g, h] = rope(proj[own, MIX_W + col:MIX_W + col + HEAD_W])
            v_ref[g, h] = proj[own, 2 * MIX_W + col:2 * MIX_W + col + HEAD_W]
        m = _memory_attention(proj[own, 3 * MIX_W:], mk_ref[0, g], mv_ref[0, g])
        m_ref[g] = m.astype(m_ref.dtype)


def _diff_project(x, pos, mk_t, mv_t, layer, w_bf, w_layer, *, t_step, group):
    b, s, _ = x.shape
    half = ROT_DIM // 2
    lane = jnp.arange(HEAD_W, dtype=jnp.int32) % DIFF_HD
    inv_freq = jnp.exp(-math.log(ROPE_THETA) * (lane % half).astype(F32) * 2.0 / ROT_DIM)
    ang = pos.astype(F32)[:, None] * inv_freq[None, :]
    cos, sin = jnp.cos(ang), jnp.sin(ang)
    tables = [jnp.where(lane < ROT_DIM, cos, 1.0),
              jnp.where((lane >= half) & (lane < ROT_DIM), sin, 0.0),
              jnp.where(lane < half, -sin, 0.0)]
    cols = w_bf.shape[-1]
    tab_spec = pl.BlockSpec((t_step, HEAD_W), lambda i, j: (j, 0))
    tok_spec = lambda w: pl.BlockSpec((group, t_step, w), lambda i, j: (i, j, 0))
    head_spec = pl.BlockSpec((group, DIFF_HEADS, t_step, HEAD_W), lambda i, j: (i, 0, j, 0))
    mem_spec = pl.BlockSpec((1, group, MEM_W, N_MEM), lambda i, j: (layer, i, 0, 0))
    heads = lambda dt: jax.ShapeDtypeStruct((b, DIFF_HEADS, s, HEAD_W), dt)
    return pl.pallas_call(
        _diff_proj_kernel,
        grid=(b // group, s // t_step),
        in_specs=[tok_spec(D_MODEL), _layer_spec((D_MODEL, cols), w_layer), tab_spec, tab_spec,
                  tab_spec, mem_spec, mem_spec],
        out_specs=[head_spec, head_spec, head_spec, tok_spec(MEM_W)],
        out_shape=[
            heads(BF16),
            heads(F32),
            heads(F32),
            jax.ShapeDtypeStruct((b, s, MEM_W), BF16),
        ],
        name="diff_proj",
        compiler_params=_params("arbitrary", "arbitrary"),
    )(x, w_bf, *tables, mk_t, mv_t)


def _diff_lambda(lq1_ref, lk1_ref, lq2_ref, lk2_ref, lam_init):
    a = jnp.sum(lq1_ref[...] * lk1_ref[...], axis=-1, keepdims=True)
    b = jnp.sum(lq2_ref[...] * lk2_ref[...], axis=-1, keepdims=True)
    return jnp.exp(a) - jnp.exp(b) + lam_init


def _diff_finish(o, g_ref, lam_init):
    ms = jnp.mean(o * o, axis=-1, keepdims=True)
    return o * lax.rsqrt(ms + LN_EPS) * g_ref[...] * (1.0 - lam_init)


def _split_heads(k):
    lane = lax.broadcasted_iota(jnp.int32, k.shape, 1)
    first = lane < DIFF_HD
    return jnp.where(first, k, 0.0).astype(BF16), jnp.where(first, 0.0, k).astype(BF16)


def _causal_tiles(n_k):
    return [(j, int(j == g), int(j == 0), int(j == g), g) for g in range(n_k) for j in range(g + 1)]


def _diff_attn_prompt_kernel(tab_ref, q_ref, k_ref, v_ref, lq1_ref, lk1_ref, lq2_ref, lk2_ref, g_ref,
                             o_ref, kx_ref, vt_ref, qt_ref, mask_ref, s_ref, p_ref, acc_ref, *,
                             tq, tk, heads, lam_init):
    seq = q_ref.shape[2]
    n_q, n_k = seq // tq, seq // tk
    q_per_k = tk // tq
    tiles = _causal_tiles(n_k)
    n_tiles = len(tiles)
    chunks_q, chunks_k = tq // CHUNK, tk // CHUNK
    streams = [(hd, qs, mp) for hd in range(heads) for qs in range(q_per_k) for mp in range(2)]

    @pl.when((pl.program_id(0) == 0) & (pl.program_id(1) == 0))
    def _():
        lane = lax.broadcasted_iota(jnp.int32, (seq, HEAD_W), 1)
        row_chunk = (lax.broadcasted_iota(jnp.int32, (seq, HEAD_W), 0) // CHUNK) % chunks_k
        chunk_one_hot = jnp.where(lane == row_chunk, 1.0, 0.0).astype(BF16)
        ones_row = jnp.where(lax.broadcasted_iota(jnp.int32, (SUM_ROWS, tk), 0) == 0, 1.0, 0.0).astype(BF16)
        for hd in range(heads):
            kx_ref[hd, 0, :seq, HEAD_W:] = chunk_one_hot
            kx_ref[hd, 1, :seq, HEAD_W:] = chunk_one_hot
            for t in range(n_k):
                vt_ref[hd, t, HEAD_W:, :] = ones_row
        bias_row = lax.broadcasted_iota(jnp.int32, (HEAD_W, tq), 0)
        bias_col = lax.broadcasted_iota(jnp.int32, (HEAD_W, tq), 1) // CHUNK
        mask_ref[0] = jnp.zeros((HEAD_W, tq), BF16)
        for qs in range(q_per_k):
            hidden = (bias_row < chunks_k) & (bias_row > bias_col + chunks_q * qs)
            mask_ref[1 + qs] = jnp.where(hidden, NEG_INF, 0.0).astype(BF16)
        acc_ref[...] = jnp.zeros(acc_ref.shape, F32)

    first = lax.broadcasted_iota(jnp.int32, (seq, HEAD_W), 1) < DIFF_HD
    for hd in range(heads):
        k = k_ref[0, hd]
        kx_ref[hd, 0, :seq, :HEAD_W] = jnp.where(first, k, 0.0).astype(BF16)
        kx_ref[hd, 1, :seq, :HEAD_W] = jnp.where(first, 0.0, k).astype(BF16)
        for t in range(n_k):
            vt_ref[hd, t, :HEAD_W, :] = v_ref[0, hd, t * tk:(t + 1) * tk, :].T.astype(BF16)
        for qi in range(n_q):
            qt_ref[hd, qi] = q_ref[0, hd, qi * tq:(qi + 1) * tq, :].astype(F32).T.astype(BF16)
    lam = _diff_lambda(lq1_ref, lk1_ref, lq2_ref, lk2_ref, lam_init)

    def tile_of(i):
        if isinstance(i, int):
            j, diag, is_first, is_last, grp = tiles[i]
            return j, diag, bool(is_first), bool(is_last), grp
        j, diag, is_first, is_last, grp = (tab_ref[i, f] for f in range(5))
        return j, diag, is_first == 1, is_last == 1, grp

    def stage_a(i, slot):
        j, diag, _, _, grp = tile_of(i)
        base = j * tk if isinstance(j, int) else pl.multiple_of(j * tk, tk)
        tops = []
        for hd, qs, mp in streams:
            q_op = jnp.concatenate([qt_ref[hd, grp * q_per_k + qs], mask_ref[diag * (1 + qs)]], axis=0)
            s = _dot(kx_ref[hd, mp, pl.ds(base, tk), :], q_op)
            s_ref[hd, slot, qs, mp, :tk, :] = s
            tops.append(jnp.max(s, axis=0, keepdims=True))
        return tuple(tops)

    def stage_b(i, slot, m, top):
        _, _, is_first, _, _ = tile_of(i)
        out = []
        for n, (hd, qs, mp) in enumerate(streams):
            m_prev = jnp.where(is_first, NEG_INF, m[n])
            m_new = jnp.maximum(m_prev, top[n])
            scale = jnp.exp2(m_prev - m_new)
            p_ref[hd, slot, qs, mp, :tk, :] = jnp.exp2(s_ref[hd, slot, qs, mp, :tk, :] - m_new).astype(BF16)
            out.append((m_new, scale))
        return tuple(zip(*out))

    def stage_c(i, slot, scale):
        j = tile_of(i)[0]
        for n, (hd, qs, mp) in enumerate(streams):
            acc_ref[hd, qs, mp] = (scale[n] * acc_ref[hd, qs, mp]
                                   + _dot(vt_ref[hd, j], p_ref[hd, slot, qs, mp, :tk, :]))

    def finish_if_last(i):
        _, _, _, is_last, grp = tile_of(i)

        def finish():
            for hd in range(heads):
                for qs in range(q_per_k):
                    w1 = 1.0 / acc_ref[hd, qs, 0, HEAD_W:HEAD_W + 1, :]
                    w2 = lam / acc_ref[hd, qs, 1, HEAD_W:HEAD_W + 1, :]
                    o_t = acc_ref[hd, qs, 0, :HEAD_W, :] * w1 - acc_ref[hd, qs, 1, :HEAD_W, :] * w2
                    ms = jnp.mean(o_t * o_t, axis=0, keepdims=True)
                    o = (o_t * (lax.rsqrt(ms + LN_EPS) * (1.0 - lam_init))).T * g_ref[...]
                    start = (grp * q_per_k + qs) * tq
                    rows = pl.ds(start if isinstance(start, int) else pl.multiple_of(start, tq), tq)
                    o_ref[0, rows, hd * HEAD_W:(hd + 1) * HEAD_W] = o.astype(o_ref.dtype)

        if isinstance(is_last, bool):
            if is_last:
                finish()
        else:
            pl.when(is_last)(finish)

    def iteration(i, slot, carry, a=True, b=True, c=True):
        m, top, scale = carry
        new_scale = scale
        if b:
            m, new_scale = stage_b(i - 1, 1 - slot, m, top)
        if a:
            top = stage_a(i, slot)
        if c:
            stage_c(i - 2, slot, scale)
            finish_if_last(i - 2)
        return m, top, new_scale

    def pair(n, carry):
        i = 2 + 2 * n
        return iteration(i + 1, 1, iteration(i, 0, carry))

    zero = (jnp.zeros((1, tq), F32),) * len(streams)
    carry = (zero, zero, zero)
    carry = iteration(0, 0, carry, b=False, c=False)
    carry = iteration(1, 1, carry, c=False)
    carry = lax.fori_loop(0, (n_tiles - 2) // 2, pair, carry)
    carry = iteration(n_tiles, 0, carry, a=False)
    iteration(n_tiles + 1, 1, carry, a=False, b=False)


def _diff_attn_prompt(q, k, v, lam_vecs, subln_g, lam_init, *, tq, tk, heads):
    b, _, s, _ = q.shape
    q_per_k = tk // tq
    rows = _causal_tiles(s // tk)
    assert len(rows) % 2 == 0
    table = jnp.asarray(np.array(rows, np.int32))
    kern = functools.partial(_diff_attn_prompt_kernel, tq=tq, tk=tk, heads=heads, lam_init=lam_init)
    const = lambda shape: pl.BlockSpec(shape, lambda i, h, tab: (0,) * len(shape))
    head_spec = pl.BlockSpec((1, heads, s, HEAD_W), lambda i, h, tab: (i, h, 0, 0))
    return pl.pallas_call(
        kern,
        grid_spec=pltpu.PrefetchScalarGridSpec(
            num_scalar_prefetch=1,
            grid=(b, DIFF_HEADS // heads),
            in_specs=[head_spec, head_spec, head_spec] + [const((1, DIFF_HD))] * 4 + [const((1, HEAD_W))],
            out_specs=pl.BlockSpec((1, s, heads * HEAD_W), lambda i, h, tab: (i, 0, h)),
            scratch_shapes=[
                pltpu.VMEM((heads, 2, s + 16, 2 * HEAD_W), BF16),
                pltpu.VMEM((heads, s // tk, HEAD_W + SUM_ROWS, tk), BF16),
                pltpu.VMEM((heads, s // tq, HEAD_W, tq), BF16),
                pltpu.VMEM((1 + tk // tq, HEAD_W, tq), BF16),
                pltpu.VMEM((heads, 2, q_per_k, 2, tk + 8, tq), F32),
                pltpu.VMEM((heads, 2, q_per_k, 2, tk + 16, tq), BF16),
                pltpu.VMEM((heads, q_per_k, 2, HEAD_W + SUM_ROWS, tq), F32),
            ]),
        out_shape=jax.ShapeDtypeStruct((b, s, MIX_W), BF16),
        name="diff_attn_prompt",
        compiler_params=_params("arbitrary", "arbitrary"),
    )(table, q, k, v, *lam_vecs, subln_g.reshape(1, HEAD_W))


def _diff_attn_sample_kernel(q_ref, kc_ref, vc_ref, kn_ref, vn_ref, lq1_ref, lk1_ref, lq2_ref, lk2_ref,
                             g_ref, o_ref, *, lam_init):
    lam = _diff_lambda(lq1_ref, lk1_ref, lq2_ref, lk2_ref, lam_init)
    for h in range(q_ref.shape[1]):
        q = q_ref[0, h]
        s = q.shape[0]
        q1, q2 = _split_heads(q)
        qq = jnp.concatenate([q1, q2], axis=0)
        sc = _dot_nt(qq, kc_ref[0, h])
        sn = _dot_nt(qq, kn_ref[0, h])
        m = jnp.maximum(jnp.max(sc, axis=-1, keepdims=True), jnp.max(sn, axis=-1, keepdims=True))
        pc = jnp.exp2(sc - m)
        pn = jnp.exp2(sn - m)
        l = jnp.sum(pc, axis=-1, keepdims=True) + jnp.sum(pn, axis=-1, keepdims=True)
        o2 = (_dot(pc, vc_ref[0, h]) + _dot(pn, vn_ref[0, h])) / l
        o = o2[:s] - lam * o2[s:]
        o_ref[0, :, h * HEAD_W:(h + 1) * HEAD_W] = _diff_finish(o, g_ref, lam_init).astype(o_ref.dtype)


def _diff_attn_sample(q, k_cache, v_cache, k_new, v_new, lam_vecs, subln_g, lam_init):
    b, heads, s, _ = q.shape
    past = k_cache.shape[2]
    kern = functools.partial(_diff_attn_sample_kernel, lam_init=lam_init)
    vec_spec = _const_spec((1, DIFF_HD))
    new_spec = pl.BlockSpec((1, heads, s, HEAD_W), lambda i: (i, 0, 0, 0))
    old_spec = pl.BlockSpec((1, heads, past, HEAD_W), lambda i: (i, 0, 0, 0))
    return pl.pallas_call(
        kern,
        grid=(b,),
        in_specs=[new_spec, old_spec, old_spec, new_spec, new_spec, vec_spec, vec_spec, vec_spec,
                  vec_spec, _const_spec((1, HEAD_W))],
        out_specs=pl.BlockSpec((1, s, heads * HEAD_W), lambda i: (i, 0, 0)),
        out_shape=jax.ShapeDtypeStruct((b, s, MIX_W), BF16),
        name="diff_attn_sample",
        compiler_params=_params("arbitrary"),
    )(q, k_cache, v_cache, k_new, v_new, *lam_vecs, subln_g.reshape(1, HEAD_W))


def _post_kernel(x_ref, o_ref, m_ref, wo_ref, g1_ref, b1_ref, wg_ref, wu_ref, wd_ref, g2_ref, b2_ref,
                 y_ref, h_ref, *, ff_chunk, row_groups):
    tm = x_ref.shape[0]
    groups = [slice(r * tm // row_groups, (r + 1) * tm // row_groups) for r in range(row_groups)]
    x1, x1b = [], []
    for rows in groups:
        mixed = (_dot(o_ref[rows, :], wo_ref[0, :MIX_W, :])
                 + _dot(m_ref[rows, :], wo_ref[0, MIX_W:, :]))
        x1.append(_layer_norm_rows(ALPHA * x_ref[rows, :] + mixed, g1_ref[0], b1_ref[0]))
        x1b.append(x1[-1].astype(BF16))
    for c in range(D_FF // ff_chunk):
        cols = slice(c * ff_chunk, (c + 1) * ff_chunk)
        for r, rows in enumerate(groups):
            gate = _dot(x1b[r], wg_ref[0, :, cols])
            up = _dot(x1b[r], wu_ref[0, :, cols])
            h_ref[rows, cols] = (gate / (1.0 + jnp.exp(-gate)) * up).astype(BF16)
    for r, rows in enumerate(groups):
        ff = _dot(h_ref[rows, :], wd_ref[0])
        y_ref[rows, :] = _layer_norm_rows(ALPHA * x1[r] + ff, g2_ref[0], b2_ref[0])


def _post(x, o, m, layer, wo, g1, b1, wg, wu, wd, g2, b2, *, tm, ff_chunk, row_groups):
    n = x.shape[0]
    row = lambda w: pl.BlockSpec((tm, w), lambda i: (i, 0))
    vec = _layer_spec((1, D_MODEL), layer)
    kern = functools.partial(_post_kernel, ff_chunk=ff_chunk, row_groups=row_groups)
    r1 = lambda a: a.reshape(DEPTH, 1, D_MODEL)
    return pl.pallas_call(
        kern,
        grid=(n // tm,),
        in_specs=[row(D_MODEL), row(MIX_W), row(MEM_W), _layer_spec((D_MODEL, D_MODEL), layer), vec, vec,
                  _layer_spec((D_MODEL, D_FF), layer), _layer_spec((D_MODEL, D_FF), layer),
                  _layer_spec((D_FF, D_MODEL), layer), vec, vec],
        out_specs=row(D_MODEL),
        out_shape=jax.ShapeDtypeStruct((n, D_MODEL), F32),
        scratch_shapes=[pltpu.VMEM((tm, D_FF), BF16)],
        name="post_mixer",
        compiler_params=_params("arbitrary"),
    )(x, o, m, wo, r1(g1), r1(b1), wg, wu, wd, r1(g2), r1(b2))


def kernel(x_prompt, x_sample, mem_prompt, cache_ret_state, cache_diff_k, cache_diff_v, cache_mem_k,
           cache_mem_v, ret_w_in, ret_gn_g, diff_w_in, diff_lambda_q1, diff_lambda_k1, diff_lambda_q2,
           diff_lambda_k2, diff_subln_g, w_mem_kv, w_o, ln1_g, ln1_b, w_gate, w_up, w_down, ln2_g, ln2_b):
    bp, sp, _ = x_prompt.shape
    bs, ss, _ = x_sample.shape
    assert sp % MIXER_ROWS == 0 and MIXER_ROWS % RET_TILE == 0 and sp % ATTN_K_TILE == 0
    assert ss == CHUNK and bs % SAMPLE_GROUP == 0 and DIFF_HEADS % ATTN_HEADS == 0
    assert (bp * sp) % POST_ROWS == 0 and (bs * ss) % POST_ROWS == 0
    pos_p = jnp.arange(sp)
    pos_s = PAST_LEN + jnp.arange(ss)
    bf = lambda a: a.astype(BF16)

    mem_k_p, mem_v_p = _mem_kv(mem_prompt, bf(w_mem_kv.transpose(0, 2, 1)))
    mem_t = lambda a: a.transpose(0, 1, 3, 4, 2).reshape(DEPTH, bs, MEM_W, N_MEM)
    mem_k_s, mem_v_s = mem_t(cache_mem_k), mem_t(cache_mem_v)

    ret_w, diff_w = bf(ret_w_in), bf(diff_w_in)
    wo, wg, wu, wd = bf(w_o), bf(w_gate), bf(w_up), bf(w_down)

    xp, xs = x_prompt, x_sample
    ret_p, ret_s, dkp, dvp, dks, dvs = [], [], [], [], [], []
    to_seq_major = lambda a: a.transpose(0, 2, 1, 3)
    for i in range(DEPTH):
        j = i // 2
        if i % 2 == 0:
            r0 = jnp.zeros((bp, RET_HEADS, RET_HD, RET_HD), F32)
            op, mp, rp = _ret_mixer(xp, pos_p, r0, mem_k_p, mem_v_p, i, ret_w, j, ret_gn_g[j],
                                    t_step=MIXER_ROWS, tile=RET_TILE, group=1)
            os_, ms, rs = _ret_mixer(xs, pos_s, cache_ret_state[j], mem_k_s, mem_v_s, i, ret_w, j,
                                     ret_gn_g[j], t_step=ss, tile=ss, group=SAMPLE_GROUP)
            ret_p.append(rp)
            ret_s.append(rs)
        else:
            lam_init = _lambda_init(i)
            lam_vecs = [a[j].reshape(1, DIFF_HD) for a in
                        (diff_lambda_q1, diff_lambda_k1, diff_lambda_q2, diff_lambda_k2)]
            q, k, v, mp = _diff_project(xp, pos_p, mem_k_p, mem_v_p, i, diff_w, j, t_step=MIXER_ROWS,
                                        group=1)
            op = _diff_attn_prompt(q, k, v, lam_vecs, diff_subln_g[j], lam_init, tq=ATTN_Q_TILE,
                                   tk=ATTN_K_TILE, heads=ATTN_HEADS)
            dkp.append(to_seq_major(k))
            dvp.append(to_seq_major(v))
            q, k, v, ms = _diff_project(xs, pos_s, mem_k_s, mem_v_s, i, diff_w, j, t_step=ss,
                                        group=SAMPLE_GROUP)
            os_ = _diff_attn_sample(q, cache_diff_k[j].transpose(0, 2, 1, 3),
                                    cache_diff_v[j].transpose(0, 2, 1, 3), k, v, lam_vecs,
                                    diff_subln_g[j], lam_init)
            dks.append(to_seq_major(k))
            dvs.append(to_seq_major(v))
        post = functools.partial(_post, layer=i, wo=wo, g1=ln1_g, b1=ln1_b, wg=wg, wu=wu, wd=wd,
                                 g2=ln2_g, b2=ln2_b, tm=POST_ROWS, ff_chunk=FF_CHUNK,
                                 row_groups=POST_ROW_GROUPS)
        xp = post(xp.reshape(bp * sp, D_MODEL), op.reshape(bp * sp, MIX_W),
                  mp.reshape(bp * sp, MEM_W)).reshape(bp, sp, D_MODEL)
        xs = post(xs.reshape(bs * ss, D_MODEL), os_.reshape(bs * ss, MIX_W),
                  ms.reshape(bs * ss, MEM_W)).reshape(bs, ss, D_MODEL)

    mem_out = lambda a: a.reshape(DEPTH, bp, MEM_HEADS, MEM_HD, N_MEM).transpose(0, 1, 4, 2, 3)
    return (xp, xs, jnp.stack(ret_p), jnp.stack(ret_s), jnp.stack(dkp), jnp.stack(dvp),
            jnp.stack(dks), jnp.stack(dvs), mem_out(mem_k_p), mem_out(mem_v_p))
```
